```python
import math
import jax, jax.numpy as jnp
from jax import lax
import numpy as np

D_MODEL = 1024
BATCH = 8
SEQ = 2048
DEPTH = 2
DEC_BATCH = 128
DEC_SEQ = 1
PAST_LEN = 16384
PAGE_SIZE = 128

F32 = jnp.float32
EPS = 1e-6
CHUNK = 64
DN_HEADS = 4
DN_DK = 128
DN_DV = 128
CONV_W = 4
GLA_HEADS = 4
GLA_DK = 64
GLA_DV = 128
GLA_RANK = 16
GLA_TAU = 16.0
RW_HEADS = 8
RW_N = 64
RW_DECAY_RANK = 64
RW_A_RANK = 64
RW_LN_EPS = 64e-5

DN_QK = DN_HEADS * DN_DK
DN_V = DN_HEADS * DN_DV
DN_CONV_DIM = 2 * DN_QK + DN_V
GLA_QK = GLA_HEADS * GLA_DK
GLA_V = GLA_HEADS * GLA_DV
RW_C = RW_HEADS * RW_N
RW_SHIFT_DIM = 3 * RW_C + RW_DECAY_RANK + RW_A_RANK
D_MIX = DN_V + GLA_V + RW_C
DN_PROJ = DN_CONV_DIM + DN_V + 2 * DN_HEADS
GLA_PROJ = 2 * GLA_QK + 2 * GLA_V + GLA_RANK
RW_PROJ = RW_SHIFT_DIM + RW_C
D_PROJ = DN_PROJ + GLA_PROJ + RW_PROJ

kernel_name = 'hybrid_deltanet_gla_rwkv7_decoder_step'


def rmsnorm(x, g, eps=EPS):
    xf = x.astype(F32)
    return xf * lax.rsqrt(jnp.mean(xf * xf, axis=-1, keepdims=True) + eps) * g.astype(F32)


def l2norm(x, eps=EPS):
    return x * lax.rsqrt(jnp.sum(x * x, axis=-1, keepdims=True) + eps)


def _chunkify(x, C):
    B, T = x.shape[:2]
    n = -(-T // C)
    x = jnp.pad(x, [(0, 0), (0, n * C - T)] + [(0, 0)] * (x.ndim - 2))
    x = x.reshape((B, n, C) + x.shape[2:])
    return jnp.swapaxes(jnp.moveaxis(x, 1, 0), 2, 3)


def _unchunk(o, T):
    o = jnp.moveaxis(jnp.swapaxes(o, 2, 3), 0, 1)
    B, n, C = o.shape[:3]
    return o.reshape((B, n * C) + o.shape[3:])[:, :T]


def gated_delta_rule(q, k, v, beta, g, S0):
    T = q.shape[1]
    C = min(CHUNK, T)
    causal = jnp.tril(jnp.ones((C, C), bool))
    strict = jnp.tril(jnp.ones((C, C), bool), -1)
    eye = jnp.eye(C, dtype=F32)

    def step(S, inp):
        q_, k_, v_, b_, g_ = inp
        G = jnp.cumsum(g_, axis=-1)
        decay = jnp.exp(jnp.where(causal, G[..., :, None] - G[..., None, :], -jnp.inf))
        kb = k_ * b_[..., None]
        L = jnp.where(strict, jnp.einsum('bhik,bhjk->bhij', kb, k_) * decay, 0.0)
        rhs = jnp.concatenate([v_ * b_[..., None], kb * jnp.exp(G)[..., None]], axis=-1)
        sol = lax.linalg.triangular_solve(L + eye, rhs, left_side=True, lower=True, unit_diagonal=True)
        u, w = sol[..., :DN_DV], sol[..., DN_DV:]
        v_new = u - jnp.einsum('bhck,bhkv->bhcv', w, S)
        A = jnp.einsum('bhik,bhjk->bhij', q_, k_) * decay
        o = (jnp.einsum('bhck,bhkv->bhcv', q_ * jnp.exp(G)[..., None], S)
             + jnp.einsum('bhij,bhjv->bhiv', A, v_new))
        G_last = G[..., -1]
        S = (S * jnp.exp(G_last)[..., None, None]
             + jnp.einsum('bhck,bhcv->bhkv', k_ * jnp.exp(G_last[..., None] - G)[..., None], v_new))
        return S, o

    xs = tuple(_chunkify(a, C) for a in (q, k, v, beta, g))
    S, o = lax.scan(step, S0.astype(F32), xs)
    return _unchunk(o, T), S


def gla_chunked(q, k, v, lf, S0):
    T = q.shape[1]
    C = min(CHUNK, T)
    causal = jnp.tril(jnp.ones((C, C), bool))

    def step(S, inp):
        q_, k_, v_, f_ = inp
        G = jnp.cumsum(f_, axis=2)
        diff = G[:, :, :, None, :] - G[:, :, None, :, :]
        rel = jnp.exp(jnp.where(causal[:, :, None], diff, -jnp.inf))
        A = jnp.einsum('bhik,bhjk,bhijk->bhij', q_, k_, rel)
        o = (jnp.einsum('bhik,bhkv->bhiv', q_ * jnp.exp(G), S)
             + jnp.einsum('bhij,bhjv->bhiv', A, v_))
        G_last = G[:, :, -1:]
        S = (S * jnp.exp(G_last[:, :, 0])[..., None]
             + jnp.einsum('bhjk,bhjv->bhkv', k_ * jnp.exp(G_last - G), v_))
        return S, o

    xs = tuple(_chunkify(a, C) for a in (q, k, v, lf))
    S, o = lax.scan(step, S0.astype(F32), xs)
    return _unchunk(o, T), S


def rwkv7_scan(r, w, k, v, kk, a, S0):
    def step(S, inp):
        r_, w_, k_, v_, kk_, a_ = inp
        Skk = jnp.einsum('bhvk,bhk->bhv', S, kk_)
        S = (S * w_[:, :, None, :] - Skk[..., None] * (kk_ * a_)[:, :, None, :]
             + v_[..., None] * k_[:, :, None, :])
        return S, jnp.einsum('bhvk,bhk->bhv', S, r_)

    xs = tuple(jnp.moveaxis(t, 1, 0) for t in (r, w, k, v, kk, a))
    S, o = lax.scan(step, S0.astype(F32), xs)
    return jnp.moveaxis(o, 0, 1), S


def hybrid_layer(x, c, conv_st, dn_st, gla_st, rs_st, rw_st,
                 norm_g, ada_w, ada_b, w_in, dn_conv_w, dn_a_log, dn_dt_bias, dn_norm_g,
                 gla_wf, gla_bf, gla_norm_g, rw_mu, rw_w0, rw_w2, rw_a0, rw_a2,
                 rw_k_k, rw_k_a, rw_r_k, rw_ln_w, rw_ln_b, w_out):
    B, T, _ = x.shape
    mod = jax.nn.silu(c.astype(F32)) @ ada_w + ada_b
    shift, scale, gate = jnp.split(mod, 3, axis=-1)
    h = rmsnorm(x, norm_g) * (1.0 + scale[:, None]) + shift[:, None]
    proj = jnp.einsum('btd,de->bte', h, w_in)
    dn_p = proj[..., :DN_PROJ]
    gla_p = proj[..., DN_PROJ:DN_PROJ + GLA_PROJ]
    rw_p = proj[..., DN_PROJ + GLA_PROJ:]

    qkv = dn_p[..., :DN_CONV_DIM]
    z_dn = dn_p[..., DN_CONV_DIM:DN_CONV_DIM + DN_V].reshape(B, T, DN_HEADS, DN_DV)
    b_raw = dn_p[..., DN_CONV_DIM + DN_V:DN_CONV_DIM + DN_V + DN_HEADS]
    a_raw = dn_p[..., DN_CONV_DIM + DN_V + DN_HEADS:]
    full = jnp.concatenate([conv_st.astype(F32), qkv], axis=1)
    conv = full[:, :T] * dn_conv_w[0]
    for j in range(1, CONV_W):
        conv = conv + full[:, j:j + T] * dn_conv_w[j]
    conv = jax.nn.silu(conv)
    conv_new = full[:, T:]
    dq = l2norm(conv[..., :DN_QK].reshape(B, T, DN_HEADS, DN_DK)) * (DN_DK ** -0.5)
    dk = l2norm(conv[..., DN_QK:2 * DN_QK].reshape(B, T, DN_HEADS, DN_DK))
    dv = conv[..., 2 * DN_QK:].reshape(B, T, DN_HEADS, DN_DV)
    beta = jax.nn.sigmoid(b_raw)
    g = -jnp.exp(dn_a_log.astype(F32)) * jax.nn.softplus(a_raw + dn_dt_bias)
    o_dn, dn_new = gated_delta_rule(dq, dk, dv, beta, g, dn_st)
    o_dn = (rmsnorm(o_dn, dn_norm_g) * jax.nn.silu(z_dn)).reshape(B, T, DN_V)

    gq = gla_p[..., :GLA_QK].reshape(B, T, GLA_HEADS, GLA_DK) * (GLA_DK ** -0.5)
    gk = gla_p[..., GLA_QK:2 * GLA_QK].reshape(B, T, GLA_HEADS, GLA_DK)
    gv = gla_p[..., 2 * GLA_QK:2 * GLA_QK + GLA_V].reshape(B, T, GLA_HEADS, GLA_DV)
    gz = gla_p[..., 2 * GLA_QK + GLA_V:2 * GLA_QK + 2 * GLA_V].reshape(B, T, GLA_HEADS, GLA_DV)
    glo = gla_p[..., 2 * GLA_QK + 2 * GLA_V:]
    lf = (jax.nn.log_sigmoid(glo @ gla_wf + gla_bf) / GLA_TAU).reshape(B, T, GLA_HEADS, GLA_DK)
    o_gla, gla_new = gla_chunked(gq, gk, gv, lf, gla_st)
    o_gla = (rmsnorm(o_gla, gla_norm_g) * jax.nn.silu(gz)).reshape(B, T, GLA_V)

    xs = rw_p[..., :RW_SHIFT_DIM]
    rz = rw_p[..., RW_SHIFT_DIM:]
    prev = jnp.concatenate([rs_st[:, None].astype(F32), xs[:, :-1]], axis=1)
    rs_new = xs[:, -1]
    xm = xs + (prev - xs) * rw_mu
    r = xm[..., :RW_C]
    k = xm[..., RW_C:2 * RW_C]
    v = xm[..., 2 * RW_C:3 * RW_C]
    wlo = xm[..., 3 * RW_C:3 * RW_C + RW_DECAY_RANK]
    alo = xm[..., 3 * RW_C + RW_DECAY_RANK:]
    w_log = -jax.nn.softplus(-(rw_w0 + jnp.tanh(wlo) @ rw_w2)) - 0.5
    decay = jnp.exp(-jnp.exp(w_log))
    a = jax.nn.sigmoid(rw_a0 + alo @ rw_a2)
    hs = (B, T, RW_HEADS, RW_N)
    kk = l2norm((k * rw_k_k).reshape(hs))
    k = k * (1.0 + (a - 1.0) * rw_k_a)
    r, k, v, decay, a = (t.reshape(hs) for t in (r, k, v, decay, a))
    o_rw, rw_new = rwkv7_scan(r, decay, k, v, kk, a, rw_st)
    mu = jnp.mean(o_rw, axis=-1, keepdims=True)
    var = jnp.mean(jnp.square(o_rw - mu), axis=-1, keepdims=True)
    o_rw = ((o_rw - mu) * lax.rsqrt(var + RW_LN_EPS)).reshape(B, T, RW_C) * rw_ln_w + rw_ln_b
    bonus = (jnp.sum(r * k * rw_r_k, axis=-1, keepdims=True) * v).reshape(B, T, RW_C)
    o_rw = (o_rw + bonus) * jax.nn.silu(rz)

    o = jnp.concatenate([o_dn, o_gla, o_rw], axis=-1)
    x = x + gate[:, None] * jnp.einsum('bte,ed->btd', o, w_out)
    return x, (conv_new, dn_new, gla_new, rs_new, rw_new)


def setup_inputs(seed: int = 0) -> dict:
    key = jax.random.key(seed)
    ks = jax.random.split(key, 32)

    def nrm(i, shape, s):
        return s * jax.random.normal(ks[i], shape, F32)

    def uni(i, shape, lo, hi):
        return jax.random.uniform(ks[i], shape, F32, lo, hi)

    L, D = DEPTH, D_MODEL
    dt = jnp.exp(uni(15, (L, DN_HEADS), math.log(1e-3), math.log(1e-1)))
    return {
        'x_prompt': nrm(0, (BATCH, SEQ, D), 1.0),
        'x_sample': nrm(1, (DEC_BATCH, DEC_SEQ, D), 1.0),
        'c_prompt': nrm(2, (BATCH, D), 1.0),
        'c_sample': nrm(3, (DEC_BATCH, D), 1.0),
        'state_dn_conv': nrm(4, (L, DEC_BATCH, CONV_W - 1, DN_CONV_DIM), 1.0),
        'state_dn': nrm(5, (L, DEC_BATCH, DN_HEADS, DN_DK, DN_DV), 0.05),
        'state_gla': nrm(6, (L, DEC_BATCH, GLA_HEADS, GLA_DK, GLA_DV), 0.1),
        'state_rwkv_shift': nrm(7, (L, DEC_BATCH, RW_SHIFT_DIM), 1.0),
        'state_rwkv': nrm(8, (L, DEC_BATCH, RW_HEADS, RW_N, RW_N), 0.05),
        'norm_g': 1.0 + nrm(9, (L, D), 0.05),
        'ada_w': nrm(10, (L, D, 3 * D), 0.5 * D ** -0.5),
        'ada_b': nrm(11, (L, 3 * D), 0.02),
        'w_in': nrm(12, (L, D, D_PROJ), D ** -0.5),
        'dn_conv_w': nrm(13, (L, CONV_W, DN_CONV_DIM), CONV_W ** -0.5),
        'dn_a_log': jnp.log(uni(14, (L, DN_HEADS), 1.0, 16.0)),
        'dn_dt_bias': dt + jnp.log(-jnp.expm1(-dt)),
        'dn_norm_g': 1.0 + nrm(16, (L, DN_DV), 0.05),
        'gla_wf': nrm(17, (L, GLA_RANK, GLA_QK), GLA_RANK ** -0.5),
        'gla_bf': 1.0 + nrm(18, (L, GLA_QK), 0.5),
        'gla_norm_g': 1.0 + nrm(19, (L, GLA_DV), 0.05),
        'rw_mu': uni(20, (L, RW_SHIFT_DIM), 0.0, 1.0),
        'rw_w0': uni(21, (L, RW_C), -6.0, -1.0),
        'rw_w2': nrm(22, (L, RW_DECAY_RANK, RW_C), 0.1 * RW_DECAY_RANK ** -0.5),
        'rw_a0': nrm(23, (L, RW_C), 0.1),
        'rw_a2': nrm(24, (L, RW_A_RANK, RW_C), RW_A_RANK ** -0.5),
        'rw_k_k': 0.85 + nrm(25, (L, RW_C), 0.05),
        'rw_k_a': 1.0 + nrm(26, (L, RW_C), 0.05),
        'rw_r_k': nrm(27, (L, RW_HEADS, RW_N), 0.1),
        'rw_ln_w': 1.0 + nrm(28, (L, RW_C), 0.05),
        'rw_ln_b': nrm(29, (L, RW_C), 0.02),
        'w_out': nrm(30, (L, D_MIX, D), D_MIX ** -0.5),
        'final_norm_g': 1.0 + nrm(31, (D,), 0.05),
    }


def reference(x_prompt, x_sample, c_prompt, c_sample, state_dn_conv, state_dn, state_gla,
              state_rwkv_shift, state_rwkv, norm_g, ada_w, ada_b, w_in, dn_conv_w, dn_a_log,
              dn_dt_bias, dn_norm_g, gla_wf, gla_bf, gla_norm_g, rw_mu, rw_w0, rw_w2, rw_a0,
              rw_a2, rw_k_k, rw_k_a, rw_r_k, rw_ln_w, rw_ln_b, w_out, final_norm_g):
    def trunk(x, c, conv_s, dn_s, gla_s, rs_s, rw_s):
        hcur = x.astype(F32)
        outs = []
        for l in range(DEPTH):
            hcur, st = hybrid_layer(
                hcur, c, conv_s[l], dn_s[l], gla_s[l], rs_s[l], rw_s[l],
                norm_g[l], ada_w[l], ada_b[l], w_in[l], dn_conv_w[l], dn_a_log[l], dn_dt_bias[l],
                dn_norm_g[l], gla_wf[l], gla_bf[l], gla_norm_g[l], rw_mu[l], rw_w0[l], rw_w2[l],
                rw_a0[l], rw_a2[l], rw_k_k[l], rw_k_a[l], rw_r_k[l], rw_ln_w[l], rw_ln_b[l], w_out[l])
            outs.append(st)
        y = rmsnorm(hcur, final_norm_g).astype(x.dtype)
        n_conv = jnp.stack([s[0] for s in outs])
        n_dn = jnp.stack([s[1] for s in outs])
        n_gla = jnp.stack([s[2] for s in outs])
        n_rs = jnp.stack([s[3] for s in outs])
        n_rw = jnp.stack([s[4] for s in outs])
        return y, n_conv, n_dn, n_gla, n_rs, n_rw

    Bp = x_prompt.shape[0]
    y_prompt, p_conv, p_dn, p_gla, p_rs, p_rw = trunk(
        x_prompt, c_prompt,
        jnp.zeros((DEPTH, Bp, CONV_W - 1, DN_CONV_DIM), F32),
        jnp.zeros((DEPTH, Bp, DN_HEADS, DN_DK, DN_DV), F32),
        jnp.zeros((DEPTH, Bp, GLA_HEADS, GLA_DK, GLA_DV), F32),
        jnp.zeros((DEPTH, Bp, RW_SHIFT_DIM), F32),
        jnp.zeros((DEPTH, Bp, RW_HEADS, RW_N, RW_N), F32))
    y_sample, s_conv, s_dn, s_gla, s_rs, s_rw = trunk(
        x_sample, c_sample, state_dn_conv, state_dn, state_gla, state_rwkv_shift, state_rwkv)
    return (y_prompt, y_sample, p_conv, p_dn, p_gla, p_rs, p_rw, s_conv, s_dn, s_gla, s_rs, s_rw)
```

```python
import functools

import jax
import jax.numpy as jnp
from jax import lax
from jax.experimental import pallas as pl
from jax.experimental.pallas import tpu as pltpu

F32 = jnp.float32
BF16 = jnp.bfloat16
HIGHEST = lax.Precision.HIGHEST

D_MODEL = 1024
EPS = 1e-6
CHUNK = 64
SUB = 16
NEG = -1e30

DN_HEADS, DN_DK, DN_DV = 4, 128, 128
GLA_HEADS, GLA_DK, GLA_DV, GLA_RANK, GLA_TAU = 4, 64, 128, 16, 16.0
RW_HEADS, RW_N = 8, 64
RW_LN_EPS = 64e-5
DN_QKV = 1536
RW_C = 512
RW_SHIFT = 1664
DN_SEG, GLA_SEG, RW_SEG = 2176, 1664, 2176
DN_PROJ, GLA_PROJ = 2056, 1552
LANE = 128
VMEM_LIMIT = 56 * 1024 * 1024


def _dot(a, b):
    return jnp.dot(a.astype(BF16), b.astype(BF16), preferred_element_type=F32)


def _dot_nt(a, b):
    return lax.dot_general(a.astype(BF16), b.astype(BF16), (((1,), (1,)), ((), ())),
                           preferred_element_type=F32)


def _dot_tn(a, b):
    return lax.dot_general(a.astype(BF16), b.astype(BF16), (((0,), (0,)), ((), ())),
                           preferred_element_type=F32)


def _dot_hi(a, b):
    return jnp.dot(a, b, precision=HIGHEST, preferred_element_type=F32)


def _dot_nt_hi(a, b):
    return lax.dot_general(a, b, (((1,), (1,)), ((), ())), precision=HIGHEST,
                           preferred_element_type=F32)


def _sigmoid(x):
    return jax.nn.sigmoid(x)


def _silu(x):
    return x * jax.nn.sigmoid(x)


def _softplus(x):
    return jnp.maximum(x, 0.0) + jnp.log1p(jnp.exp(-jnp.abs(x)))


def _iota2(shape, dim):
    return lax.broadcasted_iota(jnp.int32, shape, dim)


def _modulated_norm(x, ng, shift, scale):
    ms = jnp.mean(x * x, axis=-1, keepdims=True)
    return x * lax.rsqrt(ms + EPS) * ng * (1.0 + scale) + shift


def _l2norm_rows(x):
    return x * lax.rsqrt(jnp.sum(x * x, axis=-1, keepdims=True) + EPS)


def _rms_rows(x):
    return x * lax.rsqrt(jnp.mean(x * x, axis=-1, keepdims=True) + EPS)


def _seg_sum(x, seg_ones):
    hi = x.astype(BF16)
    lo = (x - hi.astype(F32)).astype(BF16)
    return (jnp.dot(hi, seg_ones, preferred_element_type=F32)
            + jnp.dot(lo, seg_ones, preferred_element_type=F32))


def _tri_masks(n=CHUNK):
    r, c = _iota2((n, n), 0), _iota2((n, n), 1)
    return dict(
        causal=r >= c, strict=r > c, eye=(r == c).astype(F32),
        bd16=(r >> 4) == (c >> 4),
        off32=((r >> 5) == (c >> 5)) & ((r >> 4) != (c >> 4)),
        off64=(r >> 5) != (c >> 5))


def _unit_lower_inverse(L, m):
    Ld = jnp.where(m["bd16"], L, 0.0)
    L2 = _dot_hi(Ld, Ld)
    L4 = _dot_hi(L2, L2)
    L8 = _dot_hi(L4, L4)
    X = m["eye"] - Ld
    X = X + _dot_hi(X, L2)
    X = X + _dot_hi(X, L4)
    X = X + _dot_hi(X, L8)
    C1 = jnp.where(m["off32"], L, 0.0)
    X = X - _dot_hi(_dot_hi(X, C1), X)
    C2 = jnp.where(m["off64"], L, 0.0)
    X = X - _dot_hi(_dot_hi(X, C2), X)
    return X


def _col_from_row(row, n, lane_off=0):
    w = row.shape[1]
    sel = (_iota2((n, w), 0) + lane_off) == _iota2((n, w), 1)
    return jnp.sum(jnp.where(sel, row, 0.0), axis=1, keepdims=True)


def _params(n_axes):
    return pltpu.CompilerParams(dimension_semantics=("arbitrary",) * n_axes,
                                vmem_limit_bytes=VMEM_LIMIT)


def _mod_kernel(c_ref, w_ref, b_ref, o_ref):
    o_ref[0] = _dot(_silu(c_ref[...]), w_ref[0]) + b_ref[0]


def _modulation(c_all, ada_w, ada_b):
    depth, d, n = ada_w.shape
    rows = c_all.shape[0]
    tn = 1024
    return pl.pallas_call(
        _mod_kernel,
        grid=(depth, n // tn),
        in_specs=[pl.BlockSpec((rows, d), lambda l, j: (0, 0)),
                  pl.BlockSpec((1, d, tn), lambda l, j: (l, 0, j)),
                  pl.BlockSpec((1, 1, tn), lambda l, j: (l, 0, j))],
        out_specs=pl.BlockSpec((1, rows, tn), lambda l, j: (l, 0, j)),
        out_shape=jax.ShapeDtypeStruct((depth, rows, n), F32),
        compiler_params=_params(2),
        name="adaln_mod",
    )(c_all, ada_w, ada_b.reshape(depth, 1, n))


def _dn_prompt_kernel(x_ref, mod_ref, ng_ref, w_ref, cw_ref, gp_ref, dng_ref,
                      o_ref, conv_ref, s_ref,
                      pbuf, qs, ks, vs, gb, S):
    t = pl.program_id(1)
    tb = x_ref.shape[1]
    m = _tri_masks()
    sel8 = (_iota2((8, LANE), 0) == _iota2((8, LANE), 1)).astype(F32)
    ltri = m["causal"].astype(F32)

    @pl.when(t == 0)
    def _():
        pbuf[0:8, :] = jnp.zeros((8, DN_SEG), F32)
        S[...] = jnp.zeros_like(S)

    h = _modulated_norm(x_ref[0], ng_ref[...], mod_ref[0, 0:1, :], mod_ref[0, 1:2, :])
    pbuf[8:8 + tb, :] = _dot(h, w_ref[...])

    conv = pbuf[5:5 + tb, 0:DN_QKV] * cw_ref[0:1, :]
    for j in range(1, 4):
        conv = conv + pbuf[5 + j:5 + j + tb, 0:DN_QKV] * cw_ref[j:j + 1, :]
    conv = _silu(conv)
    for hh in range(DN_HEADS):
        lo = hh * DN_DK
        qs[:, lo:lo + DN_DK] = _l2norm_rows(conv[:, lo:lo + DN_DK]) * (DN_DK ** -0.5)
        ks[:, lo:lo + DN_DK] = _l2norm_rows(conv[:, 512 + lo:512 + lo + DN_DK])
    vs[...] = conv[:, 1024:DN_QKV]
    gseg = pbuf[8:8 + tb, 2048:DN_SEG]
    lane = _iota2((tb, LANE), 1)
    gb[...] = jnp.where(lane < DN_HEADS, _sigmoid(gseg),
                        -jnp.exp(gp_ref[0:1, :]) * _softplus(gseg + gp_ref[1:2, :]))

    def chunk(c, carry):
        r0 = pl.multiple_of(c * CHUNK, CHUNK)
        rows = pl.ds(r0, CHUNK)
        gbc = gb[rows, :]
        g_col_all = _dot_hi(ltri, gbc)
        g_row_all = _dot_nt_hi(sel8, g_col_all)
        for hh in range(DN_HEADS):
            lo = hh * DN_DK
            beta = gbc[:, hh:hh + 1]
            g_col = g_col_all[:, DN_HEADS + hh:DN_HEADS + hh + 1]
            g_row = g_row_all[DN_HEADS + hh:DN_HEADS + hh + 1, :]
            g_last = g_col_all[CHUNK - 1:CHUNK, DN_HEADS + hh:DN_HEADS + hh + 1]
            q = qs[rows, lo:lo + DN_DK]
            k = ks[rows, lo:lo + DN_DK]
            v = vs[rows, lo:lo + DN_DV]
            decay = jnp.exp(jnp.where(m["causal"], g_col - g_row, NEG))
            kb = k * beta
            kq = _dot_nt(jnp.concatenate([kb, q], axis=0), k)
            L = jnp.where(m["strict"], kq[0:CHUNK] * decay, 0.0)
            A = kq[CHUNK:2 * CHUNK] * decay
            X = _unit_lower_inverse(L, m)
            e_g = jnp.exp(g_col)
            sol = _dot_hi(X, jnp.concatenate([v * beta, kb * e_g], axis=1))
            u, w = sol[:, 0:DN_DV], sol[:, DN_DV:]
            s_h = S[hh]
            ws = _dot(jnp.concatenate([w, q * e_g], axis=0), s_h)
            v_new = u - ws[0:CHUNK]
            o = ws[CHUNK:2 * CHUNK] + _dot(A, v_new)
            S[hh] = s_h * jnp.exp(g_last) + _dot_tn(k * jnp.exp(g_last - g_col), v_new)
            z = pbuf[pl.ds(pl.multiple_of(r0 + 8, 8), CHUNK), DN_QKV + lo:DN_QKV + lo + DN_DV]
            o_ref[0, rows, lo:lo + DN_DV] = (_rms_rows(o) * dng_ref[...] * _silu(z)).astype(o_ref.dtype)
        return carry

    lax.fori_loop(0, tb // CHUNK, chunk, 0)

    pbuf[0:8, :] = pbuf[tb:tb + 8, :]

    @pl.when(t == pl.num_programs(1) - 1)
    def _():
        conv_ref[0] = pbuf[tb + 5:tb + 8, 0:DN_QKV]
        s_ref[0] = S[...]


def _dn_prompt(x, mod, ng, w_seg, cw, gp, dng, tb):
    b, t, d = x.shape
    return pl.pallas_call(
        _dn_prompt_kernel,
        grid=(b, t // tb),
        in_specs=[pl.BlockSpec((1, tb, d), lambda i, j: (i, j, 0)),
                  pl.BlockSpec((1, 3, d), lambda i, j: (i, 0, 0)),
                  pl.BlockSpec((1, d), lambda i, j: (0, 0)),
                  pl.BlockSpec((d, DN_SEG), lambda i, j: (0, 0)),
                  pl.BlockSpec((4, DN_QKV), lambda i, j: (0, 0)),
                  pl.BlockSpec((2, LANE), lambda i, j: (0, 0)),
                  pl.BlockSpec((1, DN_DV), lambda i, j: (0, 0))],
        out_specs=[pl.BlockSpec((1, tb, 512), lambda i, j: (i, j, 0)),
                   pl.BlockSpec((1, 3, DN_QKV), lambda i, j: (i, 0, 0)),
                   pl.BlockSpec((1, DN_HEADS, DN_DK, DN_DV), lambda i, j: (i, 0, 0, 0))],
        out_shape=[jax.ShapeDtypeStruct((b, t, 512), BF16),
                   jax.ShapeDtypeStruct((b, 3, DN_QKV), F32),
                   jax.ShapeDtypeStruct((b, DN_HEADS, DN_DK, DN_DV), F32)],
        scratch_shapes=[pltpu.VMEM((tb + 8, DN_SEG), F32),
                        pltpu.VMEM((tb, 512), F32), pltpu.VMEM((tb, 512), F32),
                        pltpu.VMEM((tb, 512), F32), pltpu.VMEM((tb, LANE), F32),
                        pltpu.VMEM((DN_HEADS, DN_DK, DN_DV), F32)],
        compiler_params=_params(2),
        name="dn_prompt",
    )(x, mod, ng, w_seg, cw, gp, dng)


def _gla_prompt_kernel(x_ref, mod_ref, ng_ref, w_ref, wf_ref, bf_ref, gng_ref,
                       o_ref, s_ref,
                       pbuf, lfs, ST):
    t = pl.program_id(1)
    tb = x_ref.shape[1]
    m = _tri_masks()
    ltri = m["causal"].astype(F32)
    n_sub = CHUNK // SUB
    lane = _iota2((CHUNK, LANE), 1)
    head1 = lane >= GLA_DK
    rowblk = _iota2((CHUNK, LANE), 0) >> 4
    rowloc = _iota2((CHUNK, LANE), 0) & (SUB - 1)
    er = _iota2((2 * SUB, SUB * LANE), 0)
    ec = _iota2((2 * SUB, SUB * LANE), 1)
    eblk = (((ec >> 7) == (er & (SUB - 1))) & (((ec & (LANE - 1)) >= GLA_DK) == (er >= SUB))).astype(BF16)
    st_r = _iota2((2 * GLA_DV, LANE), 0)
    st_c = _iota2((2 * GLA_DV, LANE), 1)
    st_mask = (st_r >= GLA_DV) == (st_c >= GLA_DK)

    @pl.when(t == 0)
    def _():
        ST[...] = jnp.zeros_like(ST)

    h = _modulated_norm(x_ref[0], ng_ref[...], mod_ref[0, 0:1, :], mod_ref[0, 1:2, :])
    pbuf[...] = _dot(h, w_ref[...])
    pre = _dot(pbuf[:, 1536:GLA_SEG], wf_ref[...]) + bf_ref[...]
    lfs[...] = (jnp.minimum(pre, 0.0) - jnp.log1p(jnp.exp(-jnp.abs(pre)))) * (1.0 / GLA_TAU)

    def chunk(c, carry):
        r0 = pl.multiple_of(c * CHUNK, CHUNK)
        rows = pl.ds(r0, CHUNK)
        g_all = _dot_hi(ltri, lfs[rows, :])
        for p in range(GLA_HEADS // 2):
            lo = p * LANE
            G = g_all[:, lo:lo + LANE]
            q = pbuf[rows, lo:lo + LANE] * (GLA_DK ** -0.5)
            k = pbuf[rows, 256 + lo:256 + lo + LANE]
            v = pbuf[rows, 512 + 2 * lo:512 + 2 * lo + 2 * GLA_DV]
            st = ST[p]
            o_inter = _dot_nt(q * jnp.exp(G), st)

            G3 = G.reshape(n_sub, SUB, LANE)
            q3 = q.reshape(n_sub, SUB, LANE)

            def pick(a3, i):
                return jnp.broadcast_to(a3[:, i:i + 1, :], (n_sub, SUB, LANE)).reshape(CHUNK, LANE)

            ref_row = pick(G3, 0)
            q_t = q * jnp.exp(G - ref_row)
            q_parts, k_parts = [], []
            for i in range(1, n_sub):
                q_parts.append(jnp.where(rowblk == i, q_t, 0.0))
                ref_i = G[i * SUB:i * SUB + 1, :]
                k_parts.append(k * jnp.exp(jnp.where(rowblk < i, ref_i - G, NEG)))
            q_big = jnp.concatenate(q_parts, axis=1)
            k_big = jnp.concatenate(k_parts, axis=1)
            head1_big = jnp.concatenate([head1] * (n_sub - 1), axis=1)
            q_both = jnp.concatenate([jnp.where(head1_big, 0.0, q_big),
                                      jnp.where(head1_big, q_big, 0.0)], axis=0)
            a_off = _dot_nt(q_both, k_big)

            slabs = []
            for i in range(SUB):
                dec = jnp.exp(jnp.where(rowloc <= i, pick(G3, i) - G, NEG))
                slabs.append((k * pick(q3, i) * dec).astype(BF16))
            p_stack = jnp.concatenate(slabs, axis=1)
            a_diag = lax.dot_general(eblk, p_stack, (((1,), (1,)), ((), ())),
                                     preferred_element_type=F32)

            kv = _dot_tn(v, k * jnp.exp(G[CHUNK - 1:CHUNK, :] - G))
            ST[p] = st * jnp.exp(G[CHUNK - 1:CHUNK, :]) + jnp.where(st_mask, kv, 0.0)

            for hl in range(2):
                hh = 2 * p + hl
                a_d = a_diag[hl * SUB:(hl + 1) * SUB, :]
                a_h = a_off[hl * CHUNK:(hl + 1) * CHUNK, :] + jnp.where(
                    m["bd16"], jnp.concatenate([a_d] * n_sub, axis=0), 0.0)
                v_h = v[:, hl * GLA_DV:(hl + 1) * GLA_DV]
                o = o_inter[:, hl * GLA_DV:(hl + 1) * GLA_DV] + _dot(a_h, v_h)
                z = pbuf[rows, 1024 + hh * GLA_DV:1024 + (hh + 1) * GLA_DV]
                o_ref[0, rows, hh * GLA_DV:(hh + 1) * GLA_DV] = (
                    _rms_rows(o) * gng_ref[...] * _silu(z)).astype(o_ref.dtype)
        return carry

    lax.fori_loop(0, tb // CHUNK, chunk, 0)

    @pl.when(t == pl.num_programs(1) - 1)
    def _():
        s_ref[0] = ST[...]


def _gla_prompt(x, mod, ng, w_seg, wf, bf, gng, tb):
    b, t, d = x.shape
    return pl.pallas_call(
        _gla_prompt_kernel,
        grid=(b, t // tb),
        in_specs=[pl.BlockSpec((1, tb, d), lambda i, j: (i, j, 0)),
                  pl.BlockSpec((1, 3, d), lambda i, j: (i, 0, 0)),
                  pl.BlockSpec((1, d), lambda i, j: (0, 0)),
                  pl.BlockSpec((d, GLA_SEG), lambda i, j: (0, 0)),
                  pl.BlockSpec((LANE, 256), lambda i, j: (0, 0)),
                  pl.BlockSpec((1, 256), lambda i, j: (0, 0)),
                  pl.BlockSpec((1, GLA_DV), lambda i, j: (0, 0))],
        out_specs=[pl.BlockSpec((1, tb, 512), lambda i, j: (i, j, 0)),
                   pl.BlockSpec((1, 2, 2 * GLA_DV, LANE), lambda i, j: (i, 0, 0, 0))],
        out_shape=[jax.ShapeDtypeStruct((b, t, 512), BF16),
                   jax.ShapeDtypeStruct((b, 2, 2 * GLA_DV, LANE), F32)],
        scratch_shapes=[pltpu.VMEM((tb, GLA_SEG), F32), pltpu.VMEM((tb, 256), F32),
                        pltpu.VMEM((2, 2 * GLA_DV, LANE), F32)],
        compiler_params=_params(2),
        name="gla_prompt",
    )(x, mod, ng, w_seg, wf, bf, gng)


def _rw_features(xm, rp_ref, w2_ref, a2_ref, seg_ones):
    r = xm[:, 0:RW_C]
    k = xm[:, RW_C:2 * RW_C]
    v = xm[:, 2 * RW_C:3 * RW_C]
    lo = xm[:, 3 * RW_C:RW_SHIFT]
    w_log = -_softplus(-(rp_ref[0:1, :] + _dot(jnp.tanh(lo), w2_ref[...]))) - 0.5
    logw = -jnp.exp(w_log)
    a = _sigmoid(rp_ref[1:2, :] + _dot(lo, a2_ref[...]))
    kk = k * rp_ref[2:3, :]
    kk = kk * lax.rsqrt(_seg_sum(kk * kk, seg_ones) + EPS)
    k = k * (1.0 + (a - 1.0) * rp_ref[3:4, :])
    return r, k, v, kk, kk * a, logw


def _rw_finish(o, r, k, v, rz, rp_ref, seg_ones):
    mu = _seg_sum(o, seg_ones) * (1.0 / RW_N)
    dev = o - mu
    var = _seg_sum(dev * dev, seg_ones) * (1.0 / RW_N)
    on = dev * lax.rsqrt(var + RW_LN_EPS) * rp_ref[5:6, :] + rp_ref[6:7, :]
    bonus = _seg_sum(r * k * rp_ref[4:5, :], seg_ones) * v
    return (on + bonus) * _silu(rz)


def _seg_ones():
    return ((_iota2((RW_C, RW_C), 0) >> 6) == (_iota2((RW_C, RW_C), 1) >> 6)).astype(BF16)


def _rw_prompt_kernel(x_ref, mod_ref, ng_ref, w_ref, mu_ref, rp_ref, w2_ref, a2_ref,
                      o_ref, sh_ref, s_ref,
                      pbuf, rs, ks, vs, kks, bs, lws, os_, S):
    t = pl.program_id(1)
    tb = x_ref.shape[1]
    m = _tri_masks()
    ltri = m["causal"].astype(F32)
    seg_ones = _seg_ones()
    lane = _iota2((CHUNK, LANE), 1)
    head1 = lane >= RW_N
    bd = (_iota2((LANE, LANE), 0) >= RW_N) == (_iota2((LANE, LANE), 1) >= RW_N)
    mid = CHUNK // 2

    @pl.when(t == 0)
    def _():
        pbuf[0:8, :] = jnp.zeros((8, RW_SEG), F32)
        S[...] = jnp.zeros_like(S)

    h = _modulated_norm(x_ref[0], ng_ref[...], mod_ref[0, 0:1, :], mod_ref[0, 1:2, :])
    pbuf[8:8 + tb, :] = _dot(h, w_ref[...])
    xs = pbuf[8:8 + tb, 0:RW_SHIFT]
    prev = pbuf[7:7 + tb, 0:RW_SHIFT]
    xm = xs + (prev - xs) * mu_ref[...]
    r, k, v, kk, b, logw = _rw_features(xm, rp_ref, w2_ref, a2_ref, seg_ones)
    rs[...] = r
    ks[...] = k
    vs[...] = v
    kks[...] = kk
    bs[...] = b
    lws[...] = logw

    def chunk(c, carry):
        r0 = pl.multiple_of(c * CHUNK, CHUNK)
        rows = pl.ds(r0, CHUNK)
        lw_all = lws[rows, :]
        g_all = _dot_hi(ltri, lw_all)
        for p in range(RW_HEADS // 2):
            lo = p * LANE
            sl = slice(lo, lo + LANE)
            G = g_all[:, sl]
            Gx = G - lw_all[:, sl]
            ref = G[mid:mid + 1, :]
            g_end = G[CHUNK - 1:CHUNK, :]
            r_, k_, v_, kk_, b_ = rs[rows, sl], ks[rows, sl], vs[rows, sl], kks[rows, sl], bs[rows, sl]
            e_out = jnp.exp(ref - G)
            r_rel = r_ * jnp.exp(G - ref)
            kk_rel = kk_ * jnp.exp(Gx - ref)
            k_rel = k_ * e_out
            b_rel = b_ * e_out
            s_p = S[p]
            kh = _dot_nt(jnp.concatenate([kk_ * jnp.exp(Gx), r_ * jnp.exp(G)], axis=0), s_p)

            lhs = jnp.concatenate([jnp.where(head1, 0.0, kk_rel), jnp.where(head1, 0.0, r_rel),
                                   jnp.where(head1, kk_rel, 0.0), jnp.where(head1, r_rel, 0.0)], axis=0)
            m_b = _dot_nt(lhs, b_rel)
            m_k = _dot_nt(lhs, k_rel)
            u_heads, a_heads = [], []
            for hl in range(2):
                base = 2 * hl * CHUNK
                Lb = jnp.where(m["strict"], m_b[base:base + CHUNK], 0.0)
                Mk = jnp.where(m["strict"], m_k[base:base + CHUNK], 0.0)
                Arb = jnp.where(m["causal"], m_b[base + CHUNK:base + 2 * CHUNK], 0.0)
                Ark = jnp.where(m["causal"], m_k[base + CHUNK:base + 2 * CHUNK], 0.0)
                X = _unit_lower_inverse(Lb, m)
                u_heads.append(_dot_hi(X, kh[0:CHUNK] + _dot(Mk, v_)))
                a_heads.append(jnp.concatenate([Ark, -Arb], axis=1))
            u_ = jnp.where(head1, u_heads[1], u_heads[0])
            vu = jnp.concatenate([v_, u_], axis=0)
            o = kh[CHUNK:2 * CHUNK] + jnp.where(head1, _dot(a_heads[1], vu), _dot(a_heads[0], vu))
            os_[rows, sl] = o
            e_end = jnp.exp(g_end - G)
            kb_end = jnp.concatenate([k_ * e_end, -(b_ * e_end)], axis=0)
            S[p] = s_p * jnp.exp(g_end) + jnp.where(bd, _dot_tn(vu, kb_end), 0.0)
        return carry

    lax.fori_loop(0, tb // CHUNK, chunk, 0)

    rz = pbuf[8:8 + tb, RW_SHIFT:RW_SEG]
    o_ref[0] = _rw_finish(os_[...], rs[...], ks[...], vs[...], rz, rp_ref, seg_ones).astype(o_ref.dtype)
    pbuf[0:8, :] = pbuf[tb:tb + 8, :]

    @pl.when(t == pl.num_programs(1) - 1)
    def _():
        sh_ref[0] = pbuf[tb + 7:tb + 8, 0:RW_SHIFT]
        s_ref[0] = S[...]


def _rw_prompt(x, mod, ng, w_seg, mu, rp, w2, a2, tb):
    b, t, d = x.shape
    return pl.pallas_call(
        _rw_prompt_kernel,
        grid=(b, t // tb),
        in_specs=[pl.BlockSpec((1, tb, d), lambda i, j: (i, j, 0)),
                  pl.BlockSpec((1, 3, d), lambda i, j: (i, 0, 0)),
                  pl.BlockSpec((1, d), lambda i, j: (0, 0)),
                  pl.BlockSpec((d, RW_SEG), lambda i, j: (0, 0)),
                  pl.BlockSpec((1, RW_SHIFT), lambda i, j: (0, 0)),
                  pl.BlockSpec((8, RW_C), lambda i, j: (0, 0)),
                  pl.BlockSpec((LANE, RW_C), lambda i, j: (0, 0)),
                  pl.BlockSpec((LANE, RW_C), lambda i, j: (0, 0))],
        out_specs=[pl.BlockSpec((1, tb, RW_C), lambda i, j: (i, j, 0)),
                   pl.BlockSpec((1, 1, RW_SHIFT), lambda i, j: (i, 0, 0)),
                   pl.BlockSpec((1, RW_HEADS // 2, LANE, LANE), lambda i, j: (i, 0, 0, 0))],
        out_shape=[jax.ShapeDtypeStruct((b, t, RW_C), BF16),
                   jax.ShapeDtypeStruct((b, 1, RW_SHIFT), F32),
                   jax.ShapeDtypeStruct((b, RW_HEADS // 2, LANE, LANE), F32)],
        scratch_shapes=[pltpu.VMEM((tb + 8, RW_SEG), F32)]
        + [pltpu.VMEM((tb, RW_C), F32) for _ in range(7)]
        + [pltpu.VMEM((RW_HEADS // 2, LANE, LANE), F32)],
        compiler_params=_params(2),
        name="rw_prompt",
    )(x, mod, ng, w_seg, mu, rp, w2, a2)


def _outproj_kernel(od_ref, og_ref, or_ref, x_ref, gate_ref, w_ref, fg_ref, y_ref, *, final):
    acc = (_dot(od_ref[0], w_ref[0:512, :]) + _dot(og_ref[0], w_ref[512:1024, :])
           + _dot(or_ref[0], w_ref[1024:1536, :]))
    xn = x_ref[0] + gate_ref[0] * acc
    if final:
        xn = _rms_rows(xn) * fg_ref[...]
    y_ref[0] = xn


def _outproj(o_dn, o_gla, o_rw, x, gate, w_out, fg, final, tm):
    b, t, d = x.shape
    tg = gate.shape[1]
    gspec = (pl.BlockSpec((1, 1, d), lambda i, j: (i, 0, 0)) if tg == 1
             else pl.BlockSpec((1, tm, d), lambda i, j: (i, j, 0)))
    ospec = pl.BlockSpec((1, tm, 512), lambda i, j: (i, j, 0))
    return pl.pallas_call(
        functools.partial(_outproj_kernel, final=final),
        grid=(b, t // tm),
        in_specs=[ospec, ospec, ospec,
                  pl.BlockSpec((1, tm, d), lambda i, j: (i, j, 0)),
                  gspec,
                  pl.BlockSpec((1536, d), lambda i, j: (0, 0)),
                  pl.BlockSpec((1, d), lambda i, j: (0, 0))],
        out_specs=pl.BlockSpec((1, tm, d), lambda i, j: (i, j, 0)),
        out_shape=jax.ShapeDtypeStruct((b, t, d), F32),
        compiler_params=_params(2),
        name="out_proj",
    )(o_dn, o_gla, o_rw, x, gate, w_out, fg)


def _dec_proj_kernel(x_ref, shift_ref, scale_ref, ng_ref, w_ref, o_ref):
    h = _modulated_norm(x_ref[...], ng_ref[...], shift_ref[...], scale_ref[...])
    o_ref[...] = _dot(h, w_ref[...])


def _dec_proj(x, shift, scale, ng, w_seg):
    n, d = x.shape
    seg = w_seg.shape[1]
    full = lambda shape: pl.BlockSpec(shape, lambda i: (0,) * len(shape))
    return pl.pallas_call(
        _dec_proj_kernel,
        grid=(1,),
        in_specs=[full((n, d)), full((n, d)), full((n, d)), full((1, d)), full((d, seg))],
        out_specs=full((n, seg)),
        out_shape=jax.ShapeDtypeStruct((n, seg), F32),
        compiler_params=_params(1),
        name="dec_proj",
    )(x, shift, scale, ng, w_seg)


def _dn_dec_kernel(p_ref, cst_ref, s_ref, cw_ref, gp_ref, dng_ref,
                   o_ref, cnew_ref, sn_ref, o_scr):
    bt = p_ref.shape[0]
    qkv = p_ref[:, 0:DN_QKV]
    c0, c1, c2 = cst_ref[:, 0, :], cst_ref[:, 1, :], cst_ref[:, 2, :]
    conv = _silu(c0 * cw_ref[0:1, :] + c1 * cw_ref[1:2, :] + c2 * cw_ref[2:3, :] + qkv * cw_ref[3:4, :])
    cnew_ref[:, 0, :] = c1
    cnew_ref[:, 1, :] = c2
    cnew_ref[:, 2, :] = qkv
    gseg = p_ref[:, 2048:DN_SEG]
    beta_all = _sigmoid(gseg)
    a_all = jnp.exp(-jnp.exp(gp_ref[0:1, :]) * _softplus(gseg + gp_ref[1:2, :]))
    for hh in range(DN_HEADS):
        lo = hh * DN_DK
        q = _l2norm_rows(conv[:, lo:lo + DN_DK]) * (DN_DK ** -0.5)
        k = _l2norm_rows(conv[:, 512 + lo:512 + lo + DN_DK])
        v = conv[:, 1024 + lo:1024 + lo + DN_DV]
        for bi in range(bt):
            k_col = _col_from_row(k[bi:bi + 1, :], DN_DK)
            q_col = _col_from_row(q[bi:bi + 1, :], DN_DK)
            a = a_all[bi:bi + 1, DN_HEADS + hh:DN_HEADS + hh + 1]
            beta = beta_all[bi:bi + 1, hh:hh + 1]
            s_old = s_ref[bi, hh]
            sk = jnp.sum(s_old * k_col, axis=0, keepdims=True)
            v_new = beta * (v[bi:bi + 1, :] - a * sk)
            s_new = s_old * a + k_col * v_new
            sn_ref[bi, hh] = s_new
            o_scr[bi:bi + 1, lo:lo + DN_DV] = jnp.sum(s_new * q_col, axis=0, keepdims=True)
    for hh in range(DN_HEADS):
        lo = hh * DN_DV
        z = p_ref[:, DN_QKV + lo:DN_QKV + lo + DN_DV]
        o_ref[:, lo:lo + DN_DV] = _rms_rows(o_scr[:, lo:lo + DN_DV]) * dng_ref[...] * _silu(z)


def _dn_dec(p, cst, s, cw, gp, dng, bt=8):
    n = p.shape[0]
    return pl.pallas_call(
        _dn_dec_kernel,
        grid=(n // bt,),
        in_specs=[pl.BlockSpec((bt, DN_SEG), lambda i: (i, 0)),
                  pl.BlockSpec((bt, 3, DN_QKV), lambda i: (i, 0, 0)),
                  pl.BlockSpec((bt, DN_HEADS, DN_DK, DN_DV), lambda i: (i, 0, 0, 0)),
                  pl.BlockSpec((4, DN_QKV), lambda i: (0, 0)),
                  pl.BlockSpec((2, LANE), lambda i: (0, 0)),
                  pl.BlockSpec((1, DN_DV), lambda i: (0, 0))],
        out_specs=[pl.BlockSpec((bt, 512), lambda i: (i, 0)),
                   pl.BlockSpec((bt, 3, DN_QKV), lambda i: (i, 0, 0)),
                   pl.BlockSpec((bt, DN_HEADS, DN_DK, DN_DV), lambda i: (i, 0, 0, 0))],
        out_shape=[jax.ShapeDtypeStruct((n, 512), F32),
                   jax.ShapeDtypeStruct((n, 3, DN_QKV), F32),
                   jax.ShapeDtypeStruct((n, DN_HEADS, DN_DK, DN_DV), F32)],
        scratch_shapes=[pltpu.VMEM((bt, 512), F32)],
        compiler_params=_params(1),
        name="dn_dec",
    )(p, cst, s, cw, gp, dng)


def _gla_dec_kernel(p_ref, s_ref, wf_ref, bf_ref, gng_ref, o_ref, sn_ref, o_scr):
    bt = p_ref.shape[0]
    pre = _dot(p_ref[:, 1536:GLA_SEG], wf_ref[...]) + bf_ref[...]
    alpha = jnp.exp((jnp.minimum(pre, 0.0) - jnp.log1p(jnp.exp(-jnp.abs(pre)))) * (1.0 / GLA_TAU))
    q_all = p_ref[:, 0:256] * (GLA_DK ** -0.5)
    k_all = p_ref[:, 256:512]
    for hh in range(GLA_HEADS):
        pair, off = (hh // 2) * LANE, (hh % 2) * GLA_DK
        v = p_ref[:, 512 + hh * GLA_DV:512 + (hh + 1) * GLA_DV]
        for bi in range(bt):
            a_col = _col_from_row(alpha[bi:bi + 1, pair:pair + LANE], GLA_DK, off)
            k_col = _col_from_row(k_all[bi:bi + 1, pair:pair + LANE], GLA_DK, off)
            q_col = _col_from_row(q_all[bi:bi + 1, pair:pair + LANE], GLA_DK, off)
            s_new = s_ref[bi, hh] * a_col + k_col * v[bi:bi + 1, :]
            sn_ref[bi, hh] = s_new
            o_scr[bi:bi + 1, hh * GLA_DV:(hh + 1) * GLA_DV] = jnp.sum(s_new * q_col, axis=0, keepdims=True)
    for hh in range(GLA_HEADS):
        lo = hh * GLA_DV
        z = p_ref[:, 1024 + lo:1024 + lo + GLA_DV]
        o_ref[:, lo:lo + GLA_DV] = _rms_rows(o_scr[:, lo:lo + GLA_DV]) * gng_ref[...] * _silu(z)


def _gla_dec(p, s, wf, bf, gng, bt=8):
    n = p.shape[0]
    return pl.pallas_call(
        _gla_dec_kernel,
        grid=(n // bt,),
        in_specs=[pl.BlockSpec((bt, GLA_SEG), lambda i: (i, 0)),
                  pl.BlockSpec((bt, GLA_HEADS, GLA_DK, GLA_DV), lambda i: (i, 0, 0, 0)),
                  pl.BlockSpec((LANE, 256), lambda i: (0, 0)),
                  pl.BlockSpec((1, 256), lambda i: (0, 0)),
                  pl.BlockSpec((1, GLA_DV), lambda i: (0, 0))],
        out_specs=[pl.BlockSpec((bt, 512), lambda i: (i, 0)),
                   pl.BlockSpec((bt, GLA_HEADS, GLA_DK, GLA_DV), lambda i: (i, 0, 0, 0))],
        out_shape=[jax.ShapeDtypeStruct((n, 512), F32),
                   jax.ShapeDtypeStruct((n, GLA_HEADS, GLA_DK, GLA_DV), F32)],
        scratch_shapes=[pltpu.VMEM((bt, 512), F32)],
        compiler_params=_params(1),
        name="gla_dec",
    )(p, s, wf, bf, gng)


def _rw_dec_kernel(p_ref, sh_ref, s_ref, mu_ref, rp_ref, w2_ref, a2_ref,
                   o_ref, shn_ref, sn_ref, o_scr):
    bt = p_ref.shape[0]
    seg_ones = _seg_ones()
    xs = p_ref[:, 0:RW_SHIFT]
    shn_ref[...] = xs
    xm = xs + (sh_ref[...] - xs) * mu_ref[...]
    r, k, v, kk, b, logw = _rw_features(xm, rp_ref, w2_ref, a2_ref, seg_ones)
    w = jnp.exp(logw)
    for hh in range(RW_HEADS):
        pair, off = (hh // 2) * LANE, (hh % 2) * RW_N
        sl = slice(hh * RW_N, (hh + 1) * RW_N)
        for bi in range(bt):
            row = slice(bi, bi + 1)
            v_col = _col_from_row(v[row, pair:pair + LANE], RW_N, off)
            s_old = s_ref[bi, hh]
            skk = jnp.sum(s_old * kk[row, sl], axis=1, keepdims=True)
            s_new = s_old * w[row, sl] - skk * b[row, sl] + v_col * k[row, sl]
            sn_ref[bi, hh] = s_new
            o_col = jnp.sum(s_new * r[row, sl], axis=1, keepdims=True)
            eye = _iota2((RW_N, RW_N), 0) == _iota2((RW_N, RW_N), 1)
            o_scr[row, sl] = jnp.sum(jnp.where(eye, o_col, 0.0), axis=0, keepdims=True)
    rz = p_ref[:, RW_SHIFT:RW_SEG]
    o_ref[...] = _rw_finish(o_scr[...], r, k, v, rz, rp_ref, seg_ones)


def _rw_dec(p, sh, s, mu, rp, w2, a2, bt=8):
    n = p.shape[0]
    return pl.pallas_call(
        _rw_dec_kernel,
        grid=(n // bt,),
        in_specs=[pl.BlockSpec((bt, RW_SEG), lambda i: (i, 0)),
                  pl.BlockSpec((bt, RW_SHIFT), lambda i: (i, 0)),
                  pl.BlockSpec((bt, RW_HEADS, RW_N, RW_N), lambda i: (i, 0, 0, 0)),
                  pl.BlockSpec((1, RW_SHIFT), lambda i: (0, 0)),
                  pl.BlockSpec((8, RW_C), lambda i: (0, 0)),
                  pl.BlockSpec((LANE, RW_C), lambda i: (0, 0)),
                  pl.BlockSpec((LANE, RW_C), lambda i: (0, 0))],
        out_specs=[pl.BlockSpec((bt, RW_C), lambda i: (i, 0)),
                   pl.BlockSpec((bt, RW_SHIFT), lambda i: (i, 0)),
                   pl.BlockSpec((bt, RW_HEADS, RW_N, RW_N), lambda i: (i, 0, 0, 0))],
        out_shape=[jax.ShapeDtypeStruct((n, RW_C), F32),
                   jax.ShapeDtypeStruct((n, RW_SHIFT), F32),
                   jax.ShapeDtypeStruct((n, RW_HEADS, RW_N, RW_N), F32)],
        scratch_shapes=[pltpu.VMEM((bt, RW_C), F32)],
        compiler_params=_params(1),
        name="rw_dec",
    )(p, sh, s, mu, rp, w2, a2)


def _layer_params(l, norm_g, w_in, dn_conv_w, dn_a_log, dn_dt_bias, dn_norm_g, gla_wf, gla_bf,
                  gla_norm_g, rw_mu, rw_w0, rw_w2, rw_a0, rw_a2, rw_k_k, rw_k_a, rw_r_k,
                  rw_ln_w, rw_ln_b, w_out):
    w = w_in[l]
    pad_cols = lambda a, n: jnp.pad(a, ((0, 0), (0, n - a.shape[1])))
    gp = jnp.zeros((2, LANE), F32)
    gp = gp.at[0, DN_HEADS:2 * DN_HEADS].set(dn_a_log[l]).at[1, DN_HEADS:2 * DN_HEADS].set(dn_dt_bias[l])
    zeros_r = jnp.zeros((1, RW_C), F32)
    rp = jnp.concatenate([rw_w0[l][None], rw_a0[l][None], rw_k_k[l][None], rw_k_a[l][None],
                          rw_r_k[l].reshape(1, RW_C), rw_ln_w[l][None], rw_ln_b[l][None], zeros_r], axis=0)
    return dict(
        ng=norm_g[l][None],
        w_dn=pad_cols(w[:, :DN_PROJ], DN_SEG).astype(BF16),
        w_gla=pad_cols(w[:, DN_PROJ:DN_PROJ + GLA_PROJ], GLA_SEG).astype(BF16),
        w_rw=w[:, DN_PROJ + GLA_PROJ:].astype(BF16),
        cw=dn_conv_w[l], gp=gp, dng=dn_norm_g[l][None],
        wf=jnp.pad(gla_wf[l], ((0, LANE - GLA_RANK), (0, 0))).astype(BF16),
        bf=gla_bf[l][None], gng=gla_norm_g[l][None],
        mu=rw_mu[l][None], rp=rp,
        w2=jnp.pad(rw_w2[l], ((0, LANE - 64), (0, 0))).astype(BF16),
        a2=jnp.pad(rw_a2[l], ((64, 0), (0, 0))).astype(BF16),
        w_out=w_out[l].astype(BF16))


def _gla_state_from_pairs(st):
    heads = []
    for hh in range(GLA_HEADS):
        p, hl = hh // 2, hh % 2
        blk = st[:, p, hl * GLA_DV:(hl + 1) * GLA_DV, hl * GLA_DK:(hl + 1) * GLA_DK]
        heads.append(jnp.swapaxes(blk, 1, 2))
    return jnp.stack(heads, axis=1)


def _rw_state_from_pairs(s):
    heads = []
    for hh in range(RW_HEADS):
        p, hl = hh // 2, hh % 2
        heads.append(s[:, p, hl * RW_N:(hl + 1) * RW_N, hl * RW_N:(hl + 1) * RW_N])
    return jnp.stack(heads, axis=1)


def _prompt_tiles(t):
    tb = 256 if t % 256 == 0 else CHUNK
    tm = 512 if t % 512 == 0 else tb
    return tb, tm


def _prompt_trunk(x, mods, layers, final_g):
    b, t, _ = x.shape
    tb, tm = _prompt_tiles(t)
    states = []
    n_layers = len(layers)
    for l, lp in enumerate(layers):
        mod = mods[l]
        o_dn, conv_new, dn_new = _dn_prompt(x, mod, lp["ng"], lp["w_dn"], lp["cw"], lp["gp"], lp["dng"], tb)
        o_gla, gla_st = _gla_prompt(x, mod, lp["ng"], lp["w_gla"], lp["wf"], lp["bf"], lp["gng"], tb)
        o_rw, rs_new, rw_st = _rw_prompt(x, mod, lp["ng"], lp["w_rw"], lp["mu"], lp["rp"], lp["w2"], lp["a2"], tb)
        x = _outproj(o_dn, o_gla, o_rw, x, mod[:, 2:3, :], lp["w_out"], final_g, l == n_layers - 1, tm)
        states.append((conv_new, dn_new, _gla_state_from_pairs(gla_st), rs_new[:, 0, :],
                       _rw_state_from_pairs(rw_st)))
    return x, states


def _sample_trunk(x, mods, layers, final_g, conv_s, dn_s, gla_s, rs_s, rw_s):
    n = x.shape[0]
    x = x.reshape(n, D_MODEL)
    states = []
    n_layers = len(layers)
    for l, lp in enumerate(layers):
        shift, scale, gate = mods[l][:, 0, :], mods[l][:, 1, :], mods[l][:, 2, :]
        p_dn = _dec_proj(x, shift, scale, lp["ng"], lp["w_dn"])
        p_gla = _dec_proj(x, shift, scale, lp["ng"], lp["w_gla"])
        p_rw = _dec_proj(x, shift, scale, lp["ng"], lp["w_rw"])
        o_dn, conv_new, dn_new = _dn_dec(p_dn, conv_s[l], dn_s[l], lp["cw"], lp["gp"], lp["dng"])
        o_gla, gla_new = _gla_dec(p_gla, gla_s[l], lp["wf"], lp["bf"], lp["gng"])
        o_rw, rs_new, rw_new = _rw_dec(p_rw, rs_s[l], rw_s[l], lp["mu"], lp["rp"], lp["w2"], lp["a2"])
        x = _outproj(o_dn[None], o_gla[None], o_rw[None], x[None], gate[None], lp["w_out"], final_g,
                     l == n_layers - 1, n)[0]
        states.append((conv_new, dn_new, gla_new, rs_new, rw_new))
    return x.reshape(n, 1, D_MODEL), states


def kernel(x_prompt, x_sample, c_prompt, c_sample, state_dn_conv, state_dn, state_gla, state_rwkv_shift, state_rwkv, norm_g, ada_w, ada_b, w_in, dn_conv_w, dn_a_log, dn_dt_bias, dn_norm_g, gla_wf, gla_bf, gla_norm_g, rw_mu, rw_w0, rw_w2, rw_a0, rw_a2, rw_k_k, rw_k_a, rw_r_k, rw_ln_w, rw_ln_b, w_out, final_norm_g):
    depth = w_in.shape[0]
    bp = x_prompt.shape[0]
    layers = [_layer_params(l, norm_g, w_in, dn_conv_w, dn_a_log, dn_dt_bias, dn_norm_g, gla_wf, gla_bf,
                            gla_norm_g, rw_mu, rw_w0, rw_w2, rw_a0, rw_a2, rw_k_k, rw_k_a, rw_r_k,
                            rw_ln_w, rw_ln_b, w_out) for l in range(depth)]
    final_g = final_norm_g[None]
    mod_all = _modulation(jnp.concatenate([c_prompt, c_sample], axis=0), ada_w, ada_b)
    mod_all = mod_all.reshape(depth, -1, 3, D_MODEL)
    mods_p = [mod_all[l, :bp] for l in range(depth)]
    mods_s = [mod_all[l, bp:] for l in range(depth)]

    y_p, st_p = _prompt_trunk(x_prompt, mods_p, layers, final_g)
    y_s, st_s = _sample_trunk(x_sample, mods_s, layers, final_g, state_dn_conv, state_dn, state_gla,
                              state_rwkv_shift, state_rwkv)
    stack = lambda sts, i: jnp.stack([s[i] for s in sts])
    return (y_p, y_s,
            stack(st_p, 0), stack(st_p, 1), stack(st_p, 2), stack(st_p, 3), stack(st_p, 4),
            stack(st_s, 0), stack(st_s, 1), stack(st_s, 2), stack(st_s, 3), stack(st_s, 4))
```

```python
import functools

import jax
import jax.numpy as jnp
from jax import lax
from jax.experimental import pallas as pl
from jax.experimental.pallas import tpu as pltpu

F32 = jnp.float32
BF16 = jnp.bfloat16

D_MODEL = 1024
EPS = 1e-6
CHUNK = 64
SUB = 16
NEG = -1e30

DN_HEADS, DN_DK, DN_DV = 4, 128, 128
GLA_HEADS, GLA_DK, GLA_DV, GLA_RANK, GLA_TAU = 4, 64, 128, 16, 16.0
RW_HEADS, RW_N = 8, 64
RW_GROUP = 4
RW_LN_EPS = 64e-5
DN_QKV = 1536
RW_C = 512
RW_SHIFT = 1664
DN_SEG, GLA_SEG, RW_SEG = 2176, 1664, 2176
DN_PROJ, GLA_PROJ = 2056, 1552
LANE = 128
VMEM_LIMIT = 56 * 1024 * 1024


def _dot(a, b):
    return jnp.dot(a.astype(BF16), b.astype(BF16), preferred_element_type=F32)


def _dot_nt(a, b):
    return lax.dot_general(a.astype(BF16), b.astype(BF16), (((1,), (1,)), ((), ())),
                           preferred_element_type=F32)


def _dot_tn(a, b):
    return lax.dot_general(a.astype(BF16), b.astype(BF16), (((0,), (0,)), ((), ())),
                           preferred_element_type=F32)


def _sigmoid(x):
    return jax.nn.sigmoid(x)


def _silu(x):
    return x * jax.nn.sigmoid(x)


def _softplus(x):
    return jnp.maximum(x, 0.0) + jnp.log1p(jnp.exp(-jnp.abs(x)))


def _iota2(shape, dim):
    return lax.broadcasted_iota(jnp.int32, shape, dim)


def _modulated_norm(x, ng, shift, scale):
    ms = jnp.mean(x * x, axis=-1, keepdims=True)
    return x * lax.rsqrt(ms + EPS) * ng * (1.0 + scale) + shift


def _l2norm_rows(x):
    return x * lax.rsqrt(jnp.sum(x * x, axis=-1, keepdims=True) + EPS)


def _rms_rows(x):
    return x * lax.rsqrt(jnp.mean(x * x, axis=-1, keepdims=True) + EPS)


def _seg_sum(x, seg_ones):
    hi = x.astype(BF16)
    lo = (x - hi.astype(F32)).astype(BF16)
    return (jnp.dot(hi, seg_ones, preferred_element_type=F32)
            + jnp.dot(lo, seg_ones, preferred_element_type=F32))


def _split2(x):
    hi = x.astype(BF16)
    return hi, (x - hi.astype(F32)).astype(BF16)


def _split3(x):
    hi = x.astype(BF16)
    rest = x - hi.astype(F32)
    mid = rest.astype(BF16)
    return hi, mid, (rest - mid.astype(F32)).astype(BF16)


def _dot3s(a2, b2):
    return jnp.dot(jnp.concatenate([a2[0], a2[0], a2[1]], axis=1),
                   jnp.concatenate([b2[0], b2[1], b2[0]], axis=0), preferred_element_type=F32)


def _dot3(a, b):
    return _dot3s(_split2(a), _split2(b))


def _sum_dot(sel01, x):
    return jnp.dot(jnp.concatenate([sel01] * 3, axis=1), jnp.concatenate(_split3(x), axis=0),
                   preferred_element_type=F32)


def _chunk_tri(n):
    r, c = _iota2((n, n), 0), _iota2((n, n), 1)
    return (((r >> 6) == (c >> 6)) & (r >= c)).astype(BF16)


def _bd_masks(n=CHUNK):
    r, c = _iota2((n, n), 0), _iota2((n, n), 1)
    same = (r >> 6) == (c >> 6)
    return dict(
        causal=same & (r >= c), strict=same & (r > c), eye=(r == c).astype(F32), same=same,
        bd16=(r >> 4) == (c >> 4),
        off32=((r >> 5) == (c >> 5)) & ((r >> 4) != (c >> 4)),
        off64=same & ((r >> 5) != (c >> 5)))


def _unit_lower_inverse(L, m):
    Ld32 = jnp.where(m["bd16"], L, 0.0)
    Ld = Ld32.astype(BF16)
    L2 = _dot(Ld, Ld).astype(BF16)
    L4 = _dot(L2, L2).astype(BF16)
    L8 = _dot(L4, L4).astype(BF16)
    X = m["eye"] - Ld32
    X = X + _dot(X, L2)
    X = X + _dot(X, L4)
    X = X + _dot(X, L8)
    for off in ("off32", "off64"):
        Xb = X.astype(BF16)
        X = X - _dot(_dot(Xb, jnp.where(m[off], L, 0.0)), Xb)
    return X


def _col_from_row(row, n, lane_off=0):
    w = row.shape[1]
    sel = (_iota2((n, w), 0) + lane_off) == _iota2((n, w), 1)
    return jnp.sum(jnp.where(sel, row, 0.0), axis=1, keepdims=True)


def _params(n_axes):
    return pltpu.CompilerParams(dimension_semantics=("arbitrary",) * n_axes,
                                vmem_limit_bytes=VMEM_LIMIT)


def _mod_kernel(c_ref, w_ref, b_ref, o_ref):
    o_ref[0] = _dot(_silu(c_ref[...]), w_ref[0]) + b_ref[0]


def _modulation(c_all, ada_w, ada_b):
    depth, d, n = ada_w.shape
    rows = c_all.shape[0]
    tn = 1024
    return pl.pallas_call(
        _mod_kernel,
        grid=(depth, n // tn),
        in_specs=[pl.BlockSpec((rows, d), lambda l, j: (0, 0)),
                  pl.BlockSpec((1, d, tn), lambda l, j: (l, 0, j)),
                  pl.BlockSpec((1, 1, tn), lambda l, j: (l, 0, j))],
        out_specs=pl.BlockSpec((1, rows, tn), lambda l, j: (l, 0, j)),
        out_shape=jax.ShapeDtypeStruct((depth, rows, n), F32),
        compiler_params=_params(2),
        name="adaln_mod",
    )(c_all, ada_w, ada_b.reshape(depth, 1, n))


def _dn_prompt_kernel(x_ref, mod_ref, ng_ref, w_ref, cw_ref, gp_ref, dng_ref,
                      o_ref, conv_ref, s_ref,
                      pbuf, qs, ks, vs, gb, gcs, os_, U_s, W_s, Q_s, A_s, K_s, E_s, S):
    t = pl.program_id(1)
    tb = x_ref.shape[1]
    nb = DN_HEADS * CHUNK
    m = _bd_masks(nb)
    lane_c = _iota2((CHUNK, LANE), 1)
    lane_blk = _iota2((CHUNK, DN_HEADS * DN_DK), 1) >> 7
    gate_lanes = ((_iota2((8, LANE), 1) >= DN_HEADS) & (_iota2((8, LANE), 1) < 2 * DN_HEADS)).astype(BF16)
    heads = range(DN_HEADS)

    def stack(f):
        return jnp.concatenate([f(hh) for hh in heads], axis=0)

    @pl.when(t == 0)
    def _():
        pbuf[0:8, :] = jnp.zeros((8, DN_SEG), F32)
        S[...] = jnp.zeros_like(S)

    h = _modulated_norm(x_ref[0], ng_ref[...], mod_ref[0, 0:1, :], mod_ref[0, 1:2, :])
    pbuf[8:8 + tb, :] = _dot(h, w_ref[...])

    conv = pbuf[5:5 + tb, 0:DN_QKV] * cw_ref[0:1, :]
    for j in range(1, 4):
        conv = conv + pbuf[5 + j:5 + j + tb, 0:DN_QKV] * cw_ref[j:j + 1, :]
    conv = _silu(conv)
    for hh in range(DN_HEADS):
        lo = hh * DN_DK
        qs[:, lo:lo + DN_DK] = _l2norm_rows(conv[:, lo:lo + DN_DK]) * (DN_DK ** -0.5)
        ks[:, lo:lo + DN_DK] = _l2norm_rows(conv[:, 512 + lo:512 + lo + DN_DK])
    vs[...] = conv[:, 1024:DN_QKV]
    gseg = pbuf[8:8 + tb, 2048:DN_SEG]
    lane = _iota2((tb, LANE), 1)
    gb[...] = jnp.where(lane < DN_HEADS, _sigmoid(gseg),
                        -jnp.exp(gp_ref[0:1, :]) * _softplus(gseg + gp_ref[1:2, :]))

    gcs[...] = _sum_dot(_chunk_tri(tb), gb[...])

    def state_free(c, carry):
        rows = pl.ds(pl.multiple_of(c * CHUNK, CHUNK), CHUNK)
        gbc = gb[rows, :]
        gc = gcs[rows, :]
        g_col = stack(lambda hh: gc[:, DN_HEADS + hh:DN_HEADS + hh + 1])
        g_last = stack(lambda hh: jnp.broadcast_to(gc[CHUNK - 1:CHUNK, DN_HEADS + hh:DN_HEADS + hh + 1],
                                                   (CHUNK, 1)))
        beta = stack(lambda hh: gbc[:, hh:hh + 1])
        g_sel = stack(lambda hh: jnp.where(lane_c == DN_HEADS + hh, gc, 0.0))
        g_row = lax.dot_general(jnp.concatenate([gate_lanes] * 3, axis=1),
                                jnp.concatenate(_split3(g_sel), axis=1),
                                (((1,), (1,)), ((), ())), preferred_element_type=F32)[0:1, :]
        decay = jnp.exp(jnp.where(m["causal"], g_col - g_row, NEG))
        k_all = ks[rows, :]
        q_all = qs[rows, :]
        beta_l = jnp.concatenate([jnp.broadcast_to(gbc[:, hh:hh + 1], (CHUNK, DN_DK)) for hh in heads], axis=1)
        kb_all = k_all * beta_l

        def expand(a):
            return stack(lambda hh: jnp.where(lane_blk == hh, a, 0.0))

        kq = _dot_nt(jnp.concatenate([expand(kb_all), expand(q_all)], axis=0),
                     jnp.concatenate([k_all] * DN_HEADS, axis=0))
        L = jnp.where(m["strict"], kq[0:nb] * decay, 0.0)
        A = kq[nb:2 * nb] * decay
        X = _unit_lower_inverse(L, m)
        k_st = stack(lambda hh: k_all[:, hh * DN_DK:(hh + 1) * DN_DK])
        q_st = stack(lambda hh: q_all[:, hh * DN_DK:(hh + 1) * DN_DK])
        v_st = stack(lambda hh: vs[rows, hh * DN_DV:(hh + 1) * DN_DV])
        e_g = jnp.exp(g_col)
        sol = _dot3(X, jnp.concatenate([v_st * beta, k_st * (beta * e_g)], axis=1))
        U_s[c] = sol[:, 0:DN_DV]
        W_s[c] = sol[:, DN_DV:].astype(BF16)
        Q_s[c] = (q_st * e_g).astype(BF16)
        A_s[c] = A.astype(BF16)
        K_s[c] = (k_st * jnp.exp(g_last - g_col)).astype(BF16)
        E_s[c] = jnp.exp(gc[CHUNK - 8:CHUNK, :])
        return carry

    def recurrence(c, carry):
        rows = pl.ds(pl.multiple_of(c * CHUNK, CHUNK), CHUNK)
        w, qe, kh = W_s[c], Q_s[c], K_s[c]
        e_last = E_s[c][7:8, :]
        ws = []
        for hh in heads:
            sl = slice(hh * CHUNK, (hh + 1) * CHUNK)
            ws.append(jnp.dot(jnp.concatenate([w[sl], qe[sl]], axis=0), S[hh].astype(BF16),
                              preferred_element_type=F32))
        v_new = U_s[c] - jnp.concatenate([x[0:CHUNK] for x in ws], axis=0)
        vb = v_new.astype(BF16)
        o_st = jnp.concatenate([x[CHUNK:2 * CHUNK] for x in ws], axis=0) + jnp.dot(
            A_s[c], vb, preferred_element_type=F32)
        for hh in heads:
            sl = slice(hh * CHUNK, (hh + 1) * CHUNK)
            S[hh] = S[hh] * e_last[:, DN_HEADS + hh:DN_HEADS + hh + 1] + _dot_tn(kh[sl], vb[sl])
            os_[rows, hh * DN_DV:(hh + 1) * DN_DV] = o_st[sl]
        return carry

    lax.fori_loop(0, tb // CHUNK, state_free, 0, unroll=2)
    lax.fori_loop(0, tb // CHUNK, recurrence, 0)

    for hh in heads:
        lo = hh * DN_DV
        z = pbuf[8:8 + tb, DN_QKV + lo:DN_QKV + lo + DN_DV]
        o_ref[0, :, lo:lo + DN_DV] = (_rms_rows(os_[:, lo:lo + DN_DV]) * dng_ref[...] * _silu(z)).astype(o_ref.dtype)

    pbuf[0:8, :] = pbuf[tb:tb + 8, :]

    @pl.when(t == pl.num_programs(1) - 1)
    def _():
        conv_ref[0] = pbuf[tb + 5:tb + 8, 0:DN_QKV]
        s_ref[0] = S[...]


def _dn_prompt(x, mod, ng, w_seg, cw, gp, dng, tb):
    b, t, d = x.shape
    nch, nb = tb // CHUNK, DN_HEADS * CHUNK
    return pl.pallas_call(
        _dn_prompt_kernel,
        grid=(b, t // tb),
        in_specs=[pl.BlockSpec((1, tb, d), lambda i, j: (i, j, 0)),
                  pl.BlockSpec((1, 3, d), lambda i, j: (i, 0, 0)),
                  pl.BlockSpec((1, d), lambda i, j: (0, 0)),
                  pl.BlockSpec((d, DN_SEG), lambda i, j: (0, 0)),
                  pl.BlockSpec((4, DN_QKV), lambda i, j: (0, 0)),
                  pl.BlockSpec((2, LANE), lambda i, j: (0, 0)),
                  pl.BlockSpec((1, DN_DV), lambda i, j: (0, 0))],
        out_specs=[pl.BlockSpec((1, tb, 512), lambda i, j: (i, j, 0)),
                   pl.BlockSpec((1, 3, DN_QKV), lambda i, j: (i, 0, 0)),
                   pl.BlockSpec((1, DN_HEADS, DN_DK, DN_DV), lambda i, j: (i, 0, 0, 0))],
        out_shape=[jax.ShapeDtypeStruct((b, t, 512), BF16),
                   jax.ShapeDtypeStruct((b, 3, DN_QKV), F32),
                   jax.ShapeDtypeStruct((b, DN_HEADS, DN_DK, DN_DV), F32)],
        scratch_shapes=[pltpu.VMEM((tb + 8, DN_SEG), F32),
                        pltpu.VMEM((tb, 512), F32), pltpu.VMEM((tb, 512), F32),
                        pltpu.VMEM((tb, 512), F32), pltpu.VMEM((tb, LANE), F32),
                        pltpu.VMEM((tb, LANE), F32), pltpu.VMEM((tb, 512), F32),
                        pltpu.VMEM((nch, nb, DN_DV), F32), pltpu.VMEM((nch, nb, DN_DV), BF16),
                        pltpu.VMEM((nch, nb, DN_DK), BF16), pltpu.VMEM((nch, nb, nb), BF16),
                        pltpu.VMEM((nch, nb, DN_DK), BF16), pltpu.VMEM((nch, 8, LANE), F32),
                        pltpu.VMEM((DN_HEADS, DN_DK, DN_DV), F32)],
        compiler_params=_params(2),
        name="dn_prompt",
    )(x, mod, ng, w_seg, cw, gp, dng)


def _gla_prompt_kernel(x_ref, mod_ref, ng_ref, w_ref, wf_ref, bf_ref, gng_ref,
                       o_ref, s_ref,
                       pbuf, lfs, ST):
    t = pl.program_id(1)
    tb = x_ref.shape[1]
    m = _bd_masks()
    ltri = _chunk_tri(CHUNK)
    n_sub = CHUNK // SUB
    lane = _iota2((CHUNK, LANE), 1)
    head1 = lane >= GLA_DK
    rowblk = _iota2((CHUNK, LANE), 0) >> 4
    rowloc = _iota2((CHUNK, LANE), 0) & (SUB - 1)
    er = _iota2((2 * SUB, SUB * LANE), 0)
    ec = _iota2((2 * SUB, SUB * LANE), 1)
    eblk = (((ec >> 7) == (er & (SUB - 1))) & (((ec & (LANE - 1)) >= GLA_DK) == (er >= SUB))).astype(BF16)
    st_r = _iota2((2 * GLA_DV, LANE), 0)
    st_c = _iota2((2 * GLA_DV, LANE), 1)
    st_mask = (st_r >= GLA_DV) == (st_c >= GLA_DK)

    @pl.when(t == 0)
    def _():
        ST[...] = jnp.zeros_like(ST)

    h = _modulated_norm(x_ref[0], ng_ref[...], mod_ref[0, 0:1, :], mod_ref[0, 1:2, :])
    pbuf[...] = _dot(h, w_ref[...])
    pre = _dot(pbuf[:, 1536:GLA_SEG], wf_ref[...]) + bf_ref[...]
    lfs[...] = (jnp.minimum(pre, 0.0) - jnp.log1p(jnp.exp(-jnp.abs(pre)))) * (1.0 / GLA_TAU)

    def chunk(c, carry):
        r0 = pl.multiple_of(c * CHUNK, CHUNK)
        rows = pl.ds(r0, CHUNK)
        g_all = _sum_dot(ltri, lfs[rows, :])
        for p in range(GLA_HEADS // 2):
            lo = p * LANE
            G = g_all[:, lo:lo + LANE]
            q = pbuf[rows, lo:lo + LANE] * (GLA_DK ** -0.5)
            k = pbuf[rows, 256 + lo:256 + lo + LANE]
            v = pbuf[rows, 512 + 2 * lo:512 + 2 * lo + 2 * GLA_DV]
            st = ST[p]
            o_inter = _dot_nt(q * jnp.exp(G), st)

            G3 = G.reshape(n_sub, SUB, LANE)
            q3 = q.reshape(n_sub, SUB, LANE)

            def pick(a3, i):
                return jnp.broadcast_to(a3[:, i:i + 1, :], (n_sub, SUB, LANE)).reshape(CHUNK, LANE)

            ref_row = pick(G3, 0)
            q_t = q * jnp.exp(G - ref_row)
            q_parts, k_parts = [], []
            for i in range(1, n_sub):
                q_parts.append(jnp.where(rowblk == i, q_t, 0.0))
                ref_i = G[i * SUB:i * SUB + 1, :]
                k_parts.append(k * jnp.exp(jnp.where(rowblk < i, ref_i - G, NEG)))
            q_big = jnp.concatenate(q_parts, axis=1)
            k_big = jnp.concatenate(k_parts, axis=1)
            head1_big = jnp.concatenate([head1] * (n_sub - 1), axis=1)
            q_both = jnp.concatenate([jnp.where(head1_big, 0.0, q_big),
                                      jnp.where(head1_big, q_big, 0.0)], axis=0)
            a_off = _dot_nt(q_both, k_big)

            slabs = []
            for i in range(SUB):
                dec = jnp.exp(jnp.where(rowloc <= i, pick(G3, i) - G, NEG))
                slabs.append((k * pick(q3, i) * dec).astype(BF16))
            p_stack = jnp.concatenate(slabs, axis=1)
            a_diag = lax.dot_general(eblk, p_stack, (((1,), (1,)), ((), ())),
                                     preferred_element_type=F32)

            kv = _dot_tn(v, k * jnp.exp(G[CHUNK - 1:CHUNK, :] - G))
            ST[p] = st * jnp.exp(G[CHUNK - 1:CHUNK, :]) + jnp.where(st_mask, kv, 0.0)

            for hl in range(2):
                hh = 2 * p + hl
                a_d = a_diag[hl * SUB:(hl + 1) * SUB, :]
                a_h = a_off[hl * CHUNK:(hl + 1) * CHUNK, :] + jnp.where(
                    m["bd16"], jnp.concatenate([a_d] * n_sub, axis=0), 0.0)
                v_h = v[:, hl * GLA_DV:(hl + 1) * GLA_DV]
                o = o_inter[:, hl * GLA_DV:(hl + 1) * GLA_DV] + _dot(a_h, v_h)
                z = pbuf[rows, 1024 + hh * GLA_DV:1024 + (hh + 1) * GLA_DV]
                o_ref[0, rows, hh * GLA_DV:(hh + 1) * GLA_DV] = (
                    _rms_rows(o) * gng_ref[...] * _silu(z)).astype(o_ref.dtype)
        return carry

    lax.fori_loop(0, tb // CHUNK, chunk, 0)

    @pl.when(t == pl.num_programs(1) - 1)
    def _():
        s_ref[0] = ST[...]


def _gla_prompt(x, mod, ng, w_seg, wf, bf, gng, tb):
    b, t, d = x.shape
    return pl.pallas_call(
        _gla_prompt_kernel,
        grid=(b, t // tb),
        in_specs=[pl.BlockSpec((1, tb, d), lambda i, j: (i, j, 0)),
                  pl.BlockSpec((1, 3, d), lambda i, j: (i, 0, 0)),
                  pl.BlockSpec((1, d), lambda i, j: (0, 0)),
                  pl.BlockSpec((d, GLA_SEG), lambda i, j: (0, 0)),
                  pl.BlockSpec((LANE, 256), lambda i, j: (0, 0)),
                  pl.BlockSpec((1, 256), lambda i, j: (0, 0)),
                  pl.BlockSpec((1, GLA_DV), lambda i, j: (0, 0))],
        out_specs=[pl.BlockSpec((1, tb, 512), lambda i, j: (i, j, 0)),
                   pl.BlockSpec((1, 2, 2 * GLA_DV, LANE), lambda i, j: (i, 0, 0, 0))],
        out_shape=[jax.ShapeDtypeStruct((b, t, 512), BF16),
                   jax.ShapeDtypeStruct((b, 2, 2 * GLA_DV, LANE), F32)],
        scratch_shapes=[pltpu.VMEM((tb, GLA_SEG), F32), pltpu.VMEM((tb, 256), F32),
                        pltpu.VMEM((2, 2 * GLA_DV, LANE), F32)],
        compiler_params=_params(2),
        name="gla_prompt",
    )(x, mod, ng, w_seg, wf, bf, gng)


def _rw_features(xm, rp_ref, w2_ref, a2_ref, seg_ones):
    r = xm[:, 0:RW_C]
    k = xm[:, RW_C:2 * RW_C]
    v = xm[:, 2 * RW_C:3 * RW_C]
    lo = xm[:, 3 * RW_C:RW_SHIFT]
    w_log = -_softplus(-(rp_ref[0:1, :] + _dot(jnp.tanh(lo), w2_ref[...]))) - 0.5
    logw = -jnp.exp(w_log)
    a = _sigmoid(rp_ref[1:2, :] + _dot(lo, a2_ref[...]))
    kk = k * rp_ref[2:3, :]
    kk = kk * lax.rsqrt(_seg_sum(kk * kk, seg_ones) + EPS)
    k = k * (1.0 + (a - 1.0) * rp_ref[3:4, :])
    return r, k, v, kk, kk * a, logw


def _rw_finish(o, r, k, v, rz, rp_ref, seg_ones):
    mu = _seg_sum(o, seg_ones) * (1.0 / RW_N)
    dev = o - mu
    var = _seg_sum(dev * dev, seg_ones) * (1.0 / RW_N)
    on = dev * lax.rsqrt(var + RW_LN_EPS) * rp_ref[5:6, :] + rp_ref[6:7, :]
    bonus = _seg_sum(r * k * rp_ref[4:5, :], seg_ones) * v
    return (on + bonus) * _silu(rz)


def _seg_ones():
    return ((_iota2((RW_C, RW_C), 0) >> 6) == (_iota2((RW_C, RW_C), 1) >> 6)).astype(BF16)


def _rw_prompt_kernel(x_ref, mod_ref, ng_ref, w_ref, mu_ref, rp_ref, w2_ref, a2_ref,
                      o_ref, sh_ref, s_ref,
                      pbuf, rs, ks, vs, kks, bs, lws, gs, os_,
                      P1_s, KT_s, P2_s, RH_s, AB_s, KB_s, E_s, S):
    t = pl.program_id(1)
    tb = x_ref.shape[1]
    gw = RW_GROUP * RW_N
    m = _bd_masks(gw)
    seg_ones = _seg_ones()
    lane_head = _iota2((CHUNK, gw), 1) >> 6
    mid = CHUNK // 2
    heads = range(RW_GROUP)

    def expand(a):
        return jnp.concatenate([jnp.where(lane_head == hh, a, 0.0) for hh in heads], axis=0)

    def compact(a):
        return a[0:CHUNK] + a[CHUNK:2 * CHUNK] + a[2 * CHUNK:3 * CHUNK] + a[3 * CHUNK:4 * CHUNK]

    @pl.when(t == 0)
    def _():
        pbuf[0:8, :] = jnp.zeros((8, RW_SEG), F32)
        S[...] = jnp.zeros_like(S)

    h = _modulated_norm(x_ref[0], ng_ref[...], mod_ref[0, 0:1, :], mod_ref[0, 1:2, :])
    pbuf[8:8 + tb, :] = _dot(h, w_ref[...])
    xs = pbuf[8:8 + tb, 0:RW_SHIFT]
    prev = pbuf[7:7 + tb, 0:RW_SHIFT]
    xm = xs + (prev - xs) * mu_ref[...]
    r, k, v, kk, b, logw = _rw_features(xm, rp_ref, w2_ref, a2_ref, seg_ones)
    rs[...] = r
    ks[...] = k
    vs[...] = v
    kks[...] = kk
    bs[...] = b
    lws[...] = logw
    gs[...] = _sum_dot(_chunk_tri(tb), logw)

    def state_free(c, carry):
        rows = pl.ds(pl.multiple_of(c * CHUNK, CHUNK), CHUNK)
        for g in range(RW_HEADS // RW_GROUP):
            sl = slice(g * gw, (g + 1) * gw)
            G = gs[rows, sl]
            Gx = G - lws[rows, sl]
            ref = G[mid:mid + 1, :]
            r_, k_, v_, kk_, b_ = rs[rows, sl], ks[rows, sl], vs[rows, sl], kks[rows, sl], bs[rows, sl]
            e_out = jnp.exp(ref - G)
            lhs = jnp.concatenate([expand(kk_ * jnp.exp(Gx - ref)), expand(r_ * jnp.exp(G - ref))], axis=0)
            rhs = jnp.concatenate([b_ * e_out] * RW_GROUP + [k_ * e_out] * RW_GROUP, axis=0)
            mm = _dot_nt(lhs, rhs)
            Lb = jnp.where(m["strict"], mm[0:gw, 0:gw], 0.0)
            Mk = jnp.where(m["strict"], mm[0:gw, gw:], 0.0)
            Arb = jnp.where(m["causal"], mm[gw:, 0:gw], 0.0)
            Ark = jnp.where(m["causal"], mm[gw:, gw:], 0.0)
            X = _unit_lower_inverse(Lb, m)
            kv = _dot(jnp.concatenate([Mk, Ark], axis=0), expand(v_))
            sol = _dot3(X, jnp.concatenate([kv[0:gw], expand(kk_ * jnp.exp(Gx))], axis=1))
            P1_s[c, g] = sol[:, 0:gw]
            KT_s[c, g] = sol[:, gw:].astype(BF16)
            P2_s[c, g] = compact(kv[gw:])
            RH_s[c, g] = (r_ * jnp.exp(G)).astype(BF16)
            AB_s[c, g] = Arb.astype(BF16)
            e_end = jnp.exp(G[CHUNK - 1:CHUNK, :] - G)
            KB_s[c, g] = jnp.concatenate([k_ * e_end, -(b_ * e_end)], axis=0).astype(BF16)
            E_s[c, g] = jnp.exp(G[CHUNK - 8:CHUNK, :])
        return carry

    def recurrence(c, carry):
        rows = pl.ds(pl.multiple_of(c * CHUNK, CHUNK), CHUNK)
        for g in range(RW_HEADS // RW_GROUP):
            sl = slice(g * gw, (g + 1) * gw)
            s_g = S[g]
            khu = _dot_nt(jnp.concatenate([KT_s[c, g], RH_s[c, g]], axis=0), s_g)
            u_exp = P1_s[c, g] + khu[0:gw]
            o = khu[gw:gw + CHUNK] + P2_s[c, g] - compact(_dot(AB_s[c, g], u_exp))
            vu = jnp.concatenate([vs[rows, sl], compact(u_exp)], axis=0)
            S[g] = s_g * E_s[c, g][7:8, :] + jnp.where(m["same"], _dot_tn(vu, KB_s[c, g]), 0.0)
            os_[rows, sl] = o
        return carry

    lax.fori_loop(0, tb // CHUNK, state_free, 0, unroll=2)
    lax.fori_loop(0, tb // CHUNK, recurrence, 0)

    rz = pbuf[8:8 + tb, RW_SHIFT:RW_SEG]
    o_ref[0] = _rw_finish(os_[...], rs[...], ks[...], vs[...], rz, rp_ref, seg_ones).astype(o_ref.dtype)
    pbuf[0:8, :] = pbuf[tb:tb + 8, :]

    @pl.when(t == pl.num_programs(1) - 1)
    def _():
        sh_ref[0] = pbuf[tb + 7:tb + 8, 0:RW_SHIFT]
        s_ref[0] = S[...]


def _rw_prompt(x, mod, ng, w_seg, mu, rp, w2, a2, tb):
    b, t, d = x.shape
    nch, ng_, gw = tb // CHUNK, RW_HEADS // RW_GROUP, RW_GROUP * RW_N
    return pl.pallas_call(
        _rw_prompt_kernel,
        grid=(b, t // tb),
        in_specs=[pl.BlockSpec((1, tb, d), lambda i, j: (i, j, 0)),
                  pl.BlockSpec((1, 3, d), lambda i, j: (i, 0, 0)),
                  pl.BlockSpec((1, d), lambda i, j: (0, 0)),
                  pl.BlockSpec((d, RW_SEG), lambda i, j: (0, 0)),
                  pl.BlockSpec((1, RW_SHIFT), lambda i, j: (0, 0)),
                  pl.BlockSpec((8, RW_C), lambda i, j: (0, 0)),
                  pl.BlockSpec((LANE, RW_C), lambda i, j: (0, 0)),
                  pl.BlockSpec((LANE, RW_C), lambda i, j: (0, 0))],
        out_specs=[pl.BlockSpec((1, tb, RW_C), lambda i, j: (i, j, 0)),
                   pl.BlockSpec((1, 1, RW_SHIFT), lambda i, j: (i, 0, 0)),
                   pl.BlockSpec((1, ng_, gw, gw), lambda i, j: (i, 0, 0, 0))],
        out_shape=[jax.ShapeDtypeStruct((b, t, RW_C), BF16),
                   jax.ShapeDtypeStruct((b, 1, RW_SHIFT), F32),
                   jax.ShapeDtypeStruct((b, ng_, gw, gw), F32)],
        scratch_shapes=[pltpu.VMEM((tb + 8, RW_SEG), F32)]
        + [pltpu.VMEM((tb, RW_C), F32) for _ in range(8)]
        + [pltpu.VMEM((nch, ng_, gw, gw), F32), pltpu.VMEM((nch, ng_, gw, gw), BF16),
           pltpu.VMEM((nch, ng_, CHUNK, gw), F32), pltpu.VMEM((nch, ng_, CHUNK, gw), BF16),
           pltpu.VMEM((nch, ng_, gw, gw), BF16), pltpu.VMEM((nch, ng_, 2 * CHUNK, gw), BF16),
           pltpu.VMEM((nch, ng_, 8, gw), F32),
           pltpu.VMEM((ng_, gw, gw), F32)],
        compiler_params=_params(2),
        name="rw_prompt",
    )(x, mod, ng, w_seg, mu, rp, w2, a2)


def _outproj_kernel(od_ref, og_ref, or_ref, x_ref, gate_ref, w_ref, fg_ref, y_ref, *, final):
    acc = (_dot(od_ref[0], w_ref[0:512, :]) + _dot(og_ref[0], w_ref[512:1024, :])
           + _dot(or_ref[0], w_ref[1024:1536, :]))
    xn = x_ref[0] + gate_ref[0] * acc
    if final:
        xn = _rms_rows(xn) * fg_ref[...]
    y_ref[0] = xn


def _outproj(o_dn, o_gla, o_rw, x, gate, w_out, fg, final, tm):
    b, t, d = x.shape
    tg = gate.shape[1]
    gspec = (pl.BlockSpec((1, 1, d), lambda i, j: (i, 0, 0)) if tg == 1
             else pl.BlockSpec((1, tm, d), lambda i, j: (i, j, 0)))
    ospec = pl.BlockSpec((1, tm, 512), lambda i, j: (i, j, 0))
    return pl.pallas_call(
        functools.partial(_outproj_kernel, final=final),
        grid=(b, t // tm),
        in_specs=[ospec, ospec, ospec,
                  pl.BlockSpec((1, tm, d), lambda i, j: (i, j, 0)),
                  gspec,
                  pl.BlockSpec((1536, d), lambda i, j: (0, 0)),
                  pl.BlockSpec((1, d), lambda i, j: (0, 0))],
        out_specs=pl.BlockSpec((1, tm, d), lambda i, j: (i, j, 0)),
        out_shape=jax.ShapeDtypeStruct((b, t, d), F32),
        compiler_params=_params(2),
        name="out_proj",
    )(o_dn, o_gla, o_rw, x, gate, w_out, fg)


def _dec_proj_kernel(x_ref, shift_ref, scale_ref, ng_ref, w_ref, o_ref):
    h = _modulated_norm(x_ref[...], ng_ref[...], shift_ref[...], scale_ref[...])
    o_ref[...] = _dot(h, w_ref[...])


def _dec_proj(x, shift, scale, ng, w_seg):
    n, d = x.shape
    seg = w_seg.shape[1]
    full = lambda shape: pl.BlockSpec(shape, lambda i: (0,) * len(shape))
    return pl.pallas_call(
        _dec_proj_kernel,
        grid=(1,),
        in_specs=[full((n, d)), full((n, d)), full((n, d)), full((1, d)), full((d, seg))],
        out_specs=full((n, seg)),
        out_shape=jax.ShapeDtypeStruct((n, seg), F32),
        compiler_params=_params(1),
        name="dec_proj",
    )(x, shift, scale, ng, w_seg)


def _dn_dec_kernel(p_ref, cst_ref, s_ref, cw_ref, gp_ref, dng_ref,
                   o_ref, cnew_ref, sn_ref, o_scr):
    bt = p_ref.shape[0]
    qkv = p_ref[:, 0:DN_QKV]
    c0, c1, c2 = cst_ref[:, 0, :], cst_ref[:, 1, :], cst_ref[:, 2, :]
    conv = _silu(c0 * cw_ref[0:1, :] + c1 * cw_ref[1:2, :] + c2 * cw_ref[2:3, :] + qkv * cw_ref[3:4, :])
    cnew_ref[:, 0, :] = c1
    cnew_ref[:, 1, :] = c2
    cnew_ref[:, 2, :] = qkv
    gseg = p_ref[:, 2048:DN_SEG]
    beta_all = _sigmoid(gseg)
    a_all = jnp.exp(-jnp.exp(gp_ref[0:1, :]) * _softplus(gseg + gp_ref[1:2, :]))
    for hh in range(DN_HEADS):
        lo = hh * DN_DK
        q = _l2norm_rows(conv[:, lo:lo + DN_DK]) * (DN_DK ** -0.5)
        k = _l2norm_rows(conv[:, 512 + lo:512 + lo + DN_DK])
        v = conv[:, 1024 + lo:1024 + lo + DN_DV]
        for bi in range(bt):
            k_col = _col_from_row(k[bi:bi + 1, :], DN_DK)
            q_col = _col_from_row(q[bi:bi + 1, :], DN_DK)
            a = a_all[bi:bi + 1, DN_HEADS + hh:DN_HEADS + hh + 1]
            beta = beta_all[bi:bi + 1, hh:hh + 1]
            s_old = s_ref[bi, hh]
            sk = jnp.sum(s_old * k_col, axis=0, keepdims=True)
            v_new = beta * (v[bi:bi + 1, :] - a * sk)
            s_new = s_old * a + k_col * v_new
            sn_ref[bi, hh] = s_new
            o_scr[bi:bi + 1, lo:lo + DN_DV] = jnp.sum(s_new * q_col, axis=0, keepdims=True)
    for hh in range(DN_HEADS):
        lo = hh * DN_DV
        z = p_ref[:, DN_QKV + lo:DN_QKV + lo + DN_DV]
        o_ref[:, lo:lo + DN_DV] = _rms_rows(o_scr[:, lo:lo + DN_DV]) * dng_ref[...] * _silu(z)


def _dn_dec(p, cst, s, cw, gp, dng, bt=8):
    n = p.shape[0]
    return pl.pallas_call(
        _dn_dec_kernel,
        grid=(n // bt,),
        in_specs=[pl.BlockSpec((bt, DN_SEG), lambda i: (i, 0)),
                  pl.BlockSpec((bt, 3, DN_QKV), lambda i: (i, 0, 0)),
                  pl.BlockSpec((bt, DN_HEADS, DN_DK, DN_DV), lambda i: (i, 0, 0, 0)),
                  pl.BlockSpec((4, DN_QKV), lambda i: (0, 0)),
                  pl.BlockSpec((2, LANE), lambda i: (0, 0)),
                  pl.BlockSpec((1, DN_DV), lambda i: (0, 0))],
        out_specs=[pl.BlockSpec((bt, 512), lambda i: (i, 0)),
                   pl.BlockSpec((bt, 3, DN_QKV), lambda i: (i, 0, 0)),
                   pl.BlockSpec((bt, DN_HEADS, DN_DK, DN_DV), lambda i: (i, 0, 0, 0))],
        out_shape=[jax.ShapeDtypeStruct((n, 512), F32),
                   jax.ShapeDtypeStruct((n, 3, DN_QKV), F32),
                   jax.ShapeDtypeStruct((n, DN_HEADS, DN_DK, DN_DV), F32)],
        scratch_shapes=[pltpu.VMEM((bt, 512), F32)],
        compiler_params=_params(1),
        name="dn_dec",
    )(p, cst, s, cw, gp, dng)


def _gla_dec_kernel(p_ref, s_ref, wf_ref, bf_ref, gng_ref, o_ref, sn_ref, o_scr):
    bt = p_ref.shape[0]
    pre = _dot(p_ref[:, 1536:GLA_SEG], wf_ref[...]) + bf_ref[...]
    alpha = jnp.exp((jnp.minimum(pre, 0.0) - jnp.log1p(jnp.exp(-jnp.abs(pre)))) * (1.0 / GLA_TAU))
    q_all = p_ref[:, 0:256] * (GLA_DK ** -0.5)
    k_all = p_ref[:, 256:512]
    for hh in range(GLA_HEADS):
        pair, off = (hh // 2) * LANE, (hh % 2) * GLA_DK
        v = p_ref[:, 512 + hh * GLA_DV:512 + (hh + 1) * GLA_DV]
        for bi in range(bt):
            a_col = _col_from_row(alpha[bi:bi + 1, pair:pair + LANE], GLA_DK, off)
            k_col = _col_from_row(k_all[bi:bi + 1, pair:pair + LANE], GLA_DK, off)
            q_col = _col_from_row(q_all[bi:bi + 1, pair:pair + LANE], GLA_DK, off)
            s_new = s_ref[bi, hh] * a_col + k_col * v[bi:bi + 1, :]
            sn_ref[bi, hh] = s_new
            o_scr[bi:bi + 1, hh * GLA_DV:(hh + 1) * GLA_DV] = jnp.sum(s_new * q_col, axis=0, keepdims=True)
    for hh in range(GLA_HEADS):
        lo = hh * GLA_DV
        z = p_ref[:, 1024 + lo:1024 + lo + GLA_DV]
        o_ref[:, lo:lo + GLA_DV] = _rms_rows(o_scr[:, lo:lo + GLA_DV]) * gng_ref[...] * _silu(z)


def _gla_dec(p, s, wf, bf, gng, bt=8):
    n = p.shape[0]
    return pl.pallas_call(
        _gla_dec_kernel,
        grid=(n // bt,),
        in_specs=[pl.BlockSpec((bt, GLA_SEG), lambda i: (i, 0)),
                  pl.BlockSpec((bt, GLA_HEADS, GLA_DK, GLA_DV), lambda i: (i, 0, 0, 0)),
                  pl.BlockSpec((LANE, 256), lambda i: (0, 0)),
                  pl.BlockSpec((1, 256), lambda i: (0, 0)),
                  pl.BlockSpec((1, GLA_DV), lambda i: (0, 0))],
        out_specs=[pl.BlockSpec((bt, 512), lambda i: (i, 0)),
                   pl.BlockSpec((bt, GLA_HEADS, GLA_DK, GLA_DV), lambda i: (i, 0, 0, 0))],
        out_shape=[jax.ShapeDtypeStruct((n, 512), F32),
                   jax.ShapeDtypeStruct((n, GLA_HEADS, GLA_DK, GLA_DV), F32)],
        scratch_shapes=[pltpu.VMEM((bt, 512), F32)],
        compiler_params=_params(1),
        name="gla_dec",
    )(p, s, wf, bf, gng)


def _rw_dec_kernel(p_ref, sh_ref, s_ref, mu_ref, rp_ref, w2_ref, a2_ref,
                   o_ref, shn_ref, sn_ref, o_scr):
    bt = p_ref.shape[0]
    seg_ones = _seg_ones()
    xs = p_ref[:, 0:RW_SHIFT]
    shn_ref[...] = xs
    xm = xs + (sh_ref[...] - xs) * mu_ref[...]
    r, k, v, kk, b, logw = _rw_features(xm, rp_ref, w2_ref, a2_ref, seg_ones)
    w = jnp.exp(logw)
    for hh in range(RW_HEADS):
        pair, off = (hh // 2) * LANE, (hh % 2) * RW_N
        sl = slice(hh * RW_N, (hh + 1) * RW_N)
        for bi in range(bt):
            row = slice(bi, bi + 1)
            v_col = _col_from_row(v[row, pair:pair + LANE], RW_N, off)
            s_old = s_ref[bi, hh]
            skk = jnp.sum(s_old * kk[row, sl], axis=1, keepdims=True)
            s_new = s_old * w[row, sl] - skk * b[row, sl] + v_col * k[row, sl]
            sn_ref[bi, hh] = s_new
            o_col = jnp.sum(s_new * r[row, sl], axis=1, keepdims=True)
            eye = _iota2((RW_N, RW_N), 0) == _iota2((RW_N, RW_N), 1)
            o_scr[row, sl] = jnp.sum(jnp.where(eye, o_col, 0.0), axis=0, keepdims=True)
    rz = p_ref[:, RW_SHIFT:RW_SEG]
    o_ref[...] = _rw_finish(o_scr[...], r, k, v, rz, rp_ref, seg_ones)


def _rw_dec(p, sh, s, mu, rp, w2, a2, bt=8):
    n = p.shape[0]
    return pl.pallas_call(
        _rw_dec_kernel,
        grid=(n // bt,),
        in_specs=[pl.BlockSpec((bt, RW_SEG), lambda i: (i, 0)),
                  pl.BlockSpec((bt, RW_SHIFT), lambda i: (i, 0)),
                  pl.BlockSpec((bt, RW_HEADS, RW_N, RW_N), lambda i: (i, 0, 0, 0)),
                  pl.BlockSpec((1, RW_SHIFT), lambda i: (0, 0)),
                  pl.BlockSpec((8, RW_C), lambda i: (0, 0)),
                  pl.BlockSpec((LANE, RW_C), lambda i: (0, 0)),
                  pl.BlockSpec((LANE, RW_C), lambda i: (0, 0))],
        out_specs=[pl.BlockSpec((bt, RW_C), lambda i: (i, 0)),
                   pl.BlockSpec((bt, RW_SHIFT), lambda i: (i, 0)),
                   pl.BlockSpec((bt, RW_HEADS, RW_N, RW_N), lambda i: (i, 0, 0, 0))],
        out_shape=[jax.ShapeDtypeStruct((n, RW_C), F32),
                   jax.ShapeDtypeStruct((n, RW_SHIFT), F32),
                   jax.ShapeDtypeStruct((n, RW_HEADS, RW_N, RW_N), F32)],
        scratch_shapes=[pltpu.VMEM((bt, RW_C), F32)],
        compiler_params=_params(1),
        name="rw_dec",
    )(p, sh, s, mu, rp, w2, a2)


def _layer_params(l, norm_g, w_in, dn_conv_w, dn_a_log, dn_dt_bias, dn_norm_g, gla_wf, gla_bf,
                  gla_norm_g, rw_mu, rw_w0, rw_w2, rw_a0, rw_a2, rw_k_k, rw_k_a, rw_r_k,
                  rw_ln_w, rw_ln_b, w_out):
    w = w_in[l]
    pad_cols = lambda a, n: jnp.pad(a, ((0, 0), (0, n - a.shape[1])))
    gp = jnp.zeros((2, LANE), F32)
    gp = gp.at[0, DN_HEADS:2 * DN_HEADS].set(dn_a_log[l]).at[1, DN_HEADS:2 * DN_HEADS].set(dn_dt_bias[l])
    zeros_r = jnp.zeros((1, RW_C), F32)
    rp = jnp.concatenate([rw_w0[l][None], rw_a0[l][None], rw_k_k[l][None], rw_k_a[l][None],
                          rw_r_k[l].reshape(1, RW_C), rw_ln_w[l][None], rw_ln_b[l][None], zeros_r], axis=0)
    return dict(
        ng=norm_g[l][None],
        w_dn=pad_cols(w[:, :DN_PROJ], DN_SEG).astype(BF16),
        w_gla=pad_cols(w[:, DN_PROJ:DN_PROJ + GLA_PROJ], GLA_SEG).astype(BF16),
        w_rw=w[:, DN_PROJ + GLA_PROJ:].astype(BF16),
        cw=dn_conv_w[l], gp=gp, dng=dn_norm_g[l][None],
        wf=jnp.pad(gla_wf[l], ((0, LANE - GLA_RANK), (0, 0))).astype(BF16),
        bf=gla_bf[l][None], gng=gla_norm_g[l][None],
        mu=rw_mu[l][None], rp=rp,
        w2=jnp.pad(rw_w2[l], ((0, LANE - 64), (0, 0))).astype(BF16),
        a2=jnp.pad(rw_a2[l], ((64, 0), (0, 0))).astype(BF16),
        w_out=w_out[l].astype(BF16))


def _gla_state_from_pairs(st):
    heads = []
    for hh in range(GLA_HEADS):
        p, hl = hh // 2, hh % 2
        blk = st[:, p, hl * GLA_DV:(hl + 1) * GLA_DV, hl * GLA_DK:(hl + 1) * GLA_DK]
        heads.append(jnp.swapaxes(blk, 1, 2))
    return jnp.stack(heads, axis=1)


def _rw_state_from_groups(s):
    heads = []
    for hh in range(RW_HEADS):
        g, hl = hh // RW_GROUP, hh % RW_GROUP
        heads.append(s[:, g, hl * RW_N:(hl + 1) * RW_N, hl * RW_N:(hl + 1) * RW_N])
    return jnp.stack(heads, axis=1)


def _prompt_tiles(t):
    tb = 256 if t % 256 == 0 else CHUNK
    tm = 512 if t % 512 == 0 else tb
    return tb, tm


def _prompt_trunk(x, mods, layers, final_g):
    b, t, _ = x.shape
    tb, tm = _prompt_tiles(t)
    states = []
    n_layers = len(layers)
    for l, lp in enumerate(layers):
        mod = mods[l]
        o_dn, conv_new, dn_new = _dn_prompt(x, mod, lp["ng"], lp["w_dn"], lp["cw"], lp["gp"], lp["dng"], tb)
        o_gla, gla_st = _gla_prompt(x, mod, lp["ng"], lp["w_gla"], lp["wf"], lp["bf"], lp["gng"], tb)
        o_rw, rs_new, rw_st = _rw_prompt(x, mod, lp["ng"], lp["w_rw"], lp["mu"], lp["rp"], lp["w2"], lp["a2"], tb)
        x = _outproj(o_dn, o_gla, o_rw, x, mod[:, 2:3, :], lp["w_out"], final_g, l == n_layers - 1, tm)
        states.append((conv_new, dn_new, _gla_state_from_pairs(gla_st), rs_new[:, 0, :],
                       _rw_state_from_groups(rw_st)))
    return x, states


def _sample_trunk(x, mods, layers, final_g, conv_s, dn_s, gla_s, rs_s, rw_s):
    n = x.shape[0]
    x = x.reshape(n, D_MODEL)
    states = []
    n_layers = len(layers)
    for l, lp in enumerate(layers):
        shift, scale, gate = mods[l][:, 0, :], mods[l][:, 1, :], mods[l][:, 2, :]
        p_dn = _dec_proj(x, shift, scale, lp["ng"], lp["w_dn"])
        p_gla = _dec_proj(x, shift, scale, lp["ng"], lp["w_gla"])
        p_rw = _dec_proj(x, shift, scale, lp["ng"], lp["w_rw"])
        o_dn, conv_new, dn_new = _dn_dec(p_dn, conv_s[l], dn_s[l], lp["cw"], lp["gp"], lp["dng"])
        o_gla, gla_new = _gla_dec(p_gla, gla_s[l], lp["wf"], lp["bf"], lp["gng"])
        o_rw, rs_new, rw_new = _rw_dec(p_rw, rs_s[l], rw_s[l], lp["mu"], lp["rp"], lp["w2"], lp["a2"])
        x = _outproj(o_dn[None], o_gla[None], o_rw[None], x[None], gate[None], lp["w_out"], final_g,
                     l == n_layers - 1, n)[0]
        states.append((conv_new, dn_new, gla_new, rs_new, rw_new))
    return x.reshape(n, 1, D_MODEL), states


def kernel(x_prompt, x_sample, c_prompt, c_sample, state_dn_conv, state_dn, state_gla, state_rwkv_shift, state_rwkv, norm_g, ada_w, ada_b, w_in, dn_conv_w, dn_a_log, dn_dt_bias, dn_norm_g, gla_wf, gla_bf, gla_norm_g, rw_mu, rw_w0, rw_w2, rw_a0, rw_a2, rw_k_k, rw_k_a, rw_r_k, rw_ln_w, rw_ln_b, w_out, final_norm_g):
    depth = w_in.shape[0]
    bp = x_prompt.shape[0]
    layers = [_layer_params(l, norm_g, w_in, dn_conv_w, dn_a_log, dn_dt_bias, dn_norm_g, gla_wf, gla_bf,
                            gla_norm_g, rw_mu, rw_w0, rw_w2, rw_a0, rw_a2, rw_k_k, rw_k_a, rw_r_k,
                            rw_ln_w, rw_ln_b, w_out) for l in range(depth)]
    final_g = final_norm_g[None]
    mod_all = _modulation(jnp.concatenate([c_prompt, c_sample], axis=0), ada_w, ada_b)
    mod_all = mod_all.reshape(depth, -1, 3, D_MODEL)
    mods_p = [mod_all[l, :bp] for l in range(depth)]
    mods_s = [mod_all[l, bp:] for l in range(depth)]

    y_p, st_p = _prompt_trunk(x_prompt, mods_p, layers, final_g)
    y_s, st_s = _sample_trunk(x_sample, mods_s, layers, final_g, state_dn_conv, state_dn, state_gla,
                              state_rwkv_shift, state_rwkv)
    stack = lambda sts, i: jnp.stack([s[i] for s in sts])
    return (y_p, y_s,
            stack(st_p, 0), stack(st_p, 1), stack(st_p, 2), stack(st_p, 3), stack(st_p, 4),
            stack(st_s, 0), stack(st_s, 1), stack(st_s, 2), stack(st_s, 3), stack(st_s, 4))
```

```python
import functools

import jax
import jax.numpy as jnp
from jax import lax
from jax.experimental import pallas as pl
from jax.experimental.pallas import tpu as pltpu

F32 = jnp.float32
BF16 = jnp.bfloat16

D_MODEL = 1024
EPS = 1e-6
CHUNK = 64
SUB = 16
NEG = -1e30

DN_HEADS, DN_DK, DN_DV = 4, 128, 128
GLA_HEADS, GLA_DK, GLA_DV, GLA_RANK, GLA_TAU = 4, 64, 128, 16, 16.0
RW_HEADS, RW_N = 8, 64
RW_GROUP = 4
RW_LN_EPS = 64e-5
DN_QKV = 1536
RW_C = 512
RW_SHIFT = 1664
DN_SEG, GLA_SEG, RW_SEG = 2176, 1664, 2176
DN_PROJ, GLA_PROJ = 2056, 1552
LANE = 128
VMEM_LIMIT = 56 * 1024 * 1024


def _dot(a, b):
    return jnp.dot(a.astype(BF16), b.astype(BF16), preferred_element_type=F32)


def _dot_nt(a, b):
    return lax.dot_general(a.astype(BF16), b.astype(BF16), (((1,), (1,)), ((), ())),
                           preferred_element_type=F32)


def _dot_tn(a, b):
    return lax.dot_general(a.astype(BF16), b.astype(BF16), (((0,), (0,)), ((), ())),
                           preferred_element_type=F32)


def _sigmoid(x):
    return jax.nn.sigmoid(x)


def _silu(x):
    return x * jax.nn.sigmoid(x)


def _softplus(x):
    return jnp.maximum(x, 0.0) + jnp.log1p(jnp.exp(-jnp.abs(x)))


def _iota2(shape, dim):
    return lax.broadcasted_iota(jnp.int32, shape, dim)


def _modulated_norm(x, ng, shift, scale):
    ms = jnp.mean(x * x, axis=-1, keepdims=True)
    return x * lax.rsqrt(ms + EPS) * ng * (1.0 + scale) + shift


def _l2norm_rows(x):
    return x * lax.rsqrt(jnp.sum(x * x, axis=-1, keepdims=True) + EPS)


def _rms_rows(x):
    return x * lax.rsqrt(jnp.mean(x * x, axis=-1, keepdims=True) + EPS)


def _seg_sum(x, seg_ones):
    hi = x.astype(BF16)
    lo = (x - hi.astype(F32)).astype(BF16)
    return (jnp.dot(hi, seg_ones, preferred_element_type=F32)
            + jnp.dot(lo, seg_ones, preferred_element_type=F32))


def _split3(x):
    hi = x.astype(BF16)
    rest = x - hi.astype(F32)
    mid = rest.astype(BF16)
    return hi, mid, (rest - mid.astype(F32)).astype(BF16)


def _sum_dot(sel01, x):
    return jnp.dot(jnp.concatenate([sel01] * 3, axis=1), jnp.concatenate(_split3(x), axis=0),
                   preferred_element_type=F32)


def _chunk_tri(n):
    r, c = _iota2((n, n), 0), _iota2((n, n), 1)
    return (((r >> 6) == (c >> 6)) & (r >= c)).astype(BF16)


def _bd_masks(n=CHUNK):
    r, c = _iota2((n, n), 0), _iota2((n, n), 1)
    same = (r >> 6) == (c >> 6)
    return dict(
        causal=same & (r >= c), strict=same & (r > c), eye=(r == c).astype(F32), same=same,
        bd16=(r >> 4) == (c >> 4),
        off32=((r >> 5) == (c >> 5)) & ((r >> 4) != (c >> 4)),
        off64=same & ((r >> 5) != (c >> 5)))


def _expand_rows(a, n_heads, shift):
    head = _iota2(a.shape, 1) >> shift
    zero = jnp.zeros_like(a)
    return jnp.concatenate([jnp.where(head == hh, a, zero) for hh in range(n_heads)], axis=0)


def _sbs_masks(n_heads):
    shape = (CHUNK, n_heads * CHUNK)
    r, c = _iota2(shape, 0), _iota2(shape, 1) & (CHUNK - 1)
    return dict(
        causal=r >= c, strict=r > c, eye=(r == c).astype(F32),
        bd16=(r >> 4) == (c >> 4),
        off32=((r >> 5) == (c >> 5)) & ((r >> 4) != (c >> 4)),
        off64=(r >> 5) != (c >> 5))


def _unit_lower_inverse(Ls, m, n_heads):
    bd = lambda a: _expand_rows(a.astype(BF16), n_heads, 6)
    n = range(len(Ls))
    Ld = [jnp.where(m["bd16"], L, 0.0) for L in Ls]
    X = [m["eye"] - a for a in Ld]
    P = [_dot(a, bd(a)) for a in Ld]
    for _ in range(2):
        t = [_dot(jnp.concatenate([P[i], X[i]], axis=0), bd(P[i])) for i in n]
        P = [a[0:CHUNK] for a in t]
        X = [X[i] + t[i][CHUNK:] for i in n]
    X = [X[i] + _dot(X[i], bd(P[i])) for i in n]
    for off in ("off32", "off64"):
        Y = [_dot(X[i], bd(jnp.where(m[off], Ls[i], 0.0))) for i in n]
        X = [X[i] - _dot(Y[i], bd(X[i])) for i in n]
    return X


def _col_from_row(row, n, lane_off=0):
    w = row.shape[1]
    sel = (_iota2((n, w), 0) + lane_off) == _iota2((n, w), 1)
    return jnp.sum(jnp.where(sel, row, 0.0), axis=1, keepdims=True)


def _params(n_axes):
    return pltpu.CompilerParams(dimension_semantics=("arbitrary",) * n_axes,
                                vmem_limit_bytes=VMEM_LIMIT)


def _mod_kernel(c_ref, w_ref, b_ref, o_ref):
    o_ref[0] = _dot(_silu(c_ref[...]), w_ref[0]) + b_ref[0]


def _modulation(c_all, ada_w, ada_b):
    depth, d, n = ada_w.shape
    rows = c_all.shape[0]
    tn = 1024
    return pl.pallas_call(
        _mod_kernel,
        grid=(depth, n // tn),
        in_specs=[pl.BlockSpec((rows, d), lambda l, j: (0, 0)),
                  pl.BlockSpec((1, d, tn), lambda l, j: (l, 0, j)),
                  pl.BlockSpec((1, 1, tn), lambda l, j: (l, 0, j))],
        out_specs=pl.BlockSpec((1, rows, tn), lambda l, j: (l, 0, j)),
        out_shape=jax.ShapeDtypeStruct((depth, rows, n), F32),
        compiler_params=_params(2),
        name="adaln_mod",
    )(c_all, ada_w, ada_b.reshape(depth, 1, n))


def _dn_prompt_kernel(x_ref, mod_ref, ng_ref, w_ref, cw_ref, gp_ref, dng_ref,
                      o_ref, conv_ref, s_ref,
                      pbuf, qs, ks, vs, gb, gcs, os_, U_s, W_s, Q_s, A_s, K_s, E_s, S):
    t = pl.program_id(1)
    tb = x_ref.shape[1]
    nb = DN_HEADS * CHUNK
    m = _sbs_masks(DN_HEADS)
    lane_c = _iota2((CHUNK, LANE), 1)
    sbs_head = _iota2((CHUNK, nb), 1) >> 6
    gate_lanes = ((_iota2((8, LANE), 1) >= DN_HEADS) & (_iota2((8, LANE), 1) < 2 * DN_HEADS)).astype(BF16)
    heads = range(DN_HEADS)

    def stack(f):
        return jnp.concatenate([f(hh) for hh in heads], axis=0)

    def stack_heads(a):
        return stack(lambda hh: a[:, hh * DN_DK:(hh + 1) * DN_DK])

    def per_head_lanes(col_of):
        return jnp.concatenate([jnp.broadcast_to(col_of(hh), (CHUNK, DN_DK)) for hh in heads], axis=1)

    @pl.when(t == 0)
    def _():
        pbuf[0:8, :] = jnp.zeros((8, DN_SEG), F32)
        S[...] = jnp.zeros_like(S)

    h = _modulated_norm(x_ref[0], ng_ref[...], mod_ref[0, 0:1, :], mod_ref[0, 1:2, :])
    pbuf[8:8 + tb, :] = _dot(h, w_ref[...])

    conv = pbuf[5:5 + tb, 0:DN_QKV] * cw_ref[0:1, :]
    for j in range(1, 4):
        conv = conv + pbuf[5 + j:5 + j + tb, 0:DN_QKV] * cw_ref[j:j + 1, :]
    conv = _silu(conv)
    for hh in range(DN_HEADS):
        lo = hh * DN_DK
        qs[:, lo:lo + DN_DK] = _l2norm_rows(conv[:, lo:lo + DN_DK]) * (DN_DK ** -0.5)
        ks[:, lo:lo + DN_DK] = _l2norm_rows(conv[:, 512 + lo:512 + lo + DN_DK])
    vs[...] = conv[:, 1024:DN_QKV]
    gseg = pbuf[8:8 + tb, 2048:DN_SEG]
    lane = _iota2((tb, LANE), 1)
    gb[...] = jnp.where(lane < DN_HEADS, _sigmoid(gseg),
                        -jnp.exp(gp_ref[0:1, :]) * _softplus(gseg + gp_ref[1:2, :]))

    gcs[...] = _sum_dot(_chunk_tri(tb), gb[...])

    def chunk_matrices(c):
        rows = pl.ds(c * CHUNK, CHUNK)
        gbc = gb[rows, :]
        gc = gcs[rows, :]
        g_of = lambda hh: gc[:, DN_HEADS + hh:DN_HEADS + hh + 1]
        g_col = jnp.zeros((CHUNK, nb), F32)
        for hh in heads:
            g_col = jnp.where(sbs_head == hh, g_of(hh), g_col)
        g_sel = stack(lambda hh: jnp.where(lane_c == DN_HEADS + hh, gc, 0.0))
        g_row = lax.dot_general(jnp.concatenate([gate_lanes] * 3, axis=1),
                                jnp.concatenate(_split3(g_sel), axis=1),
                                (((1,), (1,)), ((), ())), preferred_element_type=F32)[0:1, :]
        decay = jnp.exp(jnp.where(m["causal"], g_col - g_row, NEG))
        k_all = ks[rows, :]
        q_all = qs[rows, :]
        beta_l = per_head_lanes(lambda hh: gbc[:, hh:hh + 1])
        g_l = per_head_lanes(g_of)
        eg_l = jnp.exp(g_l)
        kb_all = k_all * beta_l
        expand = lambda a: _expand_rows(a.astype(BF16), DN_HEADS, 7)
        kq = _dot_nt(jnp.concatenate([kb_all, q_all], axis=0), expand(k_all))
        L = jnp.where(m["strict"], kq[0:CHUNK] * decay, 0.0)
        A_s[c] = _expand_rows((kq[CHUNK:2 * CHUNK] * decay).astype(BF16), DN_HEADS, 6)
        Q_s[c] = stack_heads(q_all * eg_l).astype(BF16)
        K_s[c] = stack_heads(k_all * jnp.exp(g_l[CHUNK - 1:CHUNK, :] - g_l)).astype(BF16)
        E_s[c] = jnp.exp(gc[CHUNK - 8:CHUNK, :])
        return L, expand(vs[rows, :] * beta_l), expand(kb_all * eg_l)

    n_chunks = tb // CHUNK
    pre = [chunk_matrices(c) for c in range(n_chunks)]
    inv = _unit_lower_inverse([p[0] for p in pre], m, DN_HEADS)
    for c in range(n_chunks):
        X = inv[c].astype(BF16)
        U_s[c] = stack_heads(_dot(X, pre[c][1]))
        W_s[c] = stack_heads(_dot(X, pre[c][2])).astype(BF16)

    def recurrence(c, carry):
        rows = pl.ds(pl.multiple_of(c * CHUNK, CHUNK), CHUNK)
        w, qe, kh = W_s[c], Q_s[c], K_s[c]
        e_last = E_s[c][7:8, :]
        ws = []
        for hh in heads:
            sl = slice(hh * CHUNK, (hh + 1) * CHUNK)
            ws.append(jnp.dot(jnp.concatenate([w[sl], qe[sl]], axis=0), S[hh].astype(BF16),
                              preferred_element_type=F32))
        v_new = U_s[c] - jnp.concatenate([x[0:CHUNK] for x in ws], axis=0)
        vb = v_new.astype(BF16)
        o_st = jnp.concatenate([x[CHUNK:2 * CHUNK] for x in ws], axis=0) + jnp.dot(
            A_s[c], vb, preferred_element_type=F32)
        for hh in heads:
            sl = slice(hh * CHUNK, (hh + 1) * CHUNK)
            S[hh] = S[hh] * e_last[:, DN_HEADS + hh:DN_HEADS + hh + 1] + _dot_tn(kh[sl], vb[sl])
            os_[rows, hh * DN_DV:(hh + 1) * DN_DV] = o_st[sl]
        return carry

    lax.fori_loop(0, n_chunks, recurrence, 0)

    for hh in heads:
        lo = hh * DN_DV
        z = pbuf[8:8 + tb, DN_QKV + lo:DN_QKV + lo + DN_DV]
        o_ref[0, :, lo:lo + DN_DV] = (_rms_rows(os_[:, lo:lo + DN_DV]) * dng_ref[...] * _silu(z)).astype(o_ref.dtype)

    pbuf[0:8, :] = pbuf[tb:tb + 8, :]

    @pl.when(t == pl.num_programs(1) - 1)
    def _():
        conv_ref[0] = pbuf[tb + 5:tb + 8, 0:DN_QKV]
        s_ref[0] = S[...]


def _dn_prompt(x, mod, ng, w_seg, cw, gp, dng, tb):
    b, t, d = x.shape
    nch, nb = tb // CHUNK, DN_HEADS * CHUNK
    return pl.pallas_call(
        _dn_prompt_kernel,
        grid=(b, t // tb),
        in_specs=[pl.BlockSpec((1, tb, d), lambda i, j: (i, j, 0)),
                  pl.BlockSpec((1, 3, d), lambda i, j: (i, 0, 0)),
                  pl.BlockSpec((1, d), lambda i, j: (0, 0)),
                  pl.BlockSpec((d, DN_SEG), lambda i, j: (0, 0)),
                  pl.BlockSpec((4, DN_QKV), lambda i, j: (0, 0)),
                  pl.BlockSpec((2, LANE), lambda i, j: (0, 0)),
                  pl.BlockSpec((1, DN_DV), lambda i, j: (0, 0))],
        out_specs=[pl.BlockSpec((1, tb, 512), lambda i, j: (i, j, 0)),
                   pl.BlockSpec((1, 3, DN_QKV), lambda i, j: (i, 0, 0)),
                   pl.BlockSpec((1, DN_HEADS, DN_DK, DN_DV), lambda i, j: (i, 0, 0, 0))],
        out_shape=[jax.ShapeDtypeStruct((b, t, 512), BF16),
                   jax.ShapeDtypeStruct((b, 3, DN_QKV), F32),
                   jax.ShapeDtypeStruct((b, DN_HEADS, DN_DK, DN_DV), F32)],
        scratch_shapes=[pltpu.VMEM((tb + 8, DN_SEG), F32),
                        pltpu.VMEM((tb, 512), F32), pltpu.VMEM((tb, 512), F32),
                        pltpu.VMEM((tb, 512), F32), pltpu.VMEM((tb, LANE), F32),
                        pltpu.VMEM((tb, LANE), F32), pltpu.VMEM((tb, 512), F32),
                        pltpu.VMEM((nch, nb, DN_DV), F32), pltpu.VMEM((nch, nb, DN_DV), BF16),
                        pltpu.VMEM((nch, nb, DN_DK), BF16), pltpu.VMEM((nch, nb, nb), BF16),
                        pltpu.VMEM((nch, nb, DN_DK), BF16), pltpu.VMEM((nch, 8, LANE), F32),
                        pltpu.VMEM((DN_HEADS, DN_DK, DN_DV), F32)],
        compiler_params=_params(2),
        name="dn_prompt",
    )(x, mod, ng, w_seg, cw, gp, dng)


def _gla_prompt_kernel(x_ref, mod_ref, ng_ref, w_ref, wf_ref, bf_ref, gng_ref,
                       o_ref, s_ref,
                       pbuf, lfs, ST):
    t = pl.program_id(1)
    tb = x_ref.shape[1]
    m = _bd_masks()
    ltri = _chunk_tri(CHUNK)
    n_sub = CHUNK // SUB
    lane = _iota2((CHUNK, LANE), 1)
    head1 = lane >= GLA_DK
    rowblk = _iota2((CHUNK, LANE), 0) >> 4
    rowloc = _iota2((CHUNK, LANE), 0) & (SUB - 1)
    er = _iota2((2 * SUB, SUB * LANE), 0)
    ec = _iota2((2 * SUB, SUB * LANE), 1)
    eblk = (((ec >> 7) == (er & (SUB - 1))) & (((ec & (LANE - 1)) >= GLA_DK) == (er >= SUB))).astype(BF16)
    st_r = _iota2((2 * GLA_DV, LANE), 0)
    st_c = _iota2((2 * GLA_DV, LANE), 1)
    st_mask = (st_r >= GLA_DV) == (st_c >= GLA_DK)

    @pl.when(t == 0)
    def _():
        ST[...] = jnp.zeros_like(ST)

    h = _modulated_norm(x_ref[0], ng_ref[...], mod_ref[0, 0:1, :], mod_ref[0, 1:2, :])
    pbuf[...] = _dot(h, w_ref[...])
    pre = _dot(pbuf[:, 1536:GLA_SEG], wf_ref[...]) + bf_ref[...]
    lfs[...] = (jnp.minimum(pre, 0.0) - jnp.log1p(jnp.exp(-jnp.abs(pre)))) * (1.0 / GLA_TAU)

    def chunk(c, carry):
        r0 = pl.multiple_of(c * CHUNK, CHUNK)
        rows = pl.ds(r0, CHUNK)
        g_all = _sum_dot(ltri, lfs[rows, :])
        for p in range(GLA_HEADS // 2):
            lo = p * LANE
            G = g_all[:, lo:lo + LANE]
            q = pbuf[rows, lo:lo + LANE] * (GLA_DK ** -0.5)
            k = pbuf[rows, 256 + lo:256 + lo + LANE]
            v = pbuf[rows, 512 + 2 * lo:512 + 2 * lo + 2 * GLA_DV]
            st = ST[p]
            o_inter = _dot_nt(q * jnp.exp(G), st)

            G3 = G.reshape(n_sub, SUB, LANE)
            q3 = q.reshape(n_sub, SUB, LANE)

            def pick(a3, i):
                return jnp.broadcast_to(a3[:, i:i + 1, :], (n_sub, SUB, LANE)).reshape(CHUNK, LANE)

            ref_row = pick(G3, 0)
            q_t = q * jnp.exp(G - ref_row)
            q_parts, k_parts = [], []
            for i in range(1, n_sub):
                q_parts.append(jnp.where(rowblk == i, q_t, 0.0))
                ref_i = G[i * SUB:i * SUB + 1, :]
                k_parts.append(k * jnp.exp(jnp.where(rowblk < i, ref_i - G, NEG)))
            q_big = jnp.concatenate(q_parts, axis=1)
            k_big = jnp.concatenate(k_parts, axis=1)
            head1_big = jnp.concatenate([head1] * (n_sub - 1), axis=1)
            q_both = jnp.concatenate([jnp.where(head1_big, 0.0, q_big),
                                      jnp.where(head1_big, q_big, 0.0)], axis=0)
            a_off = _dot_nt(q_both, k_big)

            slabs = []
            for i in range(SUB):
                dec = jnp.exp(jnp.where(rowloc <= i, pick(G3, i) - G, NEG))
                slabs.append((k * pick(q3, i) * dec).astype(BF16))
            p_stack = jnp.concatenate(slabs, axis=1)
            a_diag = lax.dot_general(eblk, p_stack, (((1,), (1,)), ((), ())),
                                     preferred_element_type=F32)

            kv = _dot_tn(v, k * jnp.exp(G[CHUNK - 1:CHUNK, :] - G))
            ST[p] = st * jnp.exp(G[CHUNK - 1:CHUNK, :]) + jnp.where(st_mask, kv, 0.0)

            for hl in range(2):
                hh = 2 * p + hl
                a_d = a_diag[hl * SUB:(hl + 1) * SUB, :]
                a_h = a_off[hl * CHUNK:(hl + 1) * CHUNK, :] + jnp.where(
                    m["bd16"], jnp.concatenate([a_d] * n_sub, axis=0), 0.0)
                v_h = v[:, hl * GLA_DV:(hl + 1) * GLA_DV]
                o = o_inter[:, hl * GLA_DV:(hl + 1) * GLA_DV] + _dot(a_h, v_h)
                z = pbuf[rows, 1024 + hh * GLA_DV:1024 + (hh + 1) * GLA_DV]
                o_ref[0, rows, hh * GLA_DV:(hh + 1) * GLA_DV] = (
                    _rms_rows(o) * gng_ref[...] * _silu(z)).astype(o_ref.dtype)
        return carry

    lax.fori_loop(0, tb // CHUNK, chunk, 0)

    @pl.when(t == pl.num_programs(1) - 1)
    def _():
        s_ref[0] = ST[...]


def _gla_prompt(x, mod, ng, w_seg, wf, bf, gng, tb):
    b, t, d = x.shape
    return pl.pallas_call(
        _gla_prompt_kernel,
        grid=(b, t // tb),
        in_specs=[pl.BlockSpec((1, tb, d), lambda i, j: (i, j, 0)),
                  pl.BlockSpec((1, 3, d), lambda i, j: (i, 0, 0)),
                  pl.BlockSpec((1, d), lambda i, j: (0, 0)),
                  pl.BlockSpec((d, GLA_SEG), lambda i, j: (0, 0)),
                  pl.BlockSpec((LANE, 256), lambda i, j: (0, 0)),
                  pl.BlockSpec((1, 256), lambda i, j: (0, 0)),
                  pl.BlockSpec((1, GLA_DV), lambda i, j: (0, 0))],
        out_specs=[pl.BlockSpec((1, tb, 512), lambda i, j: (i, j, 0)),
                   pl.BlockSpec((1, 2, 2 * GLA_DV, LANE), lambda i, j: (i, 0, 0, 0))],
        out_shape=[jax.ShapeDtypeStruct((b, t, 512), BF16),
                   jax.ShapeDtypeStruct((b, 2, 2 * GLA_DV, LANE), F32)],
        scratch_shapes=[pltpu.VMEM((tb, GLA_SEG), F32), pltpu.VMEM((tb, 256), F32),
                        pltpu.VMEM((2, 2 * GLA_DV, LANE), F32)],
        compiler_params=_params(2),
        name="gla_prompt",
    )(x, mod, ng, w_seg, wf, bf, gng)


def _rw_features(xm, rp_ref, w2_ref, a2_ref, seg_ones):
    r = xm[:, 0:RW_C]
    k = xm[:, RW_C:2 * RW_C]
    v = xm[:, 2 * RW_C:3 * RW_C]
    lo = xm[:, 3 * RW_C:RW_SHIFT]
    w_log = -_softplus(-(rp_ref[0:1, :] + _dot(jnp.tanh(lo), w2_ref[...]))) - 0.5
    logw = -jnp.exp(w_log)
    a = _sigmoid(rp_ref[1:2, :] + _dot(lo, a2_ref[...]))
    kk = k * rp_ref[2:3, :]
    kk = kk * lax.rsqrt(_seg_sum(kk * kk, seg_ones) + EPS)
    k = k * (1.0 + (a - 1.0) * rp_ref[3:4, :])
    return r, k, v, kk, kk * a, logw


def _rw_finish(o, r, k, v, rz, rp_ref, seg_ones):
    mu = _seg_sum(o, seg_ones) * (1.0 / RW_N)
    dev = o - mu
    var = _seg_sum(dev * dev, seg_ones) * (1.0 / RW_N)
    on = dev * lax.rsqrt(var + RW_LN_EPS) * rp_ref[5:6, :] + rp_ref[6:7, :]
    bonus = _seg_sum(r * k * rp_ref[4:5, :], seg_ones) * v
    return (on + bonus) * _silu(rz)


def _seg_ones():
    return ((_iota2((RW_C, RW_C), 0) >> 6) == (_iota2((RW_C, RW_C), 1) >> 6)).astype(BF16)


def _rw_prompt_kernel(x_ref, mod_ref, ng_ref, w_ref, mu_ref, rp_ref, w2_ref, a2_ref,
                      o_ref, sh_ref, s_ref,
                      pbuf, rs, ks, vs, kks, bs, lws, gs, os_,
                      P1_s, KT_s, P2_s, RH_s, AB_s, KB_s, E_s, S):
    t = pl.program_id(1)
    tb = x_ref.shape[1]
    gw = RW_GROUP * RW_N
    m = _sbs_masks(RW_GROUP)
    same_head = (_iota2((gw, gw), 0) >> 6) == (_iota2((gw, gw), 1) >> 6)
    seg_ones = _seg_ones()
    mid = CHUNK // 2

    def expand(a):
        return _expand_rows(a.astype(BF16), RW_GROUP, 6)

    @pl.when(t == 0)
    def _():
        pbuf[0:8, :] = jnp.zeros((8, RW_SEG), F32)
        S[...] = jnp.zeros_like(S)

    h = _modulated_norm(x_ref[0], ng_ref[...], mod_ref[0, 0:1, :], mod_ref[0, 1:2, :])
    pbuf[8:8 + tb, :] = _dot(h, w_ref[...])
    xs = pbuf[8:8 + tb, 0:RW_SHIFT]
    prev = pbuf[7:7 + tb, 0:RW_SHIFT]
    xm = xs + (prev - xs) * mu_ref[...]
    r, k, v, kk, b, logw = _rw_features(xm, rp_ref, w2_ref, a2_ref, seg_ones)
    rs[...] = r
    ks[...] = k
    vs[...] = v
    kks[...] = kk
    bs[...] = b
    lws[...] = logw
    gs[...] = _sum_dot(_chunk_tri(tb), logw)

    def chunk_matrices(c, g):
        rows = pl.ds(c * CHUNK, CHUNK)
        sl = slice(g * gw, (g + 1) * gw)
        G = gs[rows, sl]
        Gx = G - lws[rows, sl]
        ref = G[mid:mid + 1, :]
        r_, k_, v_, kk_, b_ = rs[rows, sl], ks[rows, sl], vs[rows, sl], kks[rows, sl], bs[rows, sl]
        e_out = jnp.exp(ref - G)
        lhs = jnp.concatenate([kk_ * jnp.exp(Gx - ref), r_ * jnp.exp(G - ref)], axis=0)
        rhs = jnp.concatenate([expand(b_ * e_out), expand(k_ * e_out)], axis=0)
        mm = _dot_nt(lhs, rhs)
        Lb = jnp.where(m["strict"], mm[0:CHUNK, 0:gw], 0.0)
        Mk = jnp.where(m["strict"], mm[0:CHUNK, gw:], 0.0)
        Arb = jnp.where(m["causal"], mm[CHUNK:, 0:gw], 0.0)
        Ark = jnp.where(m["causal"], mm[CHUNK:, gw:], 0.0)
        kv = _dot(jnp.concatenate([Mk, Ark], axis=0), expand(v_))
        P2_s[c, g] = kv[CHUNK:]
        RH_s[c, g] = (r_ * jnp.exp(G)).astype(BF16)
        AB_s[c, g] = Arb.astype(BF16)
        e_end = jnp.exp(G[CHUNK - 1:CHUNK, :] - G)
        KB_s[c, g] = jnp.concatenate([k_ * e_end, -(b_ * e_end)], axis=0).astype(BF16)
        E_s[c, g] = jnp.exp(G[CHUNK - 8:CHUNK, :])
        return Lb, jnp.concatenate([expand(kv[0:CHUNK]), expand(kk_ * jnp.exp(Gx))], axis=1)

    n_chunks = tb // CHUNK
    problems = [(c, g) for c in range(n_chunks) for g in range(RW_HEADS // RW_GROUP)]
    pre = [chunk_matrices(c, g) for c, g in problems]
    inv = _unit_lower_inverse([p[0] for p in pre], m, RW_GROUP)
    for (c, g), X, p in zip(problems, inv, pre):
        sol = _dot(X, p[1])
        P1_s[c, g] = sol[:, 0:gw]
        KT_s[c, g] = sol[:, gw:].astype(BF16)

    def recurrence(c, carry):
        rows = pl.ds(pl.multiple_of(c * CHUNK, CHUNK), CHUNK)
        for g in range(RW_HEADS // RW_GROUP):
            sl = slice(g * gw, (g + 1) * gw)
            s_g = S[g]
            khu = _dot_nt(jnp.concatenate([KT_s[c, g], RH_s[c, g]], axis=0), s_g)
            u_ = P1_s[c, g] + khu[0:CHUNK]
            o = khu[CHUNK:2 * CHUNK] + P2_s[c, g] - _dot(AB_s[c, g], expand(u_))
            vu = jnp.concatenate([vs[rows, sl], u_], axis=0)
            S[g] = s_g * E_s[c, g][7:8, :] + jnp.where(same_head, _dot_tn(vu, KB_s[c, g]), 0.0)
            os_[rows, sl] = o
        return carry

    lax.fori_loop(0, n_chunks, recurrence, 0)

    rz = pbuf[8:8 + tb, RW_SHIFT:RW_SEG]
    o_ref[0] = _rw_finish(os_[...], rs[...], ks[...], vs[...], rz, rp_ref, seg_ones).astype(o_ref.dtype)
    pbuf[0:8, :] = pbuf[tb:tb + 8, :]

    @pl.when(t == pl.num_programs(1) - 1)
    def _():
        sh_ref[0] = pbuf[tb + 7:tb + 8, 0:RW_SHIFT]
        s_ref[0] = S[...]


def _rw_prompt(x, mod, ng, w_seg, mu, rp, w2, a2, tb):
    b, t, d = x.shape
    nch, ng_, gw = tb // CHUNK, RW_HEADS // RW_GROUP, RW_GROUP * RW_N
    return pl.pallas_call(
        _rw_prompt_kernel,
        grid=(b, t // tb),
        in_specs=[pl.BlockSpec((1, tb, d), lambda i, j: (i, j, 0)),
                  pl.BlockSpec((1, 3, d), lambda i, j: (i, 0, 0)),
                  pl.BlockSpec((1, d), lambda i, j: (0, 0)),
                  pl.BlockSpec((d, RW_SEG), lambda i, j: (0, 0)),
                  pl.BlockSpec((1, RW_SHIFT), lambda i, j: (0, 0)),
                  pl.BlockSpec((8, RW_C), lambda i, j: (0, 0)),
                  pl.BlockSpec((LANE, RW_C), lambda i, j: (0, 0)),
                  pl.BlockSpec((LANE, RW_C), lambda i, j: (0, 0))],
        out_specs=[pl.BlockSpec((1, tb, RW_C), lambda i, j: (i, j, 0)),
                   pl.BlockSpec((1, 1, RW_SHIFT), lambda i, j: (i, 0, 0)),
                   pl.BlockSpec((1, ng_, gw, gw), lambda i, j: (i, 0, 0, 0))],
        out_shape=[jax.ShapeDtypeStruct((b, t, RW_C), BF16),
                   jax.ShapeDtypeStruct((b, 1, RW_SHIFT), F32),
                   jax.ShapeDtypeStruct((b, ng_, gw, gw), F32)],
        scratch_shapes=[pltpu.VMEM((tb + 8, RW_SEG), F32)]
        + [pltpu.VMEM((tb, RW_C), F32) for _ in range(8)]
        + [pltpu.VMEM((nch, ng_, CHUNK, gw), F32), pltpu.VMEM((nch, ng_, CHUNK, gw), BF16),
           pltpu.VMEM((nch, ng_, CHUNK, gw), F32), pltpu.VMEM((nch, ng_, CHUNK, gw), BF16),
           pltpu.VMEM((nch, ng_, CHUNK, gw), BF16), pltpu.VMEM((nch, ng_, 2 * CHUNK, gw), BF16),
           pltpu.VMEM((nch, ng_, 8, gw), F32),
           pltpu.VMEM((ng_, gw, gw), F32)],
        compiler_params=_params(2),
        name="rw_prompt",
    )(x, mod, ng, w_seg, mu, rp, w2, a2)


def _outproj_kernel(od_ref, og_ref, or_ref, x_ref, gate_ref, w_ref, fg_ref, y_ref, *, final):
    acc = (_dot(od_ref[0], w_ref[0:512, :]) + _dot(og_ref[0], w_ref[512:1024, :])
           + _dot(or_ref[0], w_ref[1024:1536, :]))
    xn = x_ref[0] + gate_ref[0] * acc
    if final:
        xn = _rms_rows(xn) * fg_ref[...]
    y_ref[0] = xn


def _outproj(o_dn, o_gla, o_rw, x, gate, w_out, fg, final, tm):
    b, t, d = x.shape
    tg = gate.shape[1]
    gspec = (pl.BlockSpec((1, 1, d), lambda i, j: (i, 0, 0)) if tg == 1
             else pl.BlockSpec((1, tm, d), lambda i, j: (i, j, 0)))
    ospec = pl.BlockSpec((1, tm, 512), lambda i, j: (i, j, 0))
    return pl.pallas_call(
        functools.partial(_outproj_kernel, final=final),
        grid=(b, t // tm),
        in_specs=[ospec, ospec, ospec,
                  pl.BlockSpec((1, tm, d), lambda i, j: (i, j, 0)),
                  gspec,
                  pl.BlockSpec((1536, d), lambda i, j: (0, 0)),
                  pl.BlockSpec((1, d), lambda i, j: (0, 0))],
        out_specs=pl.BlockSpec((1, tm, d), lambda i, j: (i, j, 0)),
        out_shape=jax.ShapeDtypeStruct((b, t, d), F32),
        compiler_params=_params(2),
        name="out_proj",
    )(o_dn, o_gla, o_rw, x, gate, w_out, fg)


def _dec_proj_kernel(x_ref, shift_ref, scale_ref, ng_ref, w_ref, o_ref):
    h = _modulated_norm(x_ref[...], ng_ref[...], shift_ref[...], scale_ref[...])
    o_ref[...] = _dot(h, w_ref[...])


def _dec_proj(x, shift, scale, ng, w_seg):
    n, d = x.shape
    seg = w_seg.shape[1]
    full = lambda shape: pl.BlockSpec(shape, lambda i: (0,) * len(shape))
    return pl.pallas_call(
        _dec_proj_kernel,
        grid=(1,),
        in_specs=[full((n, d)), full((n, d)), full((n, d)), full((1, d)), full((d, seg))],
        out_specs=full((n, seg)),
        out_shape=jax.ShapeDtypeStruct((n, seg), F32),
        compiler_params=_params(1),
        name="dec_proj",
    )(x, shift, scale, ng, w_seg)


def _dn_dec_kernel(p_ref, cst_ref, s_ref, cw_ref, gp_ref, dng_ref,
                   o_ref, cnew_ref, sn_ref, o_scr):
    bt = p_ref.shape[0]
    qkv = p_ref[:, 0:DN_QKV]
    c0, c1, c2 = cst_ref[:, 0, :], cst_ref[:, 1, :], cst_ref[:, 2, :]
    conv = _silu(c0 * cw_ref[0:1, :] + c1 * cw_ref[1:2, :] + c2 * cw_ref[2:3, :] + qkv * cw_ref[3:4, :])
    cnew_ref[:, 0, :] = c1
    cnew_ref[:, 1, :] = c2
    cnew_ref[:, 2, :] = qkv
    gseg = p_ref[:, 2048:DN_SEG]
    beta_all = _sigmoid(gseg)
    a_all = jnp.exp(-jnp.exp(gp_ref[0:1, :]) * _softplus(gseg + gp_ref[1:2, :]))
    for hh in range(DN_HEADS):
        lo = hh * DN_DK
        q = _l2norm_rows(conv[:, lo:lo + DN_DK]) * (DN_DK ** -0.5)
        k = _l2norm_rows(conv[:, 512 + lo:512 + lo + DN_DK])
        v = conv[:, 1024 + lo:1024 + lo + DN_DV]
        for bi in range(bt):
            k_col = _col_from_row(k[bi:bi + 1, :], DN_DK)
            q_col = _col_from_row(q[bi:bi + 1, :], DN_DK)
            a = a_all[bi:bi + 1, DN_HEADS + hh:DN_HEADS + hh + 1]
            beta = beta_all[bi:bi + 1, hh:hh + 1]
            s_old = s_ref[bi, hh]
            sk = jnp.sum(s_old * k_col, axis=0, keepdims=True)
            v_new = beta * (v[bi:bi + 1, :] - a * sk)
            s_new = s_old * a + k_col * v_new
            sn_ref[bi, hh] = s_new
            o_scr[bi:bi + 1, lo:lo + DN_DV] = jnp.sum(s_new * q_col, axis=0, keepdims=True)
    for hh in range(DN_HEADS):
        lo = hh * DN_DV
        z = p_ref[:, DN_QKV + lo:DN_QKV + lo + DN_DV]
        o_ref[:, lo:lo + DN_DV] = _rms_rows(o_scr[:, lo:lo + DN_DV]) * dng_ref[...] * _silu(z)


def _dn_dec(p, cst, s, cw, gp, dng, bt=8):
    n = p.shape[0]
    return pl.pallas_call(
        _dn_dec_kernel,
        grid=(n // bt,),
        in_specs=[pl.BlockSpec((bt, DN_SEG), lambda i: (i, 0)),
                  pl.BlockSpec((bt, 3, DN_QKV), lambda i: (i, 0, 0)),
                  pl.BlockSpec((bt, DN_HEADS, DN_DK, DN_DV), lambda i: (i, 0, 0, 0)),
                  pl.BlockSpec((4, DN_QKV), lambda i: (0, 0)),
                  pl.BlockSpec((2, LANE), lambda i: (0, 0)),
                  pl.BlockSpec((1, DN_DV), lambda i: (0, 0))],
        out_specs=[pl.BlockSpec((bt, 512), lambda i: (i, 0)),
                   pl.BlockSpec((bt, 3, DN_QKV), lambda i: (i, 0, 0)),
                   pl.BlockSpec((bt, DN_HEADS, DN_DK, DN_DV), lambda i: (i, 0, 0, 0))],
        out_shape=[jax.ShapeDtypeStruct((n, 512), F32),
                   jax.ShapeDtypeStruct((n, 3, DN_QKV), F32),
                   jax.ShapeDtypeStruct((n, DN_HEADS, DN_DK, DN_DV), F32)],
        scratch_shapes=[pltpu.VMEM((bt, 512), F32)],
        compiler_params=_params(1),
        name="dn_dec",
    )(p, cst, s, cw, gp, dng)


def _gla_dec_kernel(p_ref, s_ref, wf_ref, bf_ref, gng_ref, o_ref, sn_ref, o_scr):
    bt = p_ref.shape[0]
    pre = _dot(p_ref[:, 1536:GLA_SEG], wf_ref[...]) + bf_ref[...]
    alpha = jnp.exp((jnp.minimum(pre, 0.0) - jnp.log1p(jnp.exp(-jnp.abs(pre)))) * (1.0 / GLA_TAU))
    q_all = p_ref[:, 0:256] * (GLA_DK ** -0.5)
    k_all = p_ref[:, 256:512]
    for hh in range(GLA_HEADS):
        pair, off = (hh // 2) * LANE, (hh % 2) * GLA_DK
        v = p_ref[:, 512 + hh * GLA_DV:512 + (hh + 1) * GLA_DV]
        for bi in range(bt):
            a_col = _col_from_row(alpha[bi:bi + 1, pair:pair + LANE], GLA_DK, off)
            k_col = _col_from_row(k_all[bi:bi + 1, pair:pair + LANE], GLA_DK, off)
            q_col = _col_from_row(q_all[bi:bi + 1, pair:pair + LANE], GLA_DK, off)
            s_new = s_ref[bi, hh] * a_col + k_col * v[bi:bi + 1, :]
            sn_ref[bi, hh] = s_new
            o_scr[bi:bi + 1, hh * GLA_DV:(hh + 1) * GLA_DV] = jnp.sum(s_new * q_col, axis=0, keepdims=True)
    for hh in range(GLA_HEADS):
        lo = hh * GLA_DV
        z = p_ref[:, 1024 + lo:1024 + lo + GLA_DV]
        o_ref[:, lo:lo + GLA_DV] = _rms_rows(o_scr[:, lo:lo + GLA_DV]) * gng_ref[...] * _silu(z)


def _gla_dec(p, s, wf, bf, gng, bt=8):
    n = p.shape[0]
    return pl.pallas_call(
        _gla_dec_kernel,
        grid=(n // bt,),
        in_specs=[pl.BlockSpec((bt, GLA_SEG), lambda i: (i, 0)),
                  pl.BlockSpec((bt, GLA_HEADS, GLA_DK, GLA_DV), lambda i: (i, 0, 0, 0)),
                  pl.BlockSpec((LANE, 256), lambda i: (0, 0)),
                  pl.BlockSpec((1, 256), lambda i: (0, 0)),
                  pl.BlockSpec((1, GLA_DV), lambda i: (0, 0))],
        out_specs=[pl.BlockSpec((bt, 512), lambda i: (i, 0)),
                   pl.BlockSpec((bt, GLA_HEADS, GLA_DK, GLA_DV), lambda i: (i, 0, 0, 0))],
        out_shape=[jax.ShapeDtypeStruct((n, 512), F32),
                   jax.ShapeDtypeStruct((n, GLA_HEADS, GLA_DK, GLA_DV), F32)],
        scratch_shapes=[pltpu.VMEM((bt, 512), F32)],
        compiler_params=_params(1),
        name="gla_dec",
    )(p, s, wf, bf, gng)


def _rw_dec_kernel(p_ref, sh_ref, s_ref, mu_ref, rp_ref, w2_ref, a2_ref,
                   o_ref, shn_ref, sn_ref, o_scr):
    bt = p_ref.shape[0]
    seg_ones = _seg_ones()
    xs = p_ref[:, 0:RW_SHIFT]
    shn_ref[...] = xs
    xm = xs + (sh_ref[...] - xs) * mu_ref[...]
    r, k, v, kk, b, logw = _rw_features(xm, rp_ref, w2_ref, a2_ref, seg_ones)
    w = jnp.exp(logw)
    for hh in range(RW_HEADS):
        pair, off = (hh // 2) * LANE, (hh % 2) * RW_N
        sl = slice(hh * RW_N, (hh + 1) * RW_N)
        for bi in range(bt):
            row = slice(bi, bi + 1)
            v_col = _col_from_row(v[row, pair:pair + LANE], RW_N, off)
            s_old = s_ref[bi, hh]
            skk = jnp.sum(s_old * kk[row, sl], axis=1, keepdims=True)
            s_new = s_old * w[row, sl] - skk * b[row, sl] + v_col * k[row, sl]
            sn_ref[bi, hh] = s_new
            o_col = jnp.sum(s_new * r[row, sl], axis=1, keepdims=True)
            eye = _iota2((RW_N, RW_N), 0) == _iota2((RW_N, RW_N), 1)
            o_scr[row, sl] = jnp.sum(jnp.where(eye, o_col, 0.0), axis=0, keepdims=True)
    rz = p_ref[:, RW_SHIFT:RW_SEG]
    o_ref[...] = _rw_finish(o_scr[...], r, k, v, rz, rp_ref, seg_ones)


def _rw_dec(p, sh, s, mu, rp, w2, a2, bt=8):
    n = p.shape[0]
    return pl.pallas_call(
        _rw_dec_kernel,
        grid=(n // bt,),
        in_specs=[pl.BlockSpec((bt, RW_SEG), lambda i: (i, 0)),
                  pl.BlockSpec((bt, RW_SHIFT), lambda i: (i, 0)),
                  pl.BlockSpec((bt, RW_HEADS, RW_N, RW_N), lambda i: (i, 0, 0, 0)),
                  pl.BlockSpec((1, RW_SHIFT), lambda i: (0, 0)),
                  pl.BlockSpec((8, RW_C), lambda i: (0, 0)),
                  pl.BlockSpec((LANE, RW_C), lambda i: (0, 0)),
                  pl.BlockSpec((LANE, RW_C), lambda i: (0, 0))],
        out_specs=[pl.BlockSpec((bt, RW_C), lambda i: (i, 0)),
                   pl.BlockSpec((bt, RW_SHIFT), lambda i: (i, 0)),
                   pl.BlockSpec((bt, RW_HEADS, RW_N, RW_N), lambda i: (i, 0, 0, 0))],
        out_shape=[jax.ShapeDtypeStruct((n, RW_C), F32),
                   jax.ShapeDtypeStruct((n, RW_SHIFT), F32),
                   jax.ShapeDtypeStruct((n, RW_HEADS, RW_N, RW_N), F32)],
        scratch_shapes=[pltpu.VMEM((bt, RW_C), F32)],
        compiler_params=_params(1),
        name="rw_dec",
    )(p, sh, s, mu, rp, w2, a2)


def _layer_params(l, norm_g, w_in, dn_conv_w, dn_a_log, dn_dt_bias, dn_norm_g, gla_wf, gla_bf,
                  gla_norm_g, rw_mu, rw_w0, rw_w2, rw_a0, rw_a2, rw_k_k, rw_k_a, rw_r_k,
                  rw_ln_w, rw_ln_b, w_out):
    w = w_in[l]
    pad_cols = lambda a, n: jnp.pad(a, ((0, 0), (0, n - a.shape[1])))
    gp = jnp.zeros((2, LANE), F32)
    gp = gp.at[0, DN_HEADS:2 * DN_HEADS].set(dn_a_log[l]).at[1, DN_HEADS:2 * DN_HEADS].set(dn_dt_bias[l])
    zeros_r = jnp.zeros((1, RW_C), F32)
    rp = jnp.concatenate([rw_w0[l][None], rw_a0[l][None], rw_k_k[l][None], rw_k_a[l][None],
                          rw_r_k[l].reshape(1, RW_C), rw_ln_w[l][None], rw_ln_b[l][None], zeros_r], axis=0)
    return dict(
        ng=norm_g[l][None],
        w_dn=pad_cols(w[:, :DN_PROJ], DN_SEG).astype(BF16),
        w_gla=pad_cols(w[:, DN_PROJ:DN_PROJ + GLA_PROJ], GLA_SEG).astype(BF16),
        w_rw=w[:, DN_PROJ + GLA_PROJ:].astype(BF16),
        cw=dn_conv_w[l], gp=gp, dng=dn_norm_g[l][None],
        wf=jnp.pad(gla_wf[l], ((0, LANE - GLA_RANK), (0, 0))).astype(BF16),
        bf=gla_bf[l][None], gng=gla_norm_g[l][None],
        mu=rw_mu[l][None], rp=rp,
        w2=jnp.pad(rw_w2[l], ((0, LANE - 64), (0, 0))).astype(BF16),
        a2=jnp.pad(rw_a2[l], ((64, 0), (0, 0))).astype(BF16),
        w_out=w_out[l].astype(BF16))


def _gla_state_from_pairs(st):
    heads = []
    for hh in range(GLA_HEADS):
        p, hl = hh // 2, hh % 2
        blk = st[:, p, hl * GLA_DV:(hl + 1) * GLA_DV, hl * GLA_DK:(hl + 1) * GLA_DK]
        heads.append(jnp.swapaxes(blk, 1, 2))
    return jnp.stack(heads, axis=1)


def _rw_state_from_groups(s):
    heads = []
    for hh in range(RW_HEADS):
        g, hl = hh // RW_GROUP, hh % RW_GROUP
        heads.append(s[:, g, hl * RW_N:(hl + 1) * RW_N, hl * RW_N:(hl + 1) * RW_N])
    return jnp.stack(heads, axis=1)


def _prompt_tiles(t):
    tb = 256 if t % 256 == 0 else CHUNK
    tm = 512 if t % 512 == 0 else tb
    return tb, tm


def _prompt_trunk(x, mods, layers, final_g):
    b, t, _ = x.shape
    tb, tm = _prompt_tiles(t)
    states = []
    n_layers = len(layers)
    for l, lp in enumerate(layers):
        mod = mods[l]
        o_dn, conv_new, dn_new = _dn_prompt(x, mod, lp["ng"], lp["w_dn"], lp["cw"], lp["gp"], lp["dng"], tb)
        o_gla, gla_st = _gla_prompt(x, mod, lp["ng"], lp["w_gla"], lp["wf"], lp["bf"], lp["gng"], tb)
        o_rw, rs_new, rw_st = _rw_prompt(x, mod, lp["ng"], lp["w_rw"], lp["mu"], lp["rp"], lp["w2"], lp["a2"], tb)
        x = _outproj(o_dn, o_gla, o_rw, x, mod[:, 2:3, :], lp["w_out"], final_g, l == n_layers - 1, tm)
        states.append((conv_new, dn_new, _gla_state_from_pairs(gla_st), rs_new[:, 0, :],
                       _rw_state_from_groups(rw_st)))
    return x, states


def _sample_trunk(x, mods, layers, final_g, conv_s, dn_s, gla_s, rs_s, rw_s):
    n = x.shape[0]
    x = x.reshape(n, D_MODEL)
    states = []
    n_layers = len(layers)
    for l, lp in enumerate(layers):
        shift, scale, gate = mods[l][:, 0, :], mods[l][:, 1, :], mods[l][:, 2, :]
        p_dn = _dec_proj(x, shift, scale, lp["ng"], lp["w_dn"])
        p_gla = _dec_proj(x, shift, scale, lp["ng"], lp["w_gla"])
        p_rw = _dec_proj(x, shift, scale, lp["ng"], lp["w_rw"])
        o_dn, conv_new, dn_new = _dn_dec(p_dn, conv_s[l], dn_s[l], lp["cw"], lp["gp"], lp["dng"])
        o_gla, gla_new = _gla_dec(p_gla, gla_s[l], lp["wf"], lp["bf"], lp["gng"])
        o_rw, rs_new, rw_new = _rw_dec(p_rw, rs_s[l], rw_s[l], lp["mu"], lp["rp"], lp["w2"], lp["a2"])
        x = _outproj(o_dn[None], o_gla[None], o_rw[None], x[None], gate[None], lp["w_out"], final_g,
                     l == n_layers - 1, n)[0]
        states.append((conv_new, dn_new, gla_new, rs_new, rw_new))
    return x.reshape(n, 1, D_MODEL), states


def kernel(x_prompt, x_sample, c_prompt, c_sample, state_dn_conv, state_dn, state_gla, state_rwkv_shift, state_rwkv, norm_g, ada_w, ada_b, w_in, dn_conv_w, dn_a_log, dn_dt_bias, dn_norm_g, gla_wf, gla_bf, gla_norm_g, rw_mu, rw_w0, rw_w2, rw_a0, rw_a2, rw_k_k, rw_k_a, rw_r_k, rw_ln_w, rw_ln_b, w_out, final_norm_g):
    depth = w_in.shape[0]
    bp = x_prompt.shape[0]
    layers = [_layer_params(l, norm_g, w_in, dn_conv_w, dn_a_log, dn_dt_bias, dn_norm_g, gla_wf, gla_bf,
                            gla_norm_g, rw_mu, rw_w0, rw_w2, rw_a0, rw_a2, rw_k_k, rw_k_a, rw_r_k,
                            rw_ln_w, rw_ln_b, w_out) for l in range(depth)]
    final_g = final_norm_g[None]
    mod_all = _modulation(jnp.concatenate([c_prompt, c_sample], axis=0), ada_w, ada_b)
    mod_all = mod_all.reshape(depth, -1, 3, D_MODEL)
    mods_p = [mod_all[l, :bp] for l in range(depth)]
    mods_s = [mod_all[l, bp:] for l in range(depth)]

    y_p, st_p = _prompt_trunk(x_prompt, mods_p, layers, final_g)
    y_s, st_s = _sample_trunk(x_sample, mods_s, layers, final_g, state_dn_conv, state_dn, state_gla,
                              state_rwkv_shift, state_rwkv)
    stack = lambda sts, i: jnp.stack([s[i] for s in sts])
    return (y_p, y_s,
            stack(st_p, 0), stack(st_p, 1), stack(st_p, 2), stack(st_p, 3), stack(st_p, 4),
            stack(st_s, 0), stack(st_s, 1), stack(st_s, 2), stack(st_s, 3), stack(st_s, 4))
```

```python
import functools

import jax
import jax.numpy as jnp
from jax import lax
from jax.experimental import pallas as pl
from jax.experimental.pallas import tpu as pltpu

F32 = jnp.float32
BF16 = jnp.bfloat16

D_MODEL = 1024
EPS = 1e-6
CHUNK = 64
SUB = 16
NEG = -1e30

DN_HEADS, DN_DK, DN_DV = 4, 128, 128
GLA_HEADS, GLA_DK, GLA_DV, GLA_RANK, GLA_TAU = 4, 64, 128, 16, 16.0
RW_HEADS, RW_N = 8, 64
RW_GROUP = 4
RW_LN_EPS = 64e-5
DN_QKV = 1536
RW_C = 512
RW_SHIFT = 1664
DN_SEG, GLA_SEG, RW_SEG = 2176, 1664, 2176
DN_PROJ, GLA_PROJ = 2056, 1552
LANE = 128
VMEM_LIMIT = 56 * 1024 * 1024


def _dot(a, b):
    return jnp.dot(a.astype(BF16), b.astype(BF16), preferred_element_type=F32)


def _dot_nt(a, b):
    return lax.dot_general(a.astype(BF16), b.astype(BF16), (((1,), (1,)), ((), ())),
                           preferred_element_type=F32)


def _dot_tn(a, b):
    return lax.dot_general(a.astype(BF16), b.astype(BF16), (((0,), (0,)), ((), ())),
                           preferred_element_type=F32)


def _sigmoid(x):
    return jax.nn.sigmoid(x)


def _silu(x):
    return x * jax.nn.sigmoid(x)


def _softplus(x):
    return jnp.maximum(x, 0.0) + jnp.log1p(jnp.exp(-jnp.abs(x)))


def _iota2(shape, dim):
    return lax.broadcasted_iota(jnp.int32, shape, dim)


def _modulated_norm(x, ng, shift, scale):
    ms = jnp.mean(x * x, axis=-1, keepdims=True)
    return x * lax.rsqrt(ms + EPS) * ng * (1.0 + scale) + shift


def _l2norm_rows(x):
    return x * lax.rsqrt(jnp.sum(x * x, axis=-1, keepdims=True) + EPS)


def _rms_rows(x):
    return x * lax.rsqrt(jnp.mean(x * x, axis=-1, keepdims=True) + EPS)


def _seg_sum(x, seg_ones):
    hi = x.astype(BF16)
    lo = (x - hi.astype(F32)).astype(BF16)
    return (jnp.dot(hi, seg_ones, preferred_element_type=F32)
            + jnp.dot(lo, seg_ones, preferred_element_type=F32))


def _split3(x):
    hi = x.astype(BF16)
    rest = x - hi.astype(F32)
    mid = rest.astype(BF16)
    return hi, mid, (rest - mid.astype(F32)).astype(BF16)


def _sum_dot(sel01, x):
    return jnp.dot(jnp.concatenate([sel01] * 3, axis=1), jnp.concatenate(_split3(x), axis=0),
                   preferred_element_type=F32)


def _chunk_tri(n):
    r, c = _iota2((n, n), 0), _iota2((n, n), 1)
    return (((r >> 6) == (c >> 6)) & (r >= c)).astype(BF16)


def _bd_masks(n=CHUNK):
    r, c = _iota2((n, n), 0), _iota2((n, n), 1)
    same = (r >> 6) == (c >> 6)
    return dict(
        causal=same & (r >= c), strict=same & (r > c), eye=(r == c).astype(F32), same=same,
        bd16=(r >> 4) == (c >> 4),
        off32=((r >> 5) == (c >> 5)) & ((r >> 4) != (c >> 4)),
        off64=same & ((r >> 5) != (c >> 5)))


def _expand_rows(a, n_heads, shift):
    head = _iota2(a.shape, 1) >> shift
    zero = jnp.zeros_like(a)
    return jnp.concatenate([jnp.where(head == hh, a, zero) for hh in range(n_heads)], axis=0)


def _sbs_masks(n_heads):
    shape = (CHUNK, n_heads * CHUNK)
    r, c = _iota2(shape, 0), _iota2(shape, 1) & (CHUNK - 1)
    return dict(
        causal=r >= c, strict=r > c, eye=(r == c).astype(F32),
        bd16=(r >> 4) == (c >> 4),
        off32=((r >> 5) == (c >> 5)) & ((r >> 4) != (c >> 4)),
        off64=(r >> 5) != (c >> 5))


def _unit_lower_inverse(Ls, m, n_heads):
    bd = lambda a: _expand_rows(a.astype(BF16), n_heads, 6)
    n = range(len(Ls))
    Ld = [jnp.where(m["bd16"], L, 0.0) for L in Ls]
    X = [m["eye"] - a for a in Ld]
    P = [_dot(a, bd(a)) for a in Ld]
    for _ in range(2):
        t = [_dot(jnp.concatenate([P[i], X[i]], axis=0), bd(P[i])) for i in n]
        P = [a[0:CHUNK] for a in t]
        X = [X[i] + t[i][CHUNK:] for i in n]
    X = [X[i] + _dot(X[i], bd(P[i])) for i in n]
    for off in ("off32", "off64"):
        Y = [_dot(X[i], bd(jnp.where(m[off], Ls[i], 0.0))) for i in n]
        X = [X[i] - _dot(Y[i], bd(X[i])) for i in n]
    return X


def _col_from_row(row, n, lane_off=0):
    w = row.shape[1]
    sel = (_iota2((n, w), 0) + lane_off) == _iota2((n, w), 1)
    return jnp.sum(jnp.where(sel, row, 0.0), axis=1, keepdims=True)


def _params(n_axes):
    return pltpu.CompilerParams(dimension_semantics=("arbitrary",) * n_axes,
                                vmem_limit_bytes=VMEM_LIMIT)


def _mod_kernel(c_ref, w_ref, b_ref, o_ref):
    o_ref[0] = _dot(_silu(c_ref[...]), w_ref[0]) + b_ref[0]


def _modulation(c_all, ada_w, ada_b):
    depth, d, n = ada_w.shape
    rows = c_all.shape[0]
    tn = 1024
    return pl.pallas_call(
        _mod_kernel,
        grid=(depth, n // tn),
        in_specs=[pl.BlockSpec((rows, d), lambda l, j: (0, 0)),
                  pl.BlockSpec((1, d, tn), lambda l, j: (l, 0, j)),
                  pl.BlockSpec((1, 1, tn), lambda l, j: (l, 0, j))],
        out_specs=pl.BlockSpec((1, rows, tn), lambda l, j: (l, 0, j)),
        out_shape=jax.ShapeDtypeStruct((depth, rows, n), F32),
        compiler_params=_params(2),
        name="adaln_mod",
    )(c_all, ada_w, ada_b.reshape(depth, 1, n))


def _dn_prompt_kernel(x_ref, mod_ref, ng_ref, w_ref, cw_ref, gp_ref, dng_ref,
                      o_ref, conv_ref, s_ref,
                      pbuf, qs, ks, vs, gb, gcs, os_, U_s, W_s, Q_s, A_s, K_s, E_s, S):
    t = pl.program_id(1)
    tb = x_ref.shape[1]
    nb = DN_HEADS * CHUNK
    m = _sbs_masks(DN_HEADS)
    lane_c = _iota2((CHUNK, LANE), 1)
    sbs_head = _iota2((CHUNK, nb), 1) >> 6
    gate_lanes = ((_iota2((8, LANE), 1) >= DN_HEADS) & (_iota2((8, LANE), 1) < 2 * DN_HEADS)).astype(BF16)
    heads = range(DN_HEADS)

    def stack(f):
        return jnp.concatenate([f(hh) for hh in heads], axis=0)

    def stack_heads(a):
        return stack(lambda hh: a[:, hh * DN_DK:(hh + 1) * DN_DK])

    def per_head_lanes(col_of):
        return jnp.concatenate([jnp.broadcast_to(col_of(hh), (CHUNK, DN_DK)) for hh in heads], axis=1)

    @pl.when(t == 0)
    def _():
        pbuf[0:8, :] = jnp.zeros((8, DN_SEG), F32)
        S[...] = jnp.zeros_like(S)

    h = _modulated_norm(x_ref[0], ng_ref[...], mod_ref[0, 0:1, :], mod_ref[0, 1:2, :])
    pbuf[8:8 + tb, :] = _dot(h, w_ref[...])

    conv = pbuf[5:5 + tb, 0:DN_QKV] * cw_ref[0:1, :]
    for j in range(1, 4):
        conv = conv + pbuf[5 + j:5 + j + tb, 0:DN_QKV] * cw_ref[j:j + 1, :]
    conv = _silu(conv)
    for hh in range(DN_HEADS):
        lo = hh * DN_DK
        qs[:, lo:lo + DN_DK] = _l2norm_rows(conv[:, lo:lo + DN_DK]) * (DN_DK ** -0.5)
        ks[:, lo:lo + DN_DK] = _l2norm_rows(conv[:, 512 + lo:512 + lo + DN_DK])
    vs[...] = conv[:, 1024:DN_QKV]
    gseg = pbuf[8:8 + tb, 2048:DN_SEG]
    lane = _iota2((tb, LANE), 1)
    gb[...] = jnp.where(lane < DN_HEADS, _sigmoid(gseg),
                        -jnp.exp(gp_ref[0:1, :]) * _softplus(gseg + gp_ref[1:2, :]))

    gcs[...] = _sum_dot(_chunk_tri(tb), gb[...])

    def chunk_matrices(c):
        rows = pl.ds(c * CHUNK, CHUNK)
        gbc = gb[rows, :]
        gc = gcs[rows, :]
        g_of = lambda hh: gc[:, DN_HEADS + hh:DN_HEADS + hh + 1]
        g_col = jnp.zeros((CHUNK, nb), F32)
        for hh in heads:
            g_col = jnp.where(sbs_head == hh, g_of(hh), g_col)
        g_sel = stack(lambda hh: jnp.where(lane_c == DN_HEADS + hh, gc, 0.0))
        g_row = lax.dot_general(jnp.concatenate([gate_lanes] * 3, axis=1),
                                jnp.concatenate(_split3(g_sel), axis=1),
                                (((1,), (1,)), ((), ())), preferred_element_type=F32)[0:1, :]
        decay = jnp.exp(jnp.where(m["causal"], g_col - g_row, NEG))
        k_all = ks[rows, :]
        q_all = qs[rows, :]
        beta_l = per_head_lanes(lambda hh: gbc[:, hh:hh + 1])
        g_l = per_head_lanes(g_of)
        eg_l = jnp.exp(g_l)
        kb_all = k_all * beta_l
        expand = lambda a: _expand_rows(a.astype(BF16), DN_HEADS, 7)
        kq = _dot_nt(jnp.concatenate([kb_all, q_all], axis=0), expand(k_all))
        L = jnp.where(m["strict"], kq[0:CHUNK] * decay, 0.0)
        A_s[c] = _expand_rows((kq[CHUNK:2 * CHUNK] * decay).astype(BF16), DN_HEADS, 6)
        Q_s[c] = stack_heads(q_all * eg_l).astype(BF16)
        K_s[c] = stack_heads(k_all * jnp.exp(g_l[CHUNK - 1:CHUNK, :] - g_l)).astype(BF16)
        E_s[c] = jnp.exp(gc[CHUNK - 8:CHUNK, :])
        return L, expand(vs[rows, :] * beta_l), expand(kb_all * eg_l)

    n_chunks = tb // CHUNK
    pre = [chunk_matrices(c) for c in range(n_chunks)]
    inv = _unit_lower_inverse([p[0] for p in pre], m, DN_HEADS)
    for c in range(n_chunks):
        X = inv[c].astype(BF16)
        U_s[c] = stack_heads(_dot(X, pre[c][1]))
        W_s[c] = stack_heads(_dot(X, pre[c][2])).astype(BF16)

    def recurrence(c, carry):
        rows = pl.ds(pl.multiple_of(c * CHUNK, CHUNK), CHUNK)
        w, qe, kh = W_s[c], Q_s[c], K_s[c]
        e_last = E_s[c][7:8, :]
        ws = []
        for hh in heads:
            sl = slice(hh * CHUNK, (hh + 1) * CHUNK)
            ws.append(jnp.dot(jnp.concatenate([w[sl], qe[sl]], axis=0), S[hh].astype(BF16),
                              preferred_element_type=F32))
        v_new = U_s[c] - jnp.concatenate([x[0:CHUNK] for x in ws], axis=0)
        vb = v_new.astype(BF16)
        o_st = jnp.concatenate([x[CHUNK:2 * CHUNK] for x in ws], axis=0) + jnp.dot(
            A_s[c], vb, preferred_element_type=F32)
        for hh in heads:
            sl = slice(hh * CHUNK, (hh + 1) * CHUNK)
            S[hh] = S[hh] * e_last[:, DN_HEADS + hh:DN_HEADS + hh + 1] + _dot_tn(kh[sl], vb[sl])
            os_[rows, hh * DN_DV:(hh + 1) * DN_DV] = o_st[sl]
        return carry

    lax.fori_loop(0, n_chunks, recurrence, 0)

    for hh in heads:
        lo = hh * DN_DV
        z = pbuf[8:8 + tb, DN_QKV + lo:DN_QKV + lo + DN_DV]
        o_ref[0, :, lo:lo + DN_DV] = (_rms_rows(os_[:, lo:lo + DN_DV]) * dng_ref[...] * _silu(z)).astype(o_ref.dtype)

    pbuf[0:8, :] = pbuf[tb:tb + 8, :]

    @pl.when(t == pl.num_programs(1) - 1)
    def _():
        conv_ref[0] = pbuf[tb + 5:tb + 8, 0:DN_QKV]
        s_ref[0] = S[...]


def _dn_prompt(x, mod, ng, w_seg, cw, gp, dng, tb):
    b, t, d = x.shape
    nch, nb = tb // CHUNK, DN_HEADS * CHUNK
    return pl.pallas_call(
        _dn_prompt_kernel,
        grid=(b, t // tb),
        in_specs=[pl.BlockSpec((1, tb, d), lambda i, j: (i, j, 0)),
                  pl.BlockSpec((1, 3, d), lambda i, j: (i, 0, 0)),
                  pl.BlockSpec((1, d), lambda i, j: (0, 0)),
                  pl.BlockSpec((d, DN_SEG), lambda i, j: (0, 0)),
                  pl.BlockSpec((4, DN_QKV), lambda i, j: (0, 0)),
                  pl.BlockSpec((2, LANE), lambda i, j: (0, 0)),
                  pl.BlockSpec((1, DN_DV), lambda i, j: (0, 0))],
        out_specs=[pl.BlockSpec((1, tb, 512), lambda i, j: (i, j, 0)),
                   pl.BlockSpec((1, 3, DN_QKV), lambda i, j: (i, 0, 0)),
                   pl.BlockSpec((1, DN_HEADS, DN_DK, DN_DV), lambda i, j: (i, 0, 0, 0))],
        out_shape=[jax.ShapeDtypeStruct((b, t, 512), BF16),
                   jax.ShapeDtypeStruct((b, 3, DN_QKV), F32),
                   jax.ShapeDtypeStruct((b, DN_HEADS, DN_DK, DN_DV), F32)],
        scratch_shapes=[pltpu.VMEM((tb + 8, DN_SEG), F32),
                        pltpu.VMEM((tb, 512), F32), pltpu.VMEM((tb, 512), F32),
                        pltpu.VMEM((tb, 512), F32), pltpu.VMEM((tb, LANE), F32),
                        pltpu.VMEM((tb, LANE), F32), pltpu.VMEM((tb, 512), F32),
                        pltpu.VMEM((nch, nb, DN_DV), F32), pltpu.VMEM((nch, nb, DN_DV), BF16),
                        pltpu.VMEM((nch, nb, DN_DK), BF16), pltpu.VMEM((nch, nb, nb), BF16),
                        pltpu.VMEM((nch, nb, DN_DK), BF16), pltpu.VMEM((nch, 8, LANE), F32),
                        pltpu.VMEM((DN_HEADS, DN_DK, DN_DV), F32)],
        compiler_params=_params(2),
        name="dn_prompt",
    )(x, mod, ng, w_seg, cw, gp, dng)


def _gla_prompt_kernel(x_ref, mod_ref, ng_ref, w_ref, wf_ref, bf_ref, gng_ref,
                       o_ref, s_ref,
                       pbuf, lfs, os_, QE_s, KH_s, EG_s, ST):
    t = pl.program_id(1)
    tb = x_ref.shape[1]
    m = _bd_masks()
    n_sub = CHUNK // SUB
    lane = _iota2((CHUNK, LANE), 1)
    head1 = lane >= GLA_DK
    rowblk = _iota2((CHUNK, LANE), 0) >> 4
    rowloc = _iota2((CHUNK, LANE), 0) & (SUB - 1)
    er = _iota2((2 * SUB, SUB * LANE), 0)
    ec = _iota2((2 * SUB, SUB * LANE), 1)
    eblk = (((ec >> 7) == (er & (SUB - 1))) & (((ec & (LANE - 1)) >= GLA_DK) == (er >= SUB))).astype(BF16)
    st_r = _iota2((2 * GLA_DV, LANE), 0)
    st_c = _iota2((2 * GLA_DV, LANE), 1)
    st_mask = (st_r >= GLA_DV) == (st_c >= GLA_DK)

    @pl.when(t == 0)
    def _():
        ST[...] = jnp.zeros_like(ST)

    h = _modulated_norm(x_ref[0], ng_ref[...], mod_ref[0, 0:1, :], mod_ref[0, 1:2, :])
    pbuf[...] = _dot(h, w_ref[...])
    pre = _dot(pbuf[:, 1536:GLA_SEG], wf_ref[...]) + bf_ref[...]
    lfs[...] = (jnp.minimum(pre, 0.0) - jnp.log1p(jnp.exp(-jnp.abs(pre)))) * (1.0 / GLA_TAU)

    lfs[...] = _sum_dot(_chunk_tri(tb), lfs[...])

    def intra_matrices(c, p):
        rows = pl.ds(c * CHUNK, CHUNK)
        lo = p * LANE
        G = lfs[rows, lo:lo + LANE]
        q = pbuf[rows, lo:lo + LANE] * (GLA_DK ** -0.5)
        k = pbuf[rows, 256 + lo:256 + lo + LANE]
        G3 = G.reshape(n_sub, SUB, LANE)
        q3 = q.reshape(n_sub, SUB, LANE)

        def pick(a3, i):
            return jnp.broadcast_to(a3[:, i:i + 1, :], (n_sub, SUB, LANE)).reshape(CHUNK, LANE)

        q_t = q * jnp.exp(G - pick(G3, 0))
        q_parts, k_parts = [], []
        for i in range(1, n_sub):
            q_parts.append(jnp.where(rowblk == i, q_t, 0.0))
            ref_i = G[i * SUB:i * SUB + 1, :]
            k_parts.append(k * jnp.exp(jnp.where(rowblk < i, ref_i - G, NEG)))
        q_big = jnp.concatenate(q_parts, axis=1)
        k_big = jnp.concatenate(k_parts, axis=1)
        head1_big = jnp.concatenate([head1] * (n_sub - 1), axis=1)
        q_both = jnp.concatenate([jnp.where(head1_big, 0.0, q_big),
                                  jnp.where(head1_big, q_big, 0.0)], axis=0)
        a_off = _dot_nt(q_both, k_big)

        slabs = []
        for i in range(SUB):
            dec = jnp.exp(jnp.where(rowloc <= i, pick(G3, i) - G, NEG))
            slabs.append((k * pick(q3, i) * dec).astype(BF16))
        a_diag = lax.dot_general(eblk, jnp.concatenate(slabs, axis=1), (((1,), (1,)), ((), ())),
                                 preferred_element_type=F32)

        g_end = G[CHUNK - 1:CHUNK, :]
        QE_s[c, p] = (q * jnp.exp(G)).astype(BF16)
        KH_s[c, p] = (k * jnp.exp(g_end - G)).astype(BF16)
        EG_s[c, p] = jnp.exp(G[CHUNK - 8:CHUNK, :])
        return a_off, a_diag

    def intra_output(c, p, a_off, a_diag):
        rows = pl.ds(c * CHUNK, CHUNK)
        for hl in range(2):
            hh = 2 * p + hl
            a_d = a_diag[hl * SUB:(hl + 1) * SUB, :]
            a_h = a_off[hl * CHUNK:(hl + 1) * CHUNK, :] + jnp.where(
                m["bd16"], jnp.concatenate([a_d] * n_sub, axis=0), 0.0)
            os_[rows, hh * GLA_DV:(hh + 1) * GLA_DV] = _dot(a_h, pbuf[rows, 512 + hh * GLA_DV:512 + (hh + 1) * GLA_DV])

    n_chunks = tb // CHUNK
    problems = [(c, p) for c in range(n_chunks) for p in range(GLA_HEADS // 2)]
    mats = [intra_matrices(c, p) for c, p in problems]
    for (c, p), ab in zip(problems, mats):
        intra_output(c, p, *ab)

    def recurrence(c, carry):
        rows = pl.ds(pl.multiple_of(c * CHUNK, CHUNK), CHUNK)
        pairs = range(GLA_HEADS // 2)
        st = [ST[p] for p in pairs]
        o_inter = [_dot_nt(QE_s[c, p], st[p]) for p in pairs]
        kv = [_dot_tn(pbuf[rows, 512 + 2 * p * LANE:512 + 2 * (p + 1) * LANE], KH_s[c, p]) for p in pairs]
        for p in pairs:
            ST[p] = st[p] * EG_s[c, p][7:8, :] + jnp.where(st_mask, kv[p], 0.0)
            sl = slice(2 * p * LANE, 2 * (p + 1) * LANE)
            os_[rows, sl] = os_[rows, sl] + o_inter[p]
        return carry

    lax.fori_loop(0, n_chunks, recurrence, 0)

    for hh in range(GLA_HEADS):
        sl = slice(hh * GLA_DV, (hh + 1) * GLA_DV)
        z = pbuf[:, 1024 + hh * GLA_DV:1024 + (hh + 1) * GLA_DV]
        o_ref[0, :, sl] = (_rms_rows(os_[:, sl]) * gng_ref[...] * _silu(z)).astype(o_ref.dtype)

    @pl.when(t == pl.num_programs(1) - 1)
    def _():
        s_ref[0] = ST[...]


def _gla_prompt(x, mod, ng, w_seg, wf, bf, gng, tb):
    b, t, d = x.shape
    return pl.pallas_call(
        _gla_prompt_kernel,
        grid=(b, t // tb),
        in_specs=[pl.BlockSpec((1, tb, d), lambda i, j: (i, j, 0)),
                  pl.BlockSpec((1, 3, d), lambda i, j: (i, 0, 0)),
                  pl.BlockSpec((1, d), lambda i, j: (0, 0)),
                  pl.BlockSpec((d, GLA_SEG), lambda i, j: (0, 0)),
                  pl.BlockSpec((LANE, 256), lambda i, j: (0, 0)),
                  pl.BlockSpec((1, 256), lambda i, j: (0, 0)),
                  pl.BlockSpec((1, GLA_DV), lambda i, j: (0, 0))],
        out_specs=[pl.BlockSpec((1, tb, 512), lambda i, j: (i, j, 0)),
                   pl.BlockSpec((1, 2, 2 * GLA_DV, LANE), lambda i, j: (i, 0, 0, 0))],
        out_shape=[jax.ShapeDtypeStruct((b, t, 512), BF16),
                   jax.ShapeDtypeStruct((b, 2, 2 * GLA_DV, LANE), F32)],
        scratch_shapes=[pltpu.VMEM((tb, GLA_SEG), F32), pltpu.VMEM((tb, 256), F32),
                        pltpu.VMEM((tb, 512), F32),
                        pltpu.VMEM((tb // CHUNK, 2, CHUNK, LANE), BF16),
                        pltpu.VMEM((tb // CHUNK, 2, CHUNK, LANE), BF16),
                        pltpu.VMEM((tb // CHUNK, 2, 8, LANE), F32),
                        pltpu.VMEM((2, 2 * GLA_DV, LANE), F32)],
        compiler_params=_params(2),
        name="gla_prompt",
    )(x, mod, ng, w_seg, wf, bf, gng)


def _rw_features(xm, rp_ref, w2_ref, a2_ref, seg_ones):
    r = xm[:, 0:RW_C]
    k = xm[:, RW_C:2 * RW_C]
    v = xm[:, 2 * RW_C:3 * RW_C]
    lo = xm[:, 3 * RW_C:RW_SHIFT]
    w_log = -_softplus(-(rp_ref[0:1, :] + _dot(jnp.tanh(lo), w2_ref[...]))) - 0.5
    logw = -jnp.exp(w_log)
    a = _sigmoid(rp_ref[1:2, :] + _dot(lo, a2_ref[...]))
    kk = k * rp_ref[2:3, :]
    kk = kk * lax.rsqrt(_seg_sum(kk * kk, seg_ones) + EPS)
    k = k * (1.0 + (a - 1.0) * rp_ref[3:4, :])
    return r, k, v, kk, kk * a, logw


def _rw_finish(o, r, k, v, rz, rp_ref, seg_ones):
    mu = _seg_sum(o, seg_ones) * (1.0 / RW_N)
    dev = o - mu
    var = _seg_sum(dev * dev, seg_ones) * (1.0 / RW_N)
    on = dev * lax.rsqrt(var + RW_LN_EPS) * rp_ref[5:6, :] + rp_ref[6:7, :]
    bonus = _seg_sum(r * k * rp_ref[4:5, :], seg_ones) * v
    return (on + bonus) * _silu(rz)


def _seg_ones():
    return ((_iota2((RW_C, RW_C), 0) >> 6) == (_iota2((RW_C, RW_C), 1) >> 6)).astype(BF16)


def _rw_prompt_kernel(x_ref, mod_ref, ng_ref, w_ref, mu_ref, rp_ref, w2_ref, a2_ref,
                      o_ref, sh_ref, s_ref,
                      pbuf, rs, ks, vs, kks, bs, lws, gs, os_,
                      P1_s, KT_s, P2_s, RH_s, AB_s, KB_s, E_s, S):
    t = pl.program_id(1)
    tb = x_ref.shape[1]
    gw = RW_GROUP * RW_N
    m = _sbs_masks(RW_GROUP)
    same_head = (_iota2((gw, gw), 0) >> 6) == (_iota2((gw, gw), 1) >> 6)
    seg_ones = _seg_ones()
    mid = CHUNK // 2

    def expand(a):
        return _expand_rows(a.astype(BF16), RW_GROUP, 6)

    @pl.when(t == 0)
    def _():
        pbuf[0:8, :] = jnp.zeros((8, RW_SEG), F32)
        S[...] = jnp.zeros_like(S)

    h = _modulated_norm(x_ref[0], ng_ref[...], mod_ref[0, 0:1, :], mod_ref[0, 1:2, :])
    pbuf[8:8 + tb, :] = _dot(h, w_ref[...])
    xs = pbuf[8:8 + tb, 0:RW_SHIFT]
    prev = pbuf[7:7 + tb, 0:RW_SHIFT]
    xm = xs + (prev - xs) * mu_ref[...]
    r, k, v, kk, b, logw = _rw_features(xm, rp_ref, w2_ref, a2_ref, seg_ones)
    rs[...] = r
    ks[...] = k
    vs[...] = v
    kks[...] = kk
    bs[...] = b
    lws[...] = logw
    gs[...] = _sum_dot(_chunk_tri(tb), logw)

    def chunk_matrices(c, g):
        rows = pl.ds(c * CHUNK, CHUNK)
        sl = slice(g * gw, (g + 1) * gw)
        G = gs[rows, sl]
        Gx = G - lws[rows, sl]
        ref = G[mid:mid + 1, :]
        r_, k_, v_, kk_, b_ = rs[rows, sl], ks[rows, sl], vs[rows, sl], kks[rows, sl], bs[rows, sl]
        e_out = jnp.exp(ref - G)
        lhs = jnp.concatenate([kk_ * jnp.exp(Gx - ref), r_ * jnp.exp(G - ref)], axis=0)
        rhs = jnp.concatenate([expand(b_ * e_out), expand(k_ * e_out)], axis=0)
        mm = _dot_nt(lhs, rhs)
        Lb = jnp.where(m["strict"], mm[0:CHUNK, 0:gw], 0.0)
        Mk = jnp.where(m["strict"], mm[0:CHUNK, gw:], 0.0)
        Arb = jnp.where(m["causal"], mm[CHUNK:, 0:gw], 0.0)
        Ark = jnp.where(m["causal"], mm[CHUNK:, gw:], 0.0)
        kv = _dot(jnp.concatenate([Mk, Ark], axis=0), expand(v_))
        P2_s[c, g] = kv[CHUNK:]
        RH_s[c, g] = (r_ * jnp.exp(G)).astype(BF16)
        AB_s[c, g] = Arb.astype(BF16)
        e_end = jnp.exp(G[CHUNK - 1:CHUNK, :] - G)
        KB_s[c, g] = jnp.concatenate([k_ * e_end, -(b_ * e_end)], axis=0).astype(BF16)
        E_s[c, g] = jnp.exp(G[CHUNK - 8:CHUNK, :])
        return Lb, jnp.concatenate([expand(kv[0:CHUNK]), expand(kk_ * jnp.exp(Gx))], axis=1)

    n_chunks = tb // CHUNK
    problems = [(c, g) for c in range(n_chunks) for g in range(RW_HEADS // RW_GROUP)]
    pre = [chunk_matrices(c, g) for c, g in problems]
    inv = _unit_lower_inverse([p[0] for p in pre], m, RW_GROUP)
    for (c, g), X, p in zip(problems, inv, pre):
        sol = _dot(X, p[1])
        P1_s[c, g] = sol[:, 0:gw]
        KT_s[c, g] = sol[:, gw:].astype(BF16)

    def recurrence(c, carry):
        rows = pl.ds(pl.multiple_of(c * CHUNK, CHUNK), CHUNK)
        groups = range(RW_HEADS // RW_GROUP)
        sls = [slice(g * gw, (g + 1) * gw) for g in groups]
        s_old = [S[g] for g in groups]
        khu = [_dot_nt(jnp.concatenate([KT_s[c, g], RH_s[c, g]], axis=0), s_old[g]) for g in groups]
        u_ = [P1_s[c, g] + khu[g][0:CHUNK] for g in groups]
        au = [_dot(AB_s[c, g], expand(u_[g])) for g in groups]
        upd = [_dot_tn(jnp.concatenate([vs[rows, sls[g]], u_[g]], axis=0), KB_s[c, g]) for g in groups]
        for g in groups:
            S[g] = s_old[g] * E_s[c, g][7:8, :] + jnp.where(same_head, upd[g], 0.0)
            os_[rows, sls[g]] = khu[g][CHUNK:2 * CHUNK] + P2_s[c, g] - au[g]
        return carry

    lax.fori_loop(0, n_chunks, recurrence, 0)

    rz = pbuf[8:8 + tb, RW_SHIFT:RW_SEG]
    o_ref[0] = _rw_finish(os_[...], rs[...], ks[...], vs[...], rz, rp_ref, seg_ones).astype(o_ref.dtype)
    pbuf[0:8, :] = pbuf[tb:tb + 8, :]

    @pl.when(t == pl.num_programs(1) - 1)
    def _():
        sh_ref[0] = pbuf[tb + 7:tb + 8, 0:RW_SHIFT]
        s_ref[0] = S[...]


def _rw_prompt(x, mod, ng, w_seg, mu, rp, w2, a2, tb):
    b, t, d = x.shape
    nch, ng_, gw = tb // CHUNK, RW_HEADS // RW_GROUP, RW_GROUP * RW_N
    return pl.pallas_call(
        _rw_prompt_kernel,
        grid=(b, t // tb),
        in_specs=[pl.BlockSpec((1, tb, d), lambda i, j: (i, j, 0)),
                  pl.BlockSpec((1, 3, d), lambda i, j: (i, 0, 0)),
                  pl.BlockSpec((1, d), lambda i, j: (0, 0)),
                  pl.BlockSpec((d, RW_SEG), lambda i, j: (0, 0)),
                  pl.BlockSpec((1, RW_SHIFT), lambda i, j: (0, 0)),
                  pl.BlockSpec((8, RW_C), lambda i, j: (0, 0)),
                  pl.BlockSpec((LANE, RW_C), lambda i, j: (0, 0)),
                  pl.BlockSpec((LANE, RW_C), lambda i, j: (0, 0))],
        out_specs=[pl.BlockSpec((1, tb, RW_C), lambda i, j: (i, j, 0)),
                   pl.BlockSpec((1, 1, RW_SHIFT), lambda i, j: (i, 0, 0)),
                   pl.BlockSpec((1, ng_, gw, gw), lambda i, j: (i, 0, 0, 0))],
        out_shape=[jax.ShapeDtypeStruct((b, t, RW_C), BF16),
                   jax.ShapeDtypeStruct((b, 1, RW_SHIFT), F32),
                   jax.ShapeDtypeStruct((b, ng_, gw, gw), F32)],
        scratch_shapes=[pltpu.VMEM((tb + 8, RW_SEG), F32)]
        + [pltpu.VMEM((tb, RW_C), F32) for _ in range(8)]
        + [pltpu.VMEM((nch, ng_, CHUNK, gw), F32), pltpu.VMEM((nch, ng_, CHUNK, gw), BF16),
           pltpu.VMEM((nch, ng_, CHUNK, gw), F32), pltpu.VMEM((nch, ng_, CHUNK, gw), BF16),
           pltpu.VMEM((nch, ng_, CHUNK, gw), BF16), pltpu.VMEM((nch, ng_, 2 * CHUNK, gw), BF16),
           pltpu.VMEM((nch, ng_, 8, gw), F32),
           pltpu.VMEM((ng_, gw, gw), F32)],
        compiler_params=_params(2),
        name="rw_prompt",
    )(x, mod, ng, w_seg, mu, rp, w2, a2)


def _outproj_kernel(od_ref, og_ref, or_ref, x_ref, gate_ref, w_ref, fg_ref, y_ref, *, final):
    acc = (_dot(od_ref[0], w_ref[0:512, :]) + _dot(og_ref[0], w_ref[512:1024, :])
           + _dot(or_ref[0], w_ref[1024:1536, :]))
    xn = x_ref[0] + gate_ref[0] * acc
    if final:
        xn = _rms_rows(xn) * fg_ref[...]
    y_ref[0] = xn


def _outproj(o_dn, o_gla, o_rw, x, gate, w_out, fg, final, tm):
    b, t, d = x.shape
    tg = gate.shape[1]
    gspec = (pl.BlockSpec((1, 1, d), lambda i, j: (i, 0, 0)) if tg == 1
             else pl.BlockSpec((1, tm, d), lambda i, j: (i, j, 0)))
    ospec = pl.BlockSpec((1, tm, 512), lambda i, j: (i, j, 0))
    return pl.pallas_call(
        functools.partial(_outproj_kernel, final=final),
        grid=(b, t // tm),
        in_specs=[ospec, ospec, ospec,
                  pl.BlockSpec((1, tm, d), lambda i, j: (i, j, 0)),
                  gspec,
                  pl.BlockSpec((1536, d), lambda i, j: (0, 0)),
                  pl.BlockSpec((1, d), lambda i, j: (0, 0))],
        out_specs=pl.BlockSpec((1, tm, d), lambda i, j: (i, j, 0)),
        out_shape=jax.ShapeDtypeStruct((b, t, d), F32),
        compiler_params=_params(2),
        name="out_proj",
    )(o_dn, o_gla, o_rw, x, gate, w_out, fg)


def _dec_proj_kernel(x_ref, shift_ref, scale_ref, ng_ref, w_ref, o_ref):
    h = _modulated_norm(x_ref[...], ng_ref[...], shift_ref[...], scale_ref[...])
    o_ref[...] = _dot(h, w_ref[...])


def _dec_proj(x, shift, scale, ng, w_seg):
    n, d = x.shape
    seg = w_seg.shape[1]
    full = lambda shape: pl.BlockSpec(shape, lambda i: (0,) * len(shape))
    return pl.pallas_call(
        _dec_proj_kernel,
        grid=(1,),
        in_specs=[full((n, d)), full((n, d)), full((n, d)), full((1, d)), full((d, seg))],
        out_specs=full((n, seg)),
        out_shape=jax.ShapeDtypeStruct((n, seg), F32),
        compiler_params=_params(1),
        name="dec_proj",
    )(x, shift, scale, ng, w_seg)


def _dn_dec_kernel(p_ref, cst_ref, s_ref, cw_ref, gp_ref, dng_ref,
                   o_ref, cnew_ref, sn_ref, o_scr):
    bt = p_ref.shape[0]
    qkv = p_ref[:, 0:DN_QKV]
    c0, c1, c2 = cst_ref[:, 0, :], cst_ref[:, 1, :], cst_ref[:, 2, :]
    conv = _silu(c0 * cw_ref[0:1, :] + c1 * cw_ref[1:2, :] + c2 * cw_ref[2:3, :] + qkv * cw_ref[3:4, :])
    cnew_ref[:, 0, :] = c1
    cnew_ref[:, 1, :] = c2
    cnew_ref[:, 2, :] = qkv
    gseg = p_ref[:, 2048:DN_SEG]
    beta_all = _sigmoid(gseg)
    a_all = jnp.exp(-jnp.exp(gp_ref[0:1, :]) * _softplus(gseg + gp_ref[1:2, :]))
    for hh in range(DN_HEADS):
        lo = hh * DN_DK
        q = _l2norm_rows(conv[:, lo:lo + DN_DK]) * (DN_DK ** -0.5)
        k = _l2norm_rows(conv[:, 512 + lo:512 + lo + DN_DK])
        v = conv[:, 1024 + lo:1024 + lo + DN_DV]
        for bi in range(bt):
            k_col = _col_from_row(k[bi:bi + 1, :], DN_DK)
            q_col = _col_from_row(q[bi:bi + 1, :], DN_DK)
            a = a_all[bi:bi + 1, DN_HEADS + hh:DN_HEADS + hh + 1]
            beta = beta_all[bi:bi + 1, hh:hh + 1]
            s_old = s_ref[bi, hh]
            sk = jnp.sum(s_old * k_col, axis=0, keepdims=True)
            v_new = beta * (v[bi:bi + 1, :] - a * sk)
            s_new = s_old * a + k_col * v_new
            sn_ref[bi, hh] = s_new
            o_scr[bi:bi + 1, lo:lo + DN_DV] = jnp.sum(s_new * q_col, axis=0, keepdims=True)
    for hh in range(DN_HEADS):
        lo = hh * DN_DV
        z = p_ref[:, DN_QKV + lo:DN_QKV + lo + DN_DV]
        o_ref[:, lo:lo + DN_DV] = _rms_rows(o_scr[:, lo:lo + DN_DV]) * dng_ref[...] * _silu(z)


def _dn_dec(p, cst, s, cw, gp, dng, bt=8):
    n = p.shape[0]
    return pl.pallas_call(
        _dn_dec_kernel,
        grid=(n // bt,),
        in_specs=[pl.BlockSpec((bt, DN_SEG), lambda i: (i, 0)),
                  pl.BlockSpec((bt, 3, DN_QKV), lambda i: (i, 0, 0)),
                  pl.BlockSpec((bt, DN_HEADS, DN_DK, DN_DV), lambda i: (i, 0, 0, 0)),
                  pl.BlockSpec((4, DN_QKV), lambda i: (0, 0)),
                  pl.BlockSpec((2, LANE), lambda i: (0, 0)),
                  pl.BlockSpec((1, DN_DV), lambda i: (0, 0))],
        out_specs=[pl.BlockSpec((bt, 512), lambda i: (i, 0)),
                   pl.BlockSpec((bt, 3, DN_QKV), lambda i: (i, 0, 0)),
                   pl.BlockSpec((bt, DN_HEADS, DN_DK, DN_DV), lambda i: (i, 0, 0, 0))],
        out_shape=[jax.ShapeDtypeStruct((n, 512), F32),
                   jax.ShapeDtypeStruct((n, 3, DN_QKV), F32),
                   jax.ShapeDtypeStruct((n, DN_HEADS, DN_DK, DN_DV), F32)],
        scratch_shapes=[pltpu.VMEM((bt, 512), F32)],
        compiler_params=_params(1),
        name="dn_dec",
    )(p, cst, s, cw, gp, dng)


def _gla_dec_kernel(p_ref, s_ref, wf_ref, bf_ref, gng_ref, o_ref, sn_ref, o_scr):
    bt = p_ref.shape[0]
    pre = _dot(p_ref[:, 1536:GLA_SEG], wf_ref[...]) + bf_ref[...]
    alpha = jnp.exp((jnp.minimum(pre, 0.0) - jnp.log1p(jnp.exp(-jnp.abs(pre)))) * (1.0 / GLA_TAU))
    q_all = p_ref[:, 0:256] * (GLA_DK ** -0.5)
    k_all = p_ref[:, 256:512]
    for hh in range(GLA_HEADS):
        pair, off = (hh // 2) * LANE, (hh % 2) * GLA_DK
        v = p_ref[:, 512 + hh * GLA_DV:512 + (hh + 1) * GLA_DV]
        for bi in range(bt):
            a_col = _col_from_row(alpha[bi:bi + 1, pair:pair + LANE], GLA_DK, off)
            k_col = _col_from_row(k_all[bi:bi + 1, pair:pair + LANE], GLA_DK, off)
            q_col = _col_from_row(q_all[bi:bi + 1, pair:pair + LANE], GLA_DK, off)
            s_new = s_ref[bi, hh] * a_col + k_col * v[bi:bi + 1, :]
            sn_ref[bi, hh] = s_new
            o_scr[bi:bi + 1, hh * GLA_DV:(hh + 1) * GLA_DV] = jnp.sum(s_new * q_col, axis=0, keepdims=True)
    for hh in range(GLA_HEADS):
        lo = hh * GLA_DV
        z = p_ref[:, 1024 + lo:1024 + lo + GLA_DV]
        o_ref[:, lo:lo + GLA_DV] = _rms_rows(o_scr[:, lo:lo + GLA_DV]) * gng_ref[...] * _silu(z)


def _gla_dec(p, s, wf, bf, gng, bt=8):
    n = p.shape[0]
    return pl.pallas_call(
        _gla_dec_kernel,
        grid=(n // bt,),
        in_specs=[pl.BlockSpec((bt, GLA_SEG), lambda i: (i, 0)),
                  pl.BlockSpec((bt, GLA_HEADS, GLA_DK, GLA_DV), lambda i: (i, 0, 0, 0)),
                  pl.BlockSpec((LANE, 256), lambda i: (0, 0)),
                  pl.BlockSpec((1, 256), lambda i: (0, 0)),
                  pl.BlockSpec((1, GLA_DV), lambda i: (0, 0))],
        out_specs=[pl.BlockSpec((bt, 512), lambda i: (i, 0)),
                   pl.BlockSpec((bt, GLA_HEADS, GLA_DK, GLA_DV), lambda i: (i, 0, 0, 0))],
        out_shape=[jax.ShapeDtypeStruct((n, 512), F32),
                   jax.ShapeDtypeStruct((n, GLA_HEADS, GLA_DK, GLA_DV), F32)],
        scratch_shapes=[pltpu.VMEM((bt, 512), F32)],
        compiler_params=_params(1),
        name="gla_dec",
    )(p, s, wf, bf, gng)


def _rw_dec_kernel(p_ref, sh_ref, s_ref, mu_ref, rp_ref, w2_ref, a2_ref,
                   o_ref, shn_ref, sn_ref, o_scr):
    bt = p_ref.shape[0]
    seg_ones = _seg_ones()
    xs = p_ref[:, 0:RW_SHIFT]
    shn_ref[...] = xs
    xm = xs + (sh_ref[...] - xs) * mu_ref[...]
    r, k, v, kk, b, logw = _rw_features(xm, rp_ref, w2_ref, a2_ref, seg_ones)
    w = jnp.exp(logw)
    ones = jnp.ones((RW_N, RW_N), BF16)
    eye = _iota2((RW_N, RW_N), 0) == _iota2((RW_N, RW_N), 1)
    seqs = range(bt)

    def row_sums(a):
        hi = a.astype(BF16)
        lo = (a - hi.astype(F32)).astype(BF16)
        return (jnp.dot(hi, ones, preferred_element_type=F32) + jnp.dot(lo, ones, preferred_element_type=F32))

    for hh in range(RW_HEADS):
        sl = slice(hh * RW_N, (hh + 1) * RW_N)
        s_old = [s_ref[bi, hh] for bi in seqs]
        red = row_sums(jnp.concatenate([s_old[bi] * kk[bi:bi + 1, sl] for bi in seqs]
                                       + [jnp.where(eye, v[bi:bi + 1, sl], 0.0) for bi in seqs], axis=0))
        s_new = []
        for bi in seqs:
            row = slice(bi, bi + 1)
            skk = red[bi * RW_N:(bi + 1) * RW_N]
            v_col = red[(bt + bi) * RW_N:(bt + bi + 1) * RW_N]
            s_new.append(s_old[bi] * w[row, sl] - skk * b[row, sl] + v_col * k[row, sl])
            sn_ref[bi, hh] = s_new[bi]
        red = row_sums(jnp.concatenate([s_new[bi] * r[bi:bi + 1, sl] for bi in seqs], axis=0))
        for bi in seqs:
            o_scr[bi:bi + 1, sl] = jnp.sum(jnp.where(eye, red[bi * RW_N:(bi + 1) * RW_N], 0.0),
                                           axis=0, keepdims=True)
    rz = p_ref[:, RW_SHIFT:RW_SEG]
    o_ref[...] = _rw_finish(o_scr[...], r, k, v, rz, rp_ref, seg_ones)


def _rw_dec(p, sh, s, mu, rp, w2, a2, bt=8):
    n = p.shape[0]
    return pl.pallas_call(
        _rw_dec_kernel,
        grid=(n // bt,),
        in_specs=[pl.BlockSpec((bt, RW_SEG), lambda i: (i, 0)),
                  pl.BlockSpec((bt, RW_SHIFT), lambda i: (i, 0)),
                  pl.BlockSpec((bt, RW_HEADS, RW_N, RW_N), lambda i: (i, 0, 0, 0)),
                  pl.BlockSpec((1, RW_SHIFT), lambda i: (0, 0)),
                  pl.BlockSpec((8, RW_C), lambda i: (0, 0)),
                  pl.BlockSpec((LANE, RW_C), lambda i: (0, 0)),
                  pl.BlockSpec((LANE, RW_C), lambda i: (0, 0))],
        out_specs=[pl.BlockSpec((bt, RW_C), lambda i: (i, 0)),
                   pl.BlockSpec((bt, RW_SHIFT), lambda i: (i, 0)),
                   pl.BlockSpec((bt, RW_HEADS, RW_N, RW_N), lambda i: (i, 0, 0, 0))],
        out_shape=[jax.ShapeDtypeStruct((n, RW_C), F32),
                   jax.ShapeDtypeStruct((n, RW_SHIFT), F32),
                   jax.ShapeDtypeStruct((n, RW_HEADS, RW_N, RW_N), F32)],
        scratch_shapes=[pltpu.VMEM((bt, RW_C), F32)],
        compiler_params=_params(1),
        name="rw_dec",
    )(p, sh, s, mu, rp, w2, a2)


def _layer_params(l, norm_g, w_in, dn_conv_w, dn_a_log, dn_dt_bias, dn_norm_g, gla_wf, gla_bf,
                  gla_norm_g, rw_mu, rw_w0, rw_w2, rw_a0, rw_a2, rw_k_k, rw_k_a, rw_r_k,
                  rw_ln_w, rw_ln_b, w_out):
    w = w_in[l]
    pad_cols = lambda a, n: jnp.pad(a, ((0, 0), (0, n - a.shape[1])))
    gp = jnp.zeros((2, LANE), F32)
    gp = gp.at[0, DN_HEADS:2 * DN_HEADS].set(dn_a_log[l]).at[1, DN_HEADS:2 * DN_HEADS].set(dn_dt_bias[l])
    zeros_r = jnp.zeros((1, RW_C), F32)
    rp = jnp.concatenate([rw_w0[l][None], rw_a0[l][None], rw_k_k[l][None], rw_k_a[l][None],
                          rw_r_k[l].reshape(1, RW_C), rw_ln_w[l][None], rw_ln_b[l][None], zeros_r], axis=0)
    return dict(
        ng=norm_g[l][None],
        w_dn=pad_cols(w[:, :DN_PROJ], DN_SEG).astype(BF16),
        w_gla=pad_cols(w[:, DN_PROJ:DN_PROJ + GLA_PROJ], GLA_SEG).astype(BF16),
        w_rw=w[:, DN_PROJ + GLA_PROJ:].astype(BF16),
        cw=dn_conv_w[l], gp=gp, dng=dn_norm_g[l][None],
        wf=jnp.pad(gla_wf[l], ((0, LANE - GLA_RANK), (0, 0))).astype(BF16),
        bf=gla_bf[l][None], gng=gla_norm_g[l][None],
        mu=rw_mu[l][None], rp=rp,
        w2=jnp.pad(rw_w2[l], ((0, LANE - 64), (0, 0))).astype(BF16),
        a2=jnp.pad(rw_a2[l], ((64, 0), (0, 0))).astype(BF16),
        w_out=w_out[l].astype(BF16))


def _gla_state_from_pairs(st):
    heads = []
    for hh in range(GLA_HEADS):
        p, hl = hh // 2, hh % 2
        blk = st[:, p, hl * GLA_DV:(hl + 1) * GLA_DV, hl * GLA_DK:(hl + 1) * GLA_DK]
        heads.append(jnp.swapaxes(blk, 1, 2))
    return jnp.stack(heads, axis=1)


def _rw_state_from_groups(s):
    heads = []
    for hh in range(RW_HEADS):
        g, hl = hh // RW_GROUP, hh % RW_GROUP
        heads.append(s[:, g, hl * RW_N:(hl + 1) * RW_N, hl * RW_N:(hl + 1) * RW_N])
    return jnp.stack(heads, axis=1)


def _prompt_tiles(t):
    tb = 256 if t % 256 == 0 else CHUNK
    tm = 512 if t % 512 == 0 else tb
    return tb, tm


def _prompt_trunk(x, mods, layers, final_g):
    b, t, _ = x.shape
    tb, tm = _prompt_tiles(t)
    states = []
    n_layers = len(layers)
    for l, lp in enumerate(layers):
        mod = mods[l]
        o_dn, conv_new, dn_new = _dn_prompt(x, mod, lp["ng"], lp["w_dn"], lp["cw"], lp["gp"], lp["dng"], tb)
        o_gla, gla_st = _gla_prompt(x, mod, lp["ng"], lp["w_gla"], lp["wf"], lp["bf"], lp["gng"], tb)
        o_rw, rs_new, rw_st = _rw_prompt(x, mod, lp["ng"], lp["w_rw"], lp["mu"], lp["rp"], lp["w2"], lp["a2"], tb)
        x = _outproj(o_dn, o_gla, o_rw, x, mod[:, 2:3, :], lp["w_out"], final_g, l == n_layers - 1, tm)
        states.append((conv_new, dn_new, _gla_state_from_pairs(gla_st), rs_new[:, 0, :],
                       _rw_state_from_groups(rw_st)))
    return x, states


def _sample_trunk(x, mods, layers, final_g, conv_s, dn_s, gla_s, rs_s, rw_s):
    n = x.shape[0]
    x = x.reshape(n, D_MODEL)
    states = []
    n_layers = len(layers)
    for l, lp in enumerate(layers):
        shift, scale, gate = mods[l][:, 0, :], mods[l][:, 1, :], mods[l][:, 2, :]
        p_dn = _dec_proj(x, shift, scale, lp["ng"], lp["w_dn"])
        p_gla = _dec_proj(x, shift, scale, lp["ng"], lp["w_gla"])
        p_rw = _dec_proj(x, shift, scale, lp["ng"], lp["w_rw"])
        o_dn, conv_new, dn_new = _dn_dec(p_dn, conv_s[l], dn_s[l], lp["cw"], lp["gp"], lp["dng"])
        o_gla, gla_new = _gla_dec(p_gla, gla_s[l], lp["wf"], lp["bf"], lp["gng"])
        o_rw, rs_new, rw_new = _rw_dec(p_rw, rs_s[l], rw_s[l], lp["mu"], lp["rp"], lp["w2"], lp["a2"])
        x = _outproj(o_dn[None], o_gla[None], o_rw[None], x[None], gate[None], lp["w_out"], final_g,
                     l == n_layers - 1, n)[0]
        states.append((conv_new, dn_new, gla_new, rs_new, rw_new))
    return x.reshape(n, 1, D_MODEL), states


def kernel(x_prompt, x_sample, c_prompt, c_sample, state_dn_conv, state_dn, state_gla, state_rwkv_shift, state_rwkv, norm_g, ada_w, ada_b, w_in, dn_conv_w, dn_a_log, dn_dt_bias, dn_norm_g, gla_wf, gla_bf, gla_norm_g, rw_mu, rw_w0, rw_w2, rw_a0, rw_a2, rw_k_k, rw_k_a, rw_r_k, rw_ln_w, rw_ln_b, w_out, final_norm_g):
    depth = w_in.shape[0]
    bp = x_prompt.shape[0]
    layers = [_layer_params(l, norm_g, w_in, dn_conv_w, dn_a_log, dn_dt_bias, dn_norm_g, gla_wf, gla_bf,
                            gla_norm_g, rw_mu, rw_w0, rw_w2, rw_a0, rw_a2, rw_k_k, rw_k_a, rw_r_k,
                            rw_ln_w, rw_ln_b, w_out) for l in range(depth)]
    final_g = final_norm_g[None]
    mod_all = _modulation(jnp.concatenate([c_prompt, c_sample], axis=0), ada_w, ada_b)
    mod_all = mod_all.reshape(depth, -1, 3, D_MODEL)
    mods_p = [mod_all[l, :bp] for l in range(depth)]
    mods_s = [mod_all[l, bp:] for l in range(depth)]

    y_p, st_p = _prompt_trunk(x_prompt, mods_p, layers, final_g)
    y_s, st_s = _sample_trunk(x_sample, mods_s, layers, final_g, state_dn_conv, state_dn, state_gla,
                              state_rwkv_shift, state_rwkv)
    stack = lambda sts, i: jnp.stack([s[i] for s in sts])
    return (y_p, y_s,
            stack(st_p, 0), stack(st_p, 1), stack(st_p, 2), stack(st_p, 3), stack(st_p, 4),
            stack(st_s, 0), stack(st_s, 1), stack(st_s, 2), stack(st_s, 3), stack(st_s, 4))
```

```python
import functools

import jax
import jax.numpy as jnp
from jax import lax
from jax.experimental import pallas as pl
from jax.experimental.pallas import tpu as pltpu

F32 = jnp.float32
BF16 = jnp.bfloat16

D_MODEL = 1024
EPS = 1e-6
CHUNK = 64
SUB = 16
NEG = -1e30

DN_HEADS, DN_DK, DN_DV = 4, 128, 128
GLA_HEADS, GLA_DK, GLA_DV, GLA_RANK, GLA_TAU = 4, 64, 128, 16, 16.0
RW_HEADS, RW_N = 8, 64
RW_GROUP = 4
RW_LN_EPS = 64e-5
DN_QKV = 1536
RW_C = 512
RW_SHIFT = 1664
DN_SEG, GLA_SEG, RW_SEG = 2176, 1664, 2176
DN_PROJ, GLA_PROJ, D_PROJ = 2056, 1552, 5784
GLA_WIN0, RW_WIN0 = (DN_PROJ // 128) * 128, ((DN_PROJ + GLA_PROJ) // 128) * 128
GLA_ROLL, RW_ROLL = DN_PROJ - GLA_WIN0, DN_PROJ + GLA_PROJ - RW_WIN0
GLA_WIN, RW_WIN = GLA_SEG + 128, RW_SEG + 128
LANE = 128
VMEM_LIMIT = 56 * 1024 * 1024


def _dot(a, b):
    return jnp.dot(a.astype(BF16), b.astype(BF16), preferred_element_type=F32)


def _dot_nt(a, b):
    return lax.dot_general(a.astype(BF16), b.astype(BF16), (((1,), (1,)), ((), ())),
                           preferred_element_type=F32)


def _dot_tn(a, b):
    return lax.dot_general(a.astype(BF16), b.astype(BF16), (((0,), (0,)), ((), ())),
                           preferred_element_type=F32)


def _sigmoid(x):
    return jax.nn.sigmoid(x)


def _silu(x):
    return x * jax.nn.sigmoid(x)


def _softplus(x):
    return jnp.maximum(x, 0.0) + jnp.log1p(jnp.exp(-jnp.abs(x)))


def _iota2(shape, dim):
    return lax.broadcasted_iota(jnp.int32, shape, dim)


def _modulated_norm(x, ng, shift, scale):
    ms = jnp.mean(x * x, axis=-1, keepdims=True)
    return x * lax.rsqrt(ms + EPS) * ng * (1.0 + scale) + shift


def _l2norm_rows(x):
    return x * lax.rsqrt(jnp.sum(x * x, axis=-1, keepdims=True) + EPS)


def _rms_rows(x):
    return x * lax.rsqrt(jnp.mean(x * x, axis=-1, keepdims=True) + EPS)


def _seg_sum(x, seg_ones):
    rows, width = x.shape
    hi = x.astype(BF16)
    lo = (x - hi.astype(F32)).astype(BF16)
    out = []
    for j in range(width // seg_ones.shape[0]):
        sl = slice(j * seg_ones.shape[0], (j + 1) * seg_ones.shape[0])
        t = jnp.dot(jnp.concatenate([hi[:, sl], lo[:, sl]], axis=0), seg_ones, preferred_element_type=F32)
        out.append(t[0:rows] + t[rows:])
    return jnp.concatenate(out, axis=1)


def _split3(x):
    hi = x.astype(BF16)
    rest = x - hi.astype(F32)
    mid = rest.astype(BF16)
    return hi, mid, (rest - mid.astype(F32)).astype(BF16)


def _sum_dot(sel01, x):
    return jnp.dot(jnp.concatenate([sel01] * 3, axis=1), jnp.concatenate(_split3(x), axis=0),
                   preferred_element_type=F32)


def _chunk_tri(n):
    r, c = _iota2((n, n), 0), _iota2((n, n), 1)
    return (((r >> 6) == (c >> 6)) & (r >= c)).astype(BF16)


def _bd_masks(n=CHUNK):
    r, c = _iota2((n, n), 0), _iota2((n, n), 1)
    same = (r >> 6) == (c >> 6)
    return dict(
        causal=same & (r >= c), strict=same & (r > c), eye=(r == c).astype(F32), same=same,
        bd16=(r >> 4) == (c >> 4),
        off32=((r >> 5) == (c >> 5)) & ((r >> 4) != (c >> 4)),
        off64=same & ((r >> 5) != (c >> 5)))


def _expand_rows(a, n_heads, shift):
    head = _iota2(a.shape, 1) >> shift
    zero = jnp.zeros_like(a)
    return jnp.concatenate([jnp.where(head == hh, a, zero) for hh in range(n_heads)], axis=0)


def _sbs_masks(n_heads):
    shape = (CHUNK, n_heads * CHUNK)
    r, c = _iota2(shape, 0), _iota2(shape, 1) & (CHUNK - 1)
    return dict(
        causal=r >= c, strict=r > c, eye=(r == c).astype(F32),
        bd16=(r >> 4) == (c >> 4),
        off32=((r >> 5) == (c >> 5)) & ((r >> 4) != (c >> 4)),
        off64=(r >> 5) != (c >> 5))


def _unit_lower_inverse(Ls, m, n_heads):
    bd = lambda a: _expand_rows(a.astype(BF16), n_heads, 6)
    n = range(len(Ls))
    Ld = [jnp.where(m["bd16"], L, 0.0) for L in Ls]
    X = [m["eye"] - a for a in Ld]
    P = [_dot(a, bd(a)) for a in Ld]
    for _ in range(2):
        t = [_dot(jnp.concatenate([P[i], X[i]], axis=0), bd(P[i])) for i in n]
        P = [a[0:CHUNK] for a in t]
        X = [X[i] + t[i][CHUNK:] for i in n]
    X = [X[i] + _dot(X[i], bd(P[i])) for i in n]
    for off in ("off32", "off64"):
        Y = [_dot(X[i], bd(jnp.where(m[off], Ls[i], 0.0))) for i in n]
        X = [X[i] - _dot(Y[i], bd(X[i])) for i in n]
    return X


def _shift_lanes_left(x, shift):
    return pltpu.roll(x, x.shape[1] - shift, 1) if shift else x


def _load_weight_window(w_ref, w_s, shift):
    width = w_s.shape[1]
    for r0 in range(0, w_ref.shape[0], 256):
        blk = _shift_lanes_left(w_ref[r0:r0 + 256, :].astype(F32), shift)
        w_s[r0:r0 + 256, :] = blk[:, 0:width].astype(BF16)


def _col_from_row(row, n, lane_off=0):
    w = row.shape[1]
    sel = (_iota2((n, w), 0) + lane_off) == _iota2((n, w), 1)
    return jnp.sum(jnp.where(sel, row, 0.0), axis=1, keepdims=True)


def _params(n_axes):
    return pltpu.CompilerParams(dimension_semantics=("arbitrary",) * n_axes,
                                vmem_limit_bytes=VMEM_LIMIT)


def _mod_kernel(c_ref, w_ref, b_ref, o_ref):
    o_ref[0] = _dot(_silu(c_ref[...]), w_ref[0]) + b_ref[0]


def _modulation(c_all, ada_w, ada_b):
    depth, d, n = ada_w.shape
    rows = c_all.shape[0]
    tn = 1024
    return pl.pallas_call(
        _mod_kernel,
        grid=(depth, n // tn),
        in_specs=[pl.BlockSpec((rows, d), lambda l, j: (0, 0)),
                  pl.BlockSpec((1, d, tn), lambda l, j: (l, 0, j)),
                  pl.BlockSpec((1, 1, tn), lambda l, j: (l, 0, j))],
        out_specs=pl.BlockSpec((1, rows, tn), lambda l, j: (l, 0, j)),
        out_shape=jax.ShapeDtypeStruct((depth, rows, n), F32),
        compiler_params=_params(2),
        name="adaln_mod",
    )(c_all, ada_w, ada_b.reshape(depth, 1, n))


def _dn_prompt_kernel(x_ref, mod_ref, ng_ref, w_ref, cw_ref, gp_ref, dng_ref,
                      o_ref, conv_ref, s_ref,
                      pbuf, qs, ks, vs, gb, gcs, os_, U_s, W_s, Q_s, A_s, K_s, E_s, S):
    t = pl.program_id(1)
    tb = x_ref.shape[1]
    nb = DN_HEADS * CHUNK
    m = _sbs_masks(DN_HEADS)
    lane_c = _iota2((CHUNK, LANE), 1)
    sbs_head = _iota2((CHUNK, nb), 1) >> 6
    gate_lanes = ((_iota2((8, LANE), 1) >= DN_HEADS) & (_iota2((8, LANE), 1) < 2 * DN_HEADS)).astype(BF16)
    heads = range(DN_HEADS)

    def stack(f):
        return jnp.concatenate([f(hh) for hh in heads], axis=0)

    def stack_heads(a):
        return stack(lambda hh: a[:, hh * DN_DK:(hh + 1) * DN_DK])

    def per_head_lanes(col_of):
        return jnp.concatenate([jnp.broadcast_to(col_of(hh), (CHUNK, DN_DK)) for hh in heads], axis=1)

    @pl.when(t == 0)
    def _():
        pbuf[0:8, :] = jnp.zeros((8, DN_SEG), F32)
        S[...] = jnp.zeros_like(S)

    h = _modulated_norm(x_ref[0], ng_ref[...], mod_ref[0, 0:1, :], mod_ref[0, 1:2, :])
    pbuf[8:8 + tb, :] = _dot(h, w_ref[...])

    conv = pbuf[5:5 + tb, 0:DN_QKV] * cw_ref[0:1, :]
    for j in range(1, 4):
        conv = conv + pbuf[5 + j:5 + j + tb, 0:DN_QKV] * cw_ref[j:j + 1, :]
    conv = _silu(conv)
    for hh in range(DN_HEADS):
        lo = hh * DN_DK
        qs[:, lo:lo + DN_DK] = _l2norm_rows(conv[:, lo:lo + DN_DK]) * (DN_DK ** -0.5)
        ks[:, lo:lo + DN_DK] = _l2norm_rows(conv[:, 512 + lo:512 + lo + DN_DK])
    vs[...] = conv[:, 1024:DN_QKV]
    gseg = pbuf[8:8 + tb, 2048:DN_SEG]
    lane = _iota2((tb, LANE), 1)
    gb[...] = jnp.where(lane < DN_HEADS, _sigmoid(gseg),
                        -jnp.exp(gp_ref[0:1, :]) * _softplus(gseg + gp_ref[1:2, :]))

    gcs[...] = _sum_dot(_chunk_tri(tb), gb[...])

    def chunk_matrices(c):
        rows = pl.ds(c * CHUNK, CHUNK)
        gbc = gb[rows, :]
        gc = gcs[rows, :]
        g_of = lambda hh: gc[:, DN_HEADS + hh:DN_HEADS + hh + 1]
        g_col = jnp.zeros((CHUNK, nb), F32)
        for hh in heads:
            g_col = jnp.where(sbs_head == hh, g_of(hh), g_col)
        g_sel = stack(lambda hh: jnp.where(lane_c == DN_HEADS + hh, gc, 0.0))
        g_row = lax.dot_general(jnp.concatenate([gate_lanes] * 3, axis=1),
                                jnp.concatenate(_split3(g_sel), axis=1),
                                (((1,), (1,)), ((), ())), preferred_element_type=F32)[0:1, :]
        decay = jnp.exp(jnp.where(m["causal"], g_col - g_row, NEG))
        k_all = ks[rows, :]
        q_all = qs[rows, :]
        beta_l = per_head_lanes(lambda hh: gbc[:, hh:hh + 1])
        g_l = per_head_lanes(g_of)
        eg_l = jnp.exp(g_l)
        kb_all = k_all * beta_l
        expand = lambda a: _expand_rows(a.astype(BF16), DN_HEADS, 7)
        kq = _dot_nt(jnp.concatenate([kb_all, q_all], axis=0), expand(k_all))
        L = jnp.where(m["strict"], kq[0:CHUNK] * decay, 0.0)
        A_s[c] = _expand_rows((kq[CHUNK:2 * CHUNK] * decay).astype(BF16), DN_HEADS, 6)
        Q_s[c] = stack_heads(q_all * eg_l).astype(BF16)
        K_s[c] = stack_heads(k_all * jnp.exp(g_l[CHUNK - 1:CHUNK, :] - g_l)).astype(BF16)
        E_s[c] = jnp.exp(gc[CHUNK - 8:CHUNK, :])
        return L, expand(vs[rows, :] * beta_l), expand(kb_all * eg_l)

    n_chunks = tb // CHUNK
    pre = [chunk_matrices(c) for c in range(n_chunks)]
    inv = _unit_lower_inverse([p[0] for p in pre], m, DN_HEADS)
    for c in range(n_chunks):
        X = inv[c].astype(BF16)
        U_s[c] = stack_heads(_dot(X, pre[c][1]))
        W_s[c] = stack_heads(_dot(X, pre[c][2])).astype(BF16)

    def recurrence(c, carry):
        rows = pl.ds(pl.multiple_of(c * CHUNK, CHUNK), CHUNK)
        w, qe, kh = W_s[c], Q_s[c], K_s[c]
        e_last = E_s[c][7:8, :]
        ws = []
        for hh in heads:
            sl = slice(hh * CHUNK, (hh + 1) * CHUNK)
            ws.append(jnp.dot(jnp.concatenate([w[sl], qe[sl]], axis=0), S[hh].astype(BF16),
                              preferred_element_type=F32))
        v_new = U_s[c] - jnp.concatenate([x[0:CHUNK] for x in ws], axis=0)
        vb = v_new.astype(BF16)
        o_st = jnp.concatenate([x[CHUNK:2 * CHUNK] for x in ws], axis=0) + jnp.dot(
            A_s[c], vb, preferred_element_type=F32)
        for hh in heads:
            sl = slice(hh * CHUNK, (hh + 1) * CHUNK)
            S[hh] = S[hh] * e_last[:, DN_HEADS + hh:DN_HEADS + hh + 1] + _dot_tn(kh[sl], vb[sl])
            os_[rows, hh * DN_DV:(hh + 1) * DN_DV] = o_st[sl]
        return carry

    lax.fori_loop(0, n_chunks, recurrence, 0)

    for hh in heads:
        lo = hh * DN_DV
        z = pbuf[8:8 + tb, DN_QKV + lo:DN_QKV + lo + DN_DV]
        o_ref[0, :, lo:lo + DN_DV] = (_rms_rows(os_[:, lo:lo + DN_DV]) * dng_ref[...] * _silu(z)).astype(o_ref.dtype)

    pbuf[0:8, :] = pbuf[tb:tb + 8, :]

    @pl.when(t == pl.num_programs(1) - 1)
    def _():
        conv_ref[0] = pbuf[tb + 5:tb + 8, 0:DN_QKV]
        s_ref[0] = S[...]


def _dn_prompt(x, mod, ng, w_seg, cw, gp, dng, tb):
    b, t, d = x.shape
    nch, nb = tb // CHUNK, DN_HEADS * CHUNK
    return pl.pallas_call(
        _dn_prompt_kernel,
        grid=(b, t // tb),
        in_specs=[pl.BlockSpec((1, tb, d), lambda i, j: (i, j, 0)),
                  pl.BlockSpec((1, 3, d), lambda i, j: (i, 0, 0)),
                  pl.BlockSpec((1, d), lambda i, j: (0, 0)),
                  pl.BlockSpec((d, DN_SEG), lambda i, j: (0, 0)),
                  pl.BlockSpec((4, DN_QKV), lambda i, j: (0, 0)),
                  pl.BlockSpec((2, LANE), lambda i, j: (0, 0)),
                  pl.BlockSpec((1, DN_DV), lambda i, j: (0, 0))],
        out_specs=[pl.BlockSpec((1, tb, 512), lambda i, j: (i, j, 0)),
                   pl.BlockSpec((1, 3, DN_QKV), lambda i, j: (i, 0, 0)),
                   pl.BlockSpec((1, DN_HEADS, DN_DK, DN_DV), lambda i, j: (i, 0, 0, 0))],
        out_shape=[jax.ShapeDtypeStruct((b, t, 512), BF16),
                   jax.ShapeDtypeStruct((b, 3, DN_QKV), F32),
                   jax.ShapeDtypeStruct((b, DN_HEADS, DN_DK, DN_DV), F32)],
        scratch_shapes=[pltpu.VMEM((tb + 8, DN_SEG), F32),
                        pltpu.VMEM((tb, 512), F32), pltpu.VMEM((tb, 512), F32),
                        pltpu.VMEM((tb, 512), F32), pltpu.VMEM((tb, LANE), F32),
                        pltpu.VMEM((tb, LANE), F32), pltpu.VMEM((tb, 512), F32),
                        pltpu.VMEM((nch, nb, DN_DV), F32), pltpu.VMEM((nch, nb, DN_DV), BF16),
                        pltpu.VMEM((nch, nb, DN_DK), BF16), pltpu.VMEM((nch, nb, nb), BF16),
                        pltpu.VMEM((nch, nb, DN_DK), BF16), pltpu.VMEM((nch, 8, LANE), F32),
                        pltpu.VMEM((DN_HEADS, DN_DK, DN_DV), F32)],
        compiler_params=_params(2),
        name="dn_prompt",
    )(x, mod, ng, w_seg, cw, gp, dng)


def _gla_prompt_kernel(x_ref, mod_ref, ng_ref, w_ref, wf_ref, bf_ref, gng_ref,
                       o_ref, s_ref,
                       w_s, pbuf, lfs, os_, QE_s, KH_s, EG_s, ST):
    t = pl.program_id(1)
    tb = x_ref.shape[1]
    m = _bd_masks()

    @pl.when((pl.program_id(0) == 0) & (t == 0))
    def _():
        _load_weight_window(w_ref, w_s, GLA_ROLL)

    n_sub = CHUNK // SUB
    lane = _iota2((CHUNK, LANE), 1)
    head1 = lane >= GLA_DK
    rowblk = _iota2((CHUNK, LANE), 0) >> 4
    rowloc = _iota2((CHUNK, LANE), 0) & (SUB - 1)
    er = _iota2((2 * SUB, SUB * LANE), 0)
    ec = _iota2((2 * SUB, SUB * LANE), 1)
    eblk = (((ec >> 7) == (er & (SUB - 1))) & (((ec & (LANE - 1)) >= GLA_DK) == (er >= SUB))).astype(BF16)
    st_r = _iota2((2 * GLA_DV, LANE), 0)
    st_c = _iota2((2 * GLA_DV, LANE), 1)
    st_mask = (st_r >= GLA_DV) == (st_c >= GLA_DK)

    @pl.when(t == 0)
    def _():
        ST[...] = jnp.zeros_like(ST)

    h = _modulated_norm(x_ref[0], ng_ref[...], mod_ref[0, 0:1, :], mod_ref[0, 1:2, :])
    pbuf[...] = _dot(h, w_s[...])
    pre = _dot(pbuf[:, 1536:GLA_SEG], wf_ref[...]) + bf_ref[...]
    lfs[...] = (jnp.minimum(pre, 0.0) - jnp.log1p(jnp.exp(-jnp.abs(pre)))) * (1.0 / GLA_TAU)

    lfs[...] = _sum_dot(_chunk_tri(tb), lfs[...])

    def intra_matrices(c, p):
        rows = pl.ds(c * CHUNK, CHUNK)
        lo = p * LANE
        G = lfs[rows, lo:lo + LANE]
        q = pbuf[rows, lo:lo + LANE] * (GLA_DK ** -0.5)
        k = pbuf[rows, 256 + lo:256 + lo + LANE]
        G3 = G.reshape(n_sub, SUB, LANE)
        q3 = q.reshape(n_sub, SUB, LANE)

        def pick(a3, i):
            return jnp.broadcast_to(a3[:, i:i + 1, :], (n_sub, SUB, LANE)).reshape(CHUNK, LANE)

        q_t = q * jnp.exp(G - pick(G3, 0))
        q_parts, k_parts = [], []
        for i in range(1, n_sub):
            q_parts.append(jnp.where(rowblk == i, q_t, 0.0))
            ref_i = G[i * SUB:i * SUB + 1, :]
            k_parts.append(k * jnp.exp(jnp.where(rowblk < i, ref_i - G, NEG)))
        q_big = jnp.concatenate(q_parts, axis=1)
        k_big = jnp.concatenate(k_parts, axis=1)
        head1_big = jnp.concatenate([head1] * (n_sub - 1), axis=1)
        q_both = jnp.concatenate([jnp.where(head1_big, 0.0, q_big),
                                  jnp.where(head1_big, q_big, 0.0)], axis=0)
        a_off = _dot_nt(q_both, k_big)

        slabs = []
        for i in range(SUB):
            dec = jnp.exp(jnp.where(rowloc <= i, pick(G3, i) - G, NEG))
            slabs.append((k * pick(q3, i) * dec).astype(BF16))
        a_diag = lax.dot_general(eblk, jnp.concatenate(slabs, axis=1), (((1,), (1,)), ((), ())),
                                 preferred_element_type=F32)

        g_end = G[CHUNK - 1:CHUNK, :]
        QE_s[c, p] = (q * jnp.exp(G)).astype(BF16)
        KH_s[c, p] = (k * jnp.exp(g_end - G)).astype(BF16)
        EG_s[c, p] = jnp.exp(G[CHUNK - 8:CHUNK, :])
        return a_off, a_diag

    def intra_output(c, p, a_off, a_diag):
        rows = pl.ds(c * CHUNK, CHUNK)
        for hl in range(2):
            hh = 2 * p + hl
            a_d = a_diag[hl * SUB:(hl + 1) * SUB, :]
            a_h = a_off[hl * CHUNK:(hl + 1) * CHUNK, :] + jnp.where(
                m["bd16"], jnp.concatenate([a_d] * n_sub, axis=0), 0.0)
            os_[rows, hh * GLA_DV:(hh + 1) * GLA_DV] = _dot(a_h, pbuf[rows, 512 + hh * GLA_DV:512 + (hh + 1) * GLA_DV])

    n_chunks = tb // CHUNK
    problems = [(c, p) for c in range(n_chunks) for p in range(GLA_HEADS // 2)]
    mats = [intra_matrices(c, p) for c, p in problems]
    for (c, p), ab in zip(problems, mats):
        intra_output(c, p, *ab)

    def recurrence(c, carry):
        rows = pl.ds(pl.multiple_of(c * CHUNK, CHUNK), CHUNK)
        pairs = range(GLA_HEADS // 2)
        st = [ST[p] for p in pairs]
        o_inter = [_dot_nt(QE_s[c, p], st[p]) for p in pairs]
        kv = [_dot_tn(pbuf[rows, 512 + 2 * p * LANE:512 + 2 * (p + 1) * LANE], KH_s[c, p]) for p in pairs]
        for p in pairs:
            ST[p] = st[p] * EG_s[c, p][7:8, :] + jnp.where(st_mask, kv[p], 0.0)
            sl = slice(2 * p * LANE, 2 * (p + 1) * LANE)
            os_[rows, sl] = os_[rows, sl] + o_inter[p]
        return carry

    lax.fori_loop(0, n_chunks, recurrence, 0)

    for hh in range(GLA_HEADS):
        sl = slice(hh * GLA_DV, (hh + 1) * GLA_DV)
        z = pbuf[:, 1024 + hh * GLA_DV:1024 + (hh + 1) * GLA_DV]
        o_ref[0, :, sl] = (_rms_rows(os_[:, sl]) * gng_ref[...] * _silu(z)).astype(o_ref.dtype)

    @pl.when(t == pl.num_programs(1) - 1)
    def _():
        s_ref[0] = ST[...]


def _gla_prompt(x, mod, ng, w_seg, wf, bf, gng, tb):
    b, t, d = x.shape
    return pl.pallas_call(
        _gla_prompt_kernel,
        grid=(b, t // tb),
        in_specs=[pl.BlockSpec((1, tb, d), lambda i, j: (i, j, 0)),
                  pl.BlockSpec((1, 3, d), lambda i, j: (i, 0, 0)),
                  pl.BlockSpec((1, d), lambda i, j: (0, 0)),
                  pl.BlockSpec((d, GLA_WIN), lambda i, j: (0, 0)),
                  pl.BlockSpec((LANE, 256), lambda i, j: (0, 0)),
                  pl.BlockSpec((1, 256), lambda i, j: (0, 0)),
                  pl.BlockSpec((1, GLA_DV), lambda i, j: (0, 0))],
        out_specs=[pl.BlockSpec((1, tb, 512), lambda i, j: (i, j, 0)),
                   pl.BlockSpec((1, 2, 2 * GLA_DV, LANE), lambda i, j: (i, 0, 0, 0))],
        out_shape=[jax.ShapeDtypeStruct((b, t, 512), BF16),
                   jax.ShapeDtypeStruct((b, 2, 2 * GLA_DV, LANE), F32)],
        scratch_shapes=[pltpu.VMEM((d, GLA_SEG), BF16),
                        pltpu.VMEM((tb, GLA_SEG), F32), pltpu.VMEM((tb, 256), F32),
                        pltpu.VMEM((tb, 512), F32),
                        pltpu.VMEM((tb // CHUNK, 2, CHUNK, LANE), BF16),
                        pltpu.VMEM((tb // CHUNK, 2, CHUNK, LANE), BF16),
                        pltpu.VMEM((tb // CHUNK, 2, 8, LANE), F32),
                        pltpu.VMEM((2, 2 * GLA_DV, LANE), F32)],
        compiler_params=_params(2),
        name="gla_prompt",
    )(x, mod, ng, w_seg, wf, bf, gng)


def _rw_features(xm, rp_ref, w2_ref, a2_ref, seg_ones):
    r = xm[:, 0:RW_C]
    k = xm[:, RW_C:2 * RW_C]
    v = xm[:, 2 * RW_C:3 * RW_C]
    lo = xm[:, 3 * RW_C:RW_SHIFT]
    w_log = -_softplus(-(rp_ref[0:1, :] + _dot(jnp.tanh(lo), w2_ref[...]))) - 0.5
    logw = -jnp.exp(w_log)
    a = _sigmoid(rp_ref[1:2, :] + _dot(lo, a2_ref[...]))
    kk = k * rp_ref[2:3, :]
    kk = kk * lax.rsqrt(_seg_sum(kk * kk, seg_ones) + EPS)
    k = k * (1.0 + (a - 1.0) * rp_ref[3:4, :])
    return r, k, v, kk, kk * a, logw


def _rw_finish(o, r, k, v, rz, rp_ref, seg_ones):
    mu = _seg_sum(o, seg_ones) * (1.0 / RW_N)
    dev = o - mu
    var = _seg_sum(dev * dev, seg_ones) * (1.0 / RW_N)
    on = dev * lax.rsqrt(var + RW_LN_EPS) * rp_ref[5:6, :] + rp_ref[6:7, :]
    bonus = _seg_sum(r * k * rp_ref[4:5, :], seg_ones) * v
    return (on + bonus) * _silu(rz)


def _seg_ones():
    n = RW_GROUP * RW_N
    return ((_iota2((n, n), 0) >> 6) == (_iota2((n, n), 1) >> 6)).astype(BF16)


def _rw_prompt_kernel(x_ref, mod_ref, ng_ref, w_ref, mu_ref, rp_ref, w2_ref, a2_ref,
                      o_ref, sh_ref, s_ref,
                      w_s, pbuf, rs, ks, vs, kks, bs, lws, gs, os_,
                      P1_s, KT_s, P2_s, RH_s, AB_s, KB_s, E_s, S):
    t = pl.program_id(1)
    tb = x_ref.shape[1]
    gw = RW_GROUP * RW_N

    @pl.when((pl.program_id(0) == 0) & (t == 0))
    def _():
        _load_weight_window(w_ref, w_s, RW_ROLL)

    m = _sbs_masks(RW_GROUP)
    same_head = (_iota2((gw, gw), 0) >> 6) == (_iota2((gw, gw), 1) >> 6)
    seg_ones = _seg_ones()
    mid = CHUNK // 2

    def expand(a):
        return _expand_rows(a.astype(BF16), RW_GROUP, 6)

    @pl.when(t == 0)
    def _():
        pbuf[0:8, :] = jnp.zeros((8, RW_SEG), F32)
        S[...] = jnp.zeros_like(S)

    h = _modulated_norm(x_ref[0], ng_ref[...], mod_ref[0, 0:1, :], mod_ref[0, 1:2, :])
    pbuf[8:8 + tb, :] = _dot(h, w_s[...])
    xs = pbuf[8:8 + tb, 0:RW_SHIFT]
    prev = pbuf[7:7 + tb, 0:RW_SHIFT]
    xm = xs + (prev - xs) * mu_ref[...]
    r, k, v, kk, b, logw = _rw_features(xm, rp_ref, w2_ref, a2_ref, seg_ones)
    rs[...] = r
    ks[...] = k
    vs[...] = v
    kks[...] = kk
    bs[...] = b
    lws[...] = logw
    gs[...] = _sum_dot(_chunk_tri(tb), logw)

    def chunk_matrices(c, g):
        rows = pl.ds(c * CHUNK, CHUNK)
        sl = slice(g * gw, (g + 1) * gw)
        G = gs[rows, sl]
        Gx = G - lws[rows, sl]
        ref = G[mid:mid + 1, :]
        r_, k_, v_, kk_, b_ = rs[rows, sl], ks[rows, sl], vs[rows, sl], kks[rows, sl], bs[rows, sl]
        e_out = jnp.exp(ref - G)
        lhs = jnp.concatenate([kk_ * jnp.exp(Gx - ref), r_ * jnp.exp(G - ref)], axis=0)
        rhs = jnp.concatenate([expand(b_ * e_out), expand(k_ * e_out)], axis=0)
        mm = _dot_nt(lhs, rhs)
        Lb = jnp.where(m["strict"], mm[0:CHUNK, 0:gw], 0.0)
        Mk = jnp.where(m["strict"], mm[0:CHUNK, gw:], 0.0)
        Arb = jnp.where(m["causal"], mm[CHUNK:, 0:gw], 0.0)
        Ark = jnp.where(m["causal"], mm[CHUNK:, gw:], 0.0)
        kv = _dot(jnp.concatenate([Mk, Ark], axis=0), expand(v_))
        P2_s[c, g] = kv[CHUNK:]
        RH_s[c, g] = (r_ * jnp.exp(G)).astype(BF16)
        AB_s[c, g] = Arb.astype(BF16)
        e_end = jnp.exp(G[CHUNK - 1:CHUNK, :] - G)
        KB_s[c, g] = jnp.concatenate([k_ * e_end, -(b_ * e_end)], axis=0).astype(BF16)
        E_s[c, g] = jnp.exp(G[CHUNK - 8:CHUNK, :])
        return Lb, jnp.concatenate([expand(kv[0:CHUNK]), expand(kk_ * jnp.exp(Gx))], axis=1)

    n_chunks = tb // CHUNK
    problems = [(c, g) for c in range(n_chunks) for g in range(RW_HEADS // RW_GROUP)]
    pre = [chunk_matrices(c, g) for c, g in problems]
    inv = _unit_lower_inverse([p[0] for p in pre], m, RW_GROUP)
    for (c, g), X, p in zip(problems, inv, pre):
        sol = _dot(X, p[1])
        P1_s[c, g] = sol[:, 0:gw]
        KT_s[c, g] = sol[:, gw:].astype(BF16)

    def recurrence(c, carry):
        rows = pl.ds(pl.multiple_of(c * CHUNK, CHUNK), CHUNK)
        groups = range(RW_HEADS // RW_GROUP)
        sls = [slice(g * gw, (g + 1) * gw) for g in groups]
        s_old = [S[g] for g in groups]
        khu = [_dot_nt(jnp.concatenate([KT_s[c, g], RH_s[c, g]], axis=0), s_old[g]) for g in groups]
        u_ = [P1_s[c, g] + khu[g][0:CHUNK] for g in groups]
        au = [_dot(AB_s[c, g], expand(u_[g])) for g in groups]
        upd = [_dot_tn(jnp.concatenate([vs[rows, sls[g]], u_[g]], axis=0), KB_s[c, g]) for g in groups]
        for g in groups:
            S[g] = s_old[g] * E_s[c, g][7:8, :] + jnp.where(same_head, upd[g], 0.0)
            os_[rows, sls[g]] = khu[g][CHUNK:2 * CHUNK] + P2_s[c, g] - au[g]
        return carry

    lax.fori_loop(0, n_chunks, recurrence, 0)

    rz = pbuf[8:8 + tb, RW_SHIFT:RW_SEG]
    o_ref[0] = _rw_finish(os_[...], rs[...], ks[...], vs[...], rz, rp_ref, seg_ones).astype(o_ref.dtype)
    pbuf[0:8, :] = pbuf[tb:tb + 8, :]

    @pl.when(t == pl.num_programs(1) - 1)
    def _():
        sh_ref[0] = pbuf[tb + 7:tb + 8, 0:RW_SHIFT]
        s_ref[0] = S[...]


def _rw_prompt(x, mod, ng, w_seg, mu, rp, w2, a2, tb):
    b, t, d = x.shape
    nch, ng_, gw = tb // CHUNK, RW_HEADS // RW_GROUP, RW_GROUP * RW_N
    return pl.pallas_call(
        _rw_prompt_kernel,
        grid=(b, t // tb),
        in_specs=[pl.BlockSpec((1, tb, d), lambda i, j: (i, j, 0)),
                  pl.BlockSpec((1, 3, d), lambda i, j: (i, 0, 0)),
                  pl.BlockSpec((1, d), lambda i, j: (0, 0)),
                  pl.BlockSpec((d, RW_WIN), lambda i, j: (0, 0)),
                  pl.BlockSpec((1, RW_SHIFT), lambda i, j: (0, 0)),
                  pl.BlockSpec((8, RW_C), lambda i, j: (0, 0)),
                  pl.BlockSpec((LANE, RW_C), lambda i, j: (0, 0)),
                  pl.BlockSpec((LANE, RW_C), lambda i, j: (0, 0))],
        out_specs=[pl.BlockSpec((1, tb, RW_C), lambda i, j: (i, j, 0)),
                   pl.BlockSpec((1, 1, RW_SHIFT), lambda i, j: (i, 0, 0)),
                   pl.BlockSpec((1, ng_, gw, gw), lambda i, j: (i, 0, 0, 0))],
        out_shape=[jax.ShapeDtypeStruct((b, t, RW_C), BF16),
                   jax.ShapeDtypeStruct((b, 1, RW_SHIFT), F32),
                   jax.ShapeDtypeStruct((b, ng_, gw, gw), F32)],
        scratch_shapes=[pltpu.VMEM((d, RW_SEG), BF16), pltpu.VMEM((tb + 8, RW_SEG), F32)]
        + [pltpu.VMEM((tb, RW_C), F32) for _ in range(8)]
        + [pltpu.VMEM((nch, ng_, CHUNK, gw), F32), pltpu.VMEM((nch, ng_, CHUNK, gw), BF16),
           pltpu.VMEM((nch, ng_, CHUNK, gw), F32), pltpu.VMEM((nch, ng_, CHUNK, gw), BF16),
           pltpu.VMEM((nch, ng_, CHUNK, gw), BF16), pltpu.VMEM((nch, ng_, 2 * CHUNK, gw), BF16),
           pltpu.VMEM((nch, ng_, 8, gw), F32),
           pltpu.VMEM((ng_, gw, gw), F32)],
        compiler_params=_params(2),
        name="rw_prompt",
    )(x, mod, ng, w_seg, mu, rp, w2, a2)


def _outproj_kernel(od_ref, og_ref, or_ref, x_ref, gate_ref, w_ref, fg_ref, y_ref, *, final):
    acc = (_dot(od_ref[0], w_ref[0:512, :]) + _dot(og_ref[0], w_ref[512:1024, :])
           + _dot(or_ref[0], w_ref[1024:1536, :]))
    xn = x_ref[0] + gate_ref[0] * acc
    if final:
        xn = _rms_rows(xn) * fg_ref[...]
    y_ref[0] = xn


def _outproj(o_dn, o_gla, o_rw, x, gate, w_out, fg, final, tm):
    b, t, d = x.shape
    tg = gate.shape[1]
    gspec = (pl.BlockSpec((1, 1, d), lambda i, j: (i, 0, 0)) if tg == 1
             else pl.BlockSpec((1, tm, d), lambda i, j: (i, j, 0)))
    ospec = pl.BlockSpec((1, tm, 512), lambda i, j: (i, j, 0))
    return pl.pallas_call(
        functools.partial(_outproj_kernel, final=final),
        grid=(b, t // tm),
        in_specs=[ospec, ospec, ospec,
                  pl.BlockSpec((1, tm, d), lambda i, j: (i, j, 0)),
                  gspec,
                  pl.BlockSpec((1536, d), lambda i, j: (0, 0)),
                  pl.BlockSpec((1, d), lambda i, j: (0, 0))],
        out_specs=pl.BlockSpec((1, tm, d), lambda i, j: (i, j, 0)),
        out_shape=jax.ShapeDtypeStruct((b, t, d), F32),
        compiler_params=_params(2),
        name="out_proj",
    )(o_dn, o_gla, o_rw, x, gate, w_out, fg)


def _dec_proj_kernel(x_ref, shift_ref, scale_ref, ng_ref, w_ref, o_ref, *, roll):
    h = _modulated_norm(x_ref[...], ng_ref[...], shift_ref[...], scale_ref[...])
    o_ref[...] = _shift_lanes_left(_dot(h, w_ref[...]), roll)[:, 0:o_ref.shape[1]]


def _dec_proj(x, shift, scale, ng, w_win, roll, seg):
    n, d = x.shape
    win = w_win.shape[1]
    full = lambda shape: pl.BlockSpec(shape, lambda i: (0,) * len(shape))
    return pl.pallas_call(
        functools.partial(_dec_proj_kernel, roll=roll),
        grid=(1,),
        in_specs=[full((n, d)), full((n, d)), full((n, d)), full((1, d)), full((d, win))],
        out_specs=full((n, seg)),
        out_shape=jax.ShapeDtypeStruct((n, seg), F32),
        compiler_params=_params(1),
        name="dec_proj",
    )(x, shift, scale, ng, w_win)


def _dn_dec_kernel(p_ref, cst_ref, s_ref, cw_ref, gp_ref, dng_ref,
                   o_ref, cnew_ref, sn_ref, o_scr):
    bt = p_ref.shape[0]
    qkv = p_ref[:, 0:DN_QKV]
    c0, c1, c2 = cst_ref[:, 0, :], cst_ref[:, 1, :], cst_ref[:, 2, :]
    conv = _silu(c0 * cw_ref[0:1, :] + c1 * cw_ref[1:2, :] + c2 * cw_ref[2:3, :] + qkv * cw_ref[3:4, :])
    cnew_ref[:, 0, :] = c1
    cnew_ref[:, 1, :] = c2
    cnew_ref[:, 2, :] = qkv
    gseg = p_ref[:, 2048:DN_SEG]
    beta_all = _sigmoid(gseg)
    a_all = jnp.exp(-jnp.exp(gp_ref[0:1, :]) * _softplus(gseg + gp_ref[1:2, :]))
    for hh in range(DN_HEADS):
        lo = hh * DN_DK
        q = _l2norm_rows(conv[:, lo:lo + DN_DK]) * (DN_DK ** -0.5)
        k = _l2norm_rows(conv[:, 512 + lo:512 + lo + DN_DK])
        v = conv[:, 1024 + lo:1024 + lo + DN_DV]
        for bi in range(bt):
            k_col = _col_from_row(k[bi:bi + 1, :], DN_DK)
            q_col = _col_from_row(q[bi:bi + 1, :], DN_DK)
            a = a_all[bi:bi + 1, DN_HEADS + hh:DN_HEADS + hh + 1]
            beta = beta_all[bi:bi + 1, hh:hh + 1]
            s_old = s_ref[bi, hh]
            sk = jnp.sum(s_old * k_col, axis=0, keepdims=True)
            v_new = beta * (v[bi:bi + 1, :] - a * sk)
            s_new = s_old * a + k_col * v_new
            sn_ref[bi, hh] = s_new
            o_scr[bi:bi + 1, lo:lo + DN_DV] = jnp.sum(s_new * q_col, axis=0, keepdims=True)
    for hh in range(DN_HEADS):
        lo = hh * DN_DV
        z = p_ref[:, DN_QKV + lo:DN_QKV + lo + DN_DV]
        o_ref[:, lo:lo + DN_DV] = _rms_rows(o_scr[:, lo:lo + DN_DV]) * dng_ref[...] * _silu(z)


def _dn_dec(p, cst, s, cw, gp, dng, bt=8):
    n = p.shape[0]
    return pl.pallas_call(
        _dn_dec_kernel,
        grid=(n // bt,),
        in_specs=[pl.BlockSpec((bt, DN_SEG), lambda i: (i, 0)),
                  pl.BlockSpec((bt, 3, DN_QKV), lambda i: (i, 0, 0)),
                  pl.BlockSpec((bt, DN_HEADS, DN_DK, DN_DV), lambda i: (i, 0, 0, 0)),
                  pl.BlockSpec((4, DN_QKV), lambda i: (0, 0)),
                  pl.BlockSpec((2, LANE), lambda i: (0, 0)),
                  pl.BlockSpec((1, DN_DV), lambda i: (0, 0))],
        out_specs=[pl.BlockSpec((bt, 512), lambda i: (i, 0)),
                   pl.BlockSpec((bt, 3, DN_QKV), lambda i: (i, 0, 0)),
                   pl.BlockSpec((bt, DN_HEADS, DN_DK, DN_DV), lambda i: (i, 0, 0, 0))],
        out_shape=[jax.ShapeDtypeStruct((n, 512), F32),
                   jax.ShapeDtypeStruct((n, 3, DN_QKV), F32),
                   jax.ShapeDtypeStruct((n, DN_HEADS, DN_DK, DN_DV), F32)],
        scratch_shapes=[pltpu.VMEM((bt, 512), F32)],
        compiler_params=_params(1),
        name="dn_dec",
    )(p, cst, s, cw, gp, dng)


def _gla_dec_kernel(p_ref, s_ref, wf_ref, bf_ref, gng_ref, o_ref, sn_ref, o_scr):
    bt = p_ref.shape[0]
    pre = _dot(p_ref[:, 1536:GLA_SEG], wf_ref[...]) + bf_ref[...]
    alpha = jnp.exp((jnp.minimum(pre, 0.0) - jnp.log1p(jnp.exp(-jnp.abs(pre)))) * (1.0 / GLA_TAU))
    q_all = p_ref[:, 0:256] * (GLA_DK ** -0.5)
    k_all = p_ref[:, 256:512]
    for hh in range(GLA_HEADS):
        pair, off = (hh // 2) * LANE, (hh % 2) * GLA_DK
        v = p_ref[:, 512 + hh * GLA_DV:512 + (hh + 1) * GLA_DV]
        for bi in range(bt):
            a_col = _col_from_row(alpha[bi:bi + 1, pair:pair + LANE], GLA_DK, off)
            k_col = _col_from_row(k_all[bi:bi + 1, pair:pair + LANE], GLA_DK, off)
            q_col = _col_from_row(q_all[bi:bi + 1, pair:pair + LANE], GLA_DK, off)
            s_new = s_ref[bi, hh] * a_col + k_col * v[bi:bi + 1, :]
            sn_ref[bi, hh] = s_new
            o_scr[bi:bi + 1, hh * GLA_DV:(hh + 1) * GLA_DV] = jnp.sum(s_new * q_col, axis=0, keepdims=True)
    for hh in range(GLA_HEADS):
        lo = hh * GLA_DV
        z = p_ref[:, 1024 + lo:1024 + lo + GLA_DV]
        o_ref[:, lo:lo + GLA_DV] = _rms_rows(o_scr[:, lo:lo + GLA_DV]) * gng_ref[...] * _silu(z)


def _gla_dec(p, s, wf, bf, gng, bt=8):
    n = p.shape[0]
    return pl.pallas_call(
        _gla_dec_kernel,
        grid=(n // bt,),
        in_specs=[pl.BlockSpec((bt, GLA_SEG), lambda i: (i, 0)),
                  pl.BlockSpec((bt, GLA_HEADS, GLA_DK, GLA_DV), lambda i: (i, 0, 0, 0)),
                  pl.BlockSpec((LANE, 256), lambda i: (0, 0)),
                  pl.BlockSpec((1, 256), lambda i: (0, 0)),
                  pl.BlockSpec((1, GLA_DV), lambda i: (0, 0))],
        out_specs=[pl.BlockSpec((bt, 512), lambda i: (i, 0)),
                   pl.BlockSpec((bt, GLA_HEADS, GLA_DK, GLA_DV), lambda i: (i, 0, 0, 0))],
        out_shape=[jax.ShapeDtypeStruct((n, 512), F32),
                   jax.ShapeDtypeStruct((n, GLA_HEADS, GLA_DK, GLA_DV), F32)],
        scratch_shapes=[pltpu.VMEM((bt, 512), F32)],
        compiler_params=_params(1),
        name="gla_dec",
    )(p, s, wf, bf, gng)


def _rw_dec_kernel(p_ref, sh_ref, s_ref, mu_ref, rp_ref, w2_ref, a2_ref,
                   o_ref, shn_ref, sn_ref, o_scr):
    bt = p_ref.shape[0]
    seg_ones = _seg_ones()
    xs = p_ref[:, 0:RW_SHIFT]
    shn_ref[...] = xs
    xm = xs + (sh_ref[...] - xs) * mu_ref[...]
    r, k, v, kk, b, logw = _rw_features(xm, rp_ref, w2_ref, a2_ref, seg_ones)
    w = jnp.exp(logw)
    ones = jnp.ones((RW_N, RW_N), BF16)
    eye = _iota2((RW_N, RW_N), 0) == _iota2((RW_N, RW_N), 1)
    seqs = range(bt)

    def row_sums(a):
        hi = a.astype(BF16)
        lo = (a - hi.astype(F32)).astype(BF16)
        return (jnp.dot(hi, ones, preferred_element_type=F32) + jnp.dot(lo, ones, preferred_element_type=F32))

    for hh in range(RW_HEADS):
        sl = slice(hh * RW_N, (hh + 1) * RW_N)
        s_old = [s_ref[bi, hh] for bi in seqs]
        red = row_sums(jnp.concatenate([s_old[bi] * kk[bi:bi + 1, sl] for bi in seqs]
                                       + [jnp.where(eye, v[bi:bi + 1, sl], 0.0) for bi in seqs], axis=0))
        s_new = []
        for bi in seqs:
            row = slice(bi, bi + 1)
            skk = red[bi * RW_N:(bi + 1) * RW_N]
            v_col = red[(bt + bi) * RW_N:(bt + bi + 1) * RW_N]
            s_new.append(s_old[bi] * w[row, sl] - skk * b[row, sl] + v_col * k[row, sl])
            sn_ref[bi, hh] = s_new[bi]
        red = row_sums(jnp.concatenate([s_new[bi] * r[bi:bi + 1, sl] for bi in seqs], axis=0))
        for bi in seqs:
            o_scr[bi:bi + 1, sl] = jnp.sum(jnp.where(eye, red[bi * RW_N:(bi + 1) * RW_N], 0.0),
                                           axis=0, keepdims=True)
    rz = p_ref[:, RW_SHIFT:RW_SEG]
    o_ref[...] = _rw_finish(o_scr[...], r, k, v, rz, rp_ref, seg_ones)


def _rw_dec(p, sh, s, mu, rp, w2, a2, bt=8):
    n = p.shape[0]
    return pl.pallas_call(
        _rw_dec_kernel,
        grid=(n // bt,),
        in_specs=[pl.BlockSpec((bt, RW_SEG), lambda i: (i, 0)),
                  pl.BlockSpec((bt, RW_SHIFT), lambda i: (i, 0)),
                  pl.BlockSpec((bt, RW_HEADS, RW_N, RW_N), lambda i: (i, 0, 0, 0)),
                  pl.BlockSpec((1, RW_SHIFT), lambda i: (0, 0)),
                  pl.BlockSpec((8, RW_C), lambda i: (0, 0)),
                  pl.BlockSpec((LANE, RW_C), lambda i: (0, 0)),
                  pl.BlockSpec((LANE, RW_C), lambda i: (0, 0))],
        out_specs=[pl.BlockSpec((bt, RW_C), lambda i: (i, 0)),
                   pl.BlockSpec((bt, RW_SHIFT), lambda i: (i, 0)),
                   pl.BlockSpec((bt, RW_HEADS, RW_N, RW_N), lambda i: (i, 0, 0, 0))],
        out_shape=[jax.ShapeDtypeStruct((n, RW_C), F32),
                   jax.ShapeDtypeStruct((n, RW_SHIFT), F32),
                   jax.ShapeDtypeStruct((n, RW_HEADS, RW_N, RW_N), F32)],
        scratch_shapes=[pltpu.VMEM((bt, RW_C), F32)],
        compiler_params=_params(1),
        name="rw_dec",
    )(p, sh, s, mu, rp, w2, a2)


def _layer_params(l, norm_g, w_in, dn_conv_w, dn_a_log, dn_dt_bias, dn_norm_g, gla_wf, gla_bf,
                  gla_norm_g, rw_mu, rw_w0, rw_w2, rw_a0, rw_a2, rw_k_k, rw_k_a, rw_r_k,
                  rw_ln_w, rw_ln_b, w_out):
    w = w_in[l].astype(BF16)
    gp = jnp.zeros((2, LANE), F32)
    gp = gp.at[0, DN_HEADS:2 * DN_HEADS].set(dn_a_log[l]).at[1, DN_HEADS:2 * DN_HEADS].set(dn_dt_bias[l])
    zeros_r = jnp.zeros((1, RW_C), F32)
    rp = jnp.concatenate([rw_w0[l][None], rw_a0[l][None], rw_k_k[l][None], rw_k_a[l][None],
                          rw_r_k[l].reshape(1, RW_C), rw_ln_w[l][None], rw_ln_b[l][None], zeros_r], axis=0)
    return dict(
        ng=norm_g[l][None],
        w_dn=w[:, 0:DN_SEG],
        w_gla=w[:, GLA_WIN0:GLA_WIN0 + GLA_WIN],
        w_rw=jnp.pad(w[:, RW_WIN0:], ((0, 0), (0, RW_WIN - (D_PROJ - RW_WIN0)))),
        cw=dn_conv_w[l], gp=gp, dng=dn_norm_g[l][None],
        wf=jnp.pad(gla_wf[l], ((0, LANE - GLA_RANK), (0, 0))).astype(BF16),
        bf=gla_bf[l][None], gng=gla_norm_g[l][None],
        mu=rw_mu[l][None], rp=rp,
        w2=jnp.pad(rw_w2[l], ((0, LANE - 64), (0, 0))).astype(BF16),
        a2=jnp.pad(rw_a2[l], ((64, 0), (0, 0))).astype(BF16),
        w_out=w_out[l].astype(BF16))


def _gla_state_from_pairs(st):
    heads = []
    for hh in range(GLA_HEADS):
        p, hl = hh // 2, hh % 2
        blk = st[:, p, hl * GLA_DV:(hl + 1) * GLA_DV, hl * GLA_DK:(hl + 1) * GLA_DK]
        heads.append(jnp.swapaxes(blk, 1, 2))
    return jnp.stack(heads, axis=1)


def _rw_state_from_groups(s):
    heads = []
    for hh in range(RW_HEADS):
        g, hl = hh // RW_GROUP, hh % RW_GROUP
        heads.append(s[:, g, hl * RW_N:(hl + 1) * RW_N, hl * RW_N:(hl + 1) * RW_N])
    return jnp.stack(heads, axis=1)


def _prompt_tiles(t):
    tb = 256 if t % 256 == 0 else CHUNK
    tm = 512 if t % 512 == 0 else tb
    return tb, tm


def _prompt_trunk(x, mods, layers, final_g):
    b, t, _ = x.shape
    tb, tm = _prompt_tiles(t)
    states = []
    n_layers = len(layers)
    for l, lp in enumerate(layers):
        mod = mods[l]
        o_dn, conv_new, dn_new = _dn_prompt(x, mod, lp["ng"], lp["w_dn"], lp["cw"], lp["gp"], lp["dng"], tb)
        o_gla, gla_st = _gla_prompt(x, mod, lp["ng"], lp["w_gla"], lp["wf"], lp["bf"], lp["gng"], tb)
        o_rw, rs_new, rw_st = _rw_prompt(x, mod, lp["ng"], lp["w_rw"], lp["mu"], lp["rp"], lp["w2"], lp["a2"], tb)
        x = _outproj(o_dn, o_gla, o_rw, x, mod[:, 2:3, :], lp["w_out"], final_g, l == n_layers - 1, tm)
        states.append((conv_new, dn_new, _gla_state_from_pairs(gla_st), rs_new[:, 0, :],
                       _rw_state_from_groups(rw_st)))
    return x, states


def _sample_trunk(x, mods, layers, final_g, conv_s, dn_s, gla_s, rs_s, rw_s):
    n = x.shape[0]
    x = x.reshape(n, D_MODEL)
    states = []
    n_layers = len(layers)
    for l, lp in enumerate(layers):
        shift, scale, gate = mods[l][:, 0, :], mods[l][:, 1, :], mods[l][:, 2, :]
        p_dn = _dec_proj(x, shift, scale, lp["ng"], lp["w_dn"], 0, DN_SEG)
        p_gla = _dec_proj(x, shift, scale, lp["ng"], lp["w_gla"], GLA_ROLL, GLA_SEG)
        p_rw = _dec_proj(x, shift, scale, lp["ng"], lp["w_rw"], RW_ROLL, RW_SEG)
        o_dn, conv_new, dn_new = _dn_dec(p_dn, conv_s[l], dn_s[l], lp["cw"], lp["gp"], lp["dng"])
        o_gla, gla_new = _gla_dec(p_gla, gla_s[l], lp["wf"], lp["bf"], lp["gng"])
        o_rw, rs_new, rw_new = _rw_dec(p_rw, rs_s[l], rw_s[l], lp["mu"], lp["rp"], lp["w2"], lp["a2"])
        x = _outproj(o_dn[None], o_gla[None], o_rw[None], x[None], gate[None], lp["w_out"], final_g,
                     l == n_layers - 1, n)[0]
        states.append((conv_new, dn_new, gla_new, rs_new, rw_new))
    return x.reshape(n, 1, D_MODEL), states


def kernel(x_prompt, x_sample, c_prompt, c_sample, state_dn_conv, state_dn, state_gla, state_rwkv_shift, state_rwkv, norm_g, ada_w, ada_b, w_in, dn_conv_w, dn_a_log, dn_dt_bias, dn_norm_g, gla_wf, gla_bf, gla_norm_g, rw_mu, rw_w0, rw_w2, rw_a0, rw_a2, rw_k_k, rw_k_a, rw_r_k, rw_ln_w, rw_ln_b, w_out, final_norm_g):
    depth = w_in.shape[0]
    bp = x_prompt.shape[0]
    layers = [_layer_params(l, norm_g, w_in, dn_conv_w, dn_a_log, dn_dt_bias, dn_norm_g, gla_wf, gla_bf,
                            gla_norm_g, rw_mu, rw_w0, rw_w2, rw_a0, rw_a2, rw_k_k, rw_k_a, rw_r_k,
                            rw_ln_w, rw_ln_b, w_out) for l in range(depth)]
    final_g = final_norm_g[None]
    mod_all = _modulation(jnp.concatenate([c_prompt, c_sample], axis=0), ada_w, ada_b)
    mod_all = mod_all.reshape(depth, -1, 3, D_MODEL)
    mods_p = [mod_all[l, :bp] for l in range(depth)]
    mods_s = [mod_all[l, bp:] for l in range(depth)]

    y_p, st_p = _prompt_trunk(x_prompt, mods_p, layers, final_g)
    y_s, st_s = _sample_trunk(x_sample, mods_s, layers, final_g, state_dn_conv, state_dn, state_gla,
                              state_rwkv_shift, state_rwkv)
    stack = lambda sts, i: jnp.stack([s[i] for s in sts])
    return (y_p, y_s,
            stack(st_p, 0), stack(st_p, 1), stack(st_p, 2), stack(st_p, 3), stack(st_p, 4),
            stack(st_s, 0), stack(st_s, 1), stack(st_s, 2), stack(st_s, 3), stack(st_s, 4))
```

```python
import functools

import jax
import jax.numpy as jnp
from jax import lax
from jax.experimental import pallas as pl
from jax.experimental.pallas import tpu as pltpu

F32 = jnp.float32
BF16 = jnp.bfloat16

D_MODEL = 1024
EPS = 1e-6
CHUNK = 64
SUB = 16
NEG = -1e30

DN_HEADS, DN_DK, DN_DV = 4, 128, 128
GLA_HEADS, GLA_DK, GLA_DV, GLA_RANK, GLA_TAU = 4, 64, 128, 16, 16.0
RW_HEADS, RW_N = 8, 64
RW_GROUP = 4
RW_LN_EPS = 64e-5
DN_QKV = 1536
RW_C = 512
RW_SHIFT = 1664
DN_SEG, GLA_SEG, RW_SEG = 2176, 1664, 2176
DN_PROJ, GLA_PROJ, D_PROJ = 2056, 1552, 5784
GLA_WIN0, RW_WIN0 = (DN_PROJ // 128) * 128, ((DN_PROJ + GLA_PROJ) // 128) * 128
GLA_ROLL, RW_ROLL = DN_PROJ - GLA_WIN0, DN_PROJ + GLA_PROJ - RW_WIN0
GLA_WIN, RW_WIN = GLA_SEG + 128, RW_SEG + 128
LANE = 128
VMEM_LIMIT = 56 * 1024 * 1024


def _dot(a, b):
    return jnp.dot(a.astype(BF16), b.astype(BF16), preferred_element_type=F32)


def _dot_nt(a, b):
    return lax.dot_general(a.astype(BF16), b.astype(BF16), (((1,), (1,)), ((), ())),
                           preferred_element_type=F32)


def _dot_tn(a, b):
    return lax.dot_general(a.astype(BF16), b.astype(BF16), (((0,), (0,)), ((), ())),
                           preferred_element_type=F32)


def _sigmoid(x):
    return jax.nn.sigmoid(x)


def _silu(x):
    return x * jax.nn.sigmoid(x)


def _softplus(x):
    return jnp.maximum(x, 0.0) + jnp.log1p(jnp.exp(-jnp.abs(x)))


def _iota2(shape, dim):
    return lax.broadcasted_iota(jnp.int32, shape, dim)


def _modulated_norm(x, ng, shift, scale):
    ms = jnp.mean(x * x, axis=-1, keepdims=True)
    return x * lax.rsqrt(ms + EPS) * ng * (1.0 + scale) + shift


def _l2norm_rows(x):
    return x * lax.rsqrt(jnp.sum(x * x, axis=-1, keepdims=True) + EPS)


def _rms_rows(x):
    return x * lax.rsqrt(jnp.mean(x * x, axis=-1, keepdims=True) + EPS)


def _seg_sum(x, seg_ones):
    rows, width = x.shape
    hi = x.astype(BF16)
    lo = (x - hi.astype(F32)).astype(BF16)
    out = []
    for j in range(width // seg_ones.shape[0]):
        sl = slice(j * seg_ones.shape[0], (j + 1) * seg_ones.shape[0])
        t = jnp.dot(jnp.concatenate([hi[:, sl], lo[:, sl]], axis=0), seg_ones, preferred_element_type=F32)
        out.append(t[0:rows] + t[rows:])
    return jnp.concatenate(out, axis=1)


def _split3(x):
    hi = x.astype(BF16)
    rest = x - hi.astype(F32)
    mid = rest.astype(BF16)
    return hi, mid, (rest - mid.astype(F32)).astype(BF16)


def _sum_dot(sel01, x):
    return jnp.dot(jnp.concatenate([sel01] * 3, axis=1), jnp.concatenate(_split3(x), axis=0),
                   preferred_element_type=F32)


def _chunk_tri(n):
    r, c = _iota2((n, n), 0), _iota2((n, n), 1)
    return (((r >> 6) == (c >> 6)) & (r >= c)).astype(BF16)


def _bd_masks(n=CHUNK):
    r, c = _iota2((n, n), 0), _iota2((n, n), 1)
    same = (r >> 6) == (c >> 6)
    return dict(
        causal=same & (r >= c), strict=same & (r > c), eye=(r == c).astype(F32), same=same,
        bd16=(r >> 4) == (c >> 4),
        off32=((r >> 5) == (c >> 5)) & ((r >> 4) != (c >> 4)),
        off64=same & ((r >> 5) != (c >> 5)))


def _expand_rows(a, n_heads, shift):
    head = _iota2(a.shape, 1) >> shift
    zero = jnp.zeros_like(a)
    return jnp.concatenate([jnp.where(head == hh, a, zero) for hh in range(n_heads)], axis=0)


def _sbs_masks(n_heads):
    shape = (CHUNK, n_heads * CHUNK)
    r, c = _iota2(shape, 0), _iota2(shape, 1) & (CHUNK - 1)
    return dict(
        causal=r >= c, strict=r > c, eye=(r == c).astype(F32),
        bd16=(r >> 4) == (c >> 4),
        off32=((r >> 5) == (c >> 5)) & ((r >> 4) != (c >> 4)),
        off64=(r >> 5) != (c >> 5))


def _unit_lower_inverse(Ls, m, n_heads):
    bd = lambda a: _expand_rows(a.astype(BF16), n_heads, 6)
    n = range(len(Ls))
    Ld = [jnp.where(m["bd16"], L, 0.0) for L in Ls]
    X = [m["eye"] - a for a in Ld]
    P = [_dot(a, bd(a)) for a in Ld]
    for _ in range(2):
        t = [_dot(jnp.concatenate([P[i], X[i]], axis=0), bd(P[i])) for i in n]
        P = [a[0:CHUNK] for a in t]
        X = [X[i] + t[i][CHUNK:] for i in n]
    X = [X[i] + _dot(X[i], bd(P[i])) for i in n]
    for off in ("off32", "off64"):
        Y = [_dot(X[i], bd(jnp.where(m[off], Ls[i], 0.0))) for i in n]
        X = [X[i] - _dot(Y[i], bd(X[i])) for i in n]
    return X


def _shift_lanes_left(x, shift):
    return pltpu.roll(x, x.shape[1] - shift, 1) if shift else x


def _load_weight_window(w_ref, w_s, shift):
    width = w_s.shape[1]
    for r0 in range(0, w_ref.shape[0], 256):
        blk = _shift_lanes_left(w_ref[r0:r0 + 256, :].astype(F32), shift)
        w_s[r0:r0 + 256, :] = blk[:, 0:width].astype(BF16)


def _col_from_row(row, n, lane_off=0):
    w = row.shape[1]
    sel = (_iota2((n, w), 0) + lane_off) == _iota2((n, w), 1)
    return jnp.sum(jnp.where(sel, row, 0.0), axis=1, keepdims=True)


def _params(n_axes):
    return pltpu.CompilerParams(dimension_semantics=("arbitrary",) * n_axes,
                                vmem_limit_bytes=VMEM_LIMIT)


def _mod_kernel(c_ref, w_ref, b_ref, o_ref):
    o_ref[0] = _dot(_silu(c_ref[...]), w_ref[0]) + b_ref[0]


def _modulation(c_all, ada_w, ada_b):
    depth, d, n = ada_w.shape
    rows = c_all.shape[0]
    tn = 1024
    return pl.pallas_call(
        _mod_kernel,
        grid=(depth, n // tn),
        in_specs=[pl.BlockSpec((rows, d), lambda l, j: (0, 0)),
                  pl.BlockSpec((1, d, tn), lambda l, j: (l, 0, j)),
                  pl.BlockSpec((1, 1, tn), lambda l, j: (l, 0, j))],
        out_specs=pl.BlockSpec((1, rows, tn), lambda l, j: (l, 0, j)),
        out_shape=jax.ShapeDtypeStruct((depth, rows, n), F32),
        compiler_params=_params(2),
        name="adaln_mod",
    )(c_all, ada_w, ada_b.reshape(depth, 1, n))


def _dn_prompt_kernel(x_ref, mod_ref, ng_ref, w_ref, cw_ref, gp_ref, dng_ref,
                      o_ref, conv_ref, s_ref,
                      pbuf, qs, ks, vs, gb, gcs, os_, U_s, W_s, Q_s, A_s, K_s, E_s, S):
    t = pl.program_id(1)
    tb = x_ref.shape[1]
    nb = DN_HEADS * CHUNK
    m = _sbs_masks(DN_HEADS)
    lane_c = _iota2((CHUNK, LANE), 1)
    sbs_head = _iota2((CHUNK, nb), 1) >> 6
    gate_lanes = ((_iota2((8, LANE), 1) >= DN_HEADS) & (_iota2((8, LANE), 1) < 2 * DN_HEADS)).astype(BF16)
    heads = range(DN_HEADS)

    def stack(f):
        return jnp.concatenate([f(hh) for hh in heads], axis=0)

    def stack_heads(a):
        return stack(lambda hh: a[:, hh * DN_DK:(hh + 1) * DN_DK])

    def per_head_lanes(col_of):
        return jnp.concatenate([jnp.broadcast_to(col_of(hh), (CHUNK, DN_DK)) for hh in heads], axis=1)

    @pl.when(t == 0)
    def _():
        pbuf[0:8, :] = jnp.zeros((8, DN_SEG), F32)
        S[...] = jnp.zeros_like(S)

    h = _modulated_norm(x_ref[0], ng_ref[...], mod_ref[0, 0:1, :], mod_ref[0, 1:2, :])
    pbuf[8:8 + tb, :] = _dot(h, w_ref[...])

    conv = pbuf[5:5 + tb, 0:DN_QKV] * cw_ref[0:1, :]
    for j in range(1, 4):
        conv = conv + pbuf[5 + j:5 + j + tb, 0:DN_QKV] * cw_ref[j:j + 1, :]
    conv = _silu(conv)
    for hh in range(DN_HEADS):
        lo = hh * DN_DK
        qs[:, lo:lo + DN_DK] = _l2norm_rows(conv[:, lo:lo + DN_DK]) * (DN_DK ** -0.5)
        ks[:, lo:lo + DN_DK] = _l2norm_rows(conv[:, 512 + lo:512 + lo + DN_DK])
    vs[...] = conv[:, 1024:DN_QKV]
    gseg = pbuf[8:8 + tb, 2048:DN_SEG]
    lane = _iota2((tb, LANE), 1)
    gb[...] = jnp.where(lane < DN_HEADS, _sigmoid(gseg),
                        -jnp.exp(gp_ref[0:1, :]) * _softplus(gseg + gp_ref[1:2, :]))

    gcs[...] = _sum_dot(_chunk_tri(tb), gb[...])

    def chunk_matrices(c):
        rows = pl.ds(c * CHUNK, CHUNK)
        gbc = gb[rows, :]
        gc = gcs[rows, :]
        g_of = lambda hh: gc[:, DN_HEADS + hh:DN_HEADS + hh + 1]
        g_col = jnp.zeros((CHUNK, nb), F32)
        for hh in heads:
            g_col = jnp.where(sbs_head == hh, g_of(hh), g_col)
        g_sel = stack(lambda hh: jnp.where(lane_c == DN_HEADS + hh, gc, 0.0))
        g_row = lax.dot_general(jnp.concatenate([gate_lanes] * 3, axis=1),
                                jnp.concatenate(_split3(g_sel), axis=1),
                                (((1,), (1,)), ((), ())), preferred_element_type=F32)[0:1, :]
        decay = jnp.exp(jnp.where(m["causal"], g_col - g_row, NEG))
        k_all = ks[rows, :]
        q_all = qs[rows, :]
        beta_l = per_head_lanes(lambda hh: gbc[:, hh:hh + 1])
        g_l = per_head_lanes(g_of)
        eg_l = jnp.exp(g_l)
        kb_all = k_all * beta_l
        expand = lambda a: _expand_rows(a.astype(BF16), DN_HEADS, 7)
        kq = _dot_nt(jnp.concatenate([kb_all, q_all], axis=0), expand(k_all))
        L = jnp.where(m["strict"], kq[0:CHUNK] * decay, 0.0)
        A_s[c] = _expand_rows((kq[CHUNK:2 * CHUNK] * decay).astype(BF16), DN_HEADS, 6)
        Q_s[c] = stack_heads(q_all * eg_l).astype(BF16)
        K_s[c] = stack_heads(k_all * jnp.exp(g_l[CHUNK - 1:CHUNK, :] - g_l)).astype(BF16)
        E_s[c] = jnp.exp(gc[CHUNK - 8:CHUNK, :])
        return L, expand(vs[rows, :] * beta_l), expand(kb_all * eg_l)

    n_chunks = tb // CHUNK
    pre = [chunk_matrices(c) for c in range(n_chunks)]
    inv = _unit_lower_inverse([p[0] for p in pre], m, DN_HEADS)
    for c in range(n_chunks):
        X = inv[c].astype(BF16)
        U_s[c] = stack_heads(_dot(X, pre[c][1]))
        W_s[c] = stack_heads(_dot(X, pre[c][2])).astype(BF16)

    def recurrence(c, carry):
        rows = pl.ds(pl.multiple_of(c * CHUNK, CHUNK), CHUNK)
        w, qe, kh = W_s[c], Q_s[c], K_s[c]
        e_last = E_s[c][7:8, :]
        ws = []
        for hh in heads:
            sl = slice(hh * CHUNK, (hh + 1) * CHUNK)
            ws.append(jnp.dot(jnp.concatenate([w[sl], qe[sl]], axis=0), S[hh].astype(BF16),
                              preferred_element_type=F32))
        v_new = U_s[c] - jnp.concatenate([x[0:CHUNK] for x in ws], axis=0)
        vb = v_new.astype(BF16)
        o_st = jnp.concatenate([x[CHUNK:2 * CHUNK] for x in ws], axis=0) + jnp.dot(
            A_s[c], vb, preferred_element_type=F32)
        for hh in heads:
            sl = slice(hh * CHUNK, (hh + 1) * CHUNK)
            S[hh] = S[hh] * e_last[:, DN_HEADS + hh:DN_HEADS + hh + 1] + _dot_tn(kh[sl], vb[sl])
            os_[rows, hh * DN_DV:(hh + 1) * DN_DV] = o_st[sl]
        return carry

    lax.fori_loop(0, n_chunks, recurrence, 0)

    for hh in heads:
        lo = hh * DN_DV
        z = pbuf[8:8 + tb, DN_QKV + lo:DN_QKV + lo + DN_DV]
        o_ref[0, :, lo:lo + DN_DV] = (_rms_rows(os_[:, lo:lo + DN_DV]) * dng_ref[...] * _silu(z)).astype(o_ref.dtype)

    pbuf[0:8, :] = pbuf[tb:tb + 8, :]

    @pl.when(t == pl.num_programs(1) - 1)
    def _():
        conv_ref[0] = pbuf[tb + 5:tb + 8, 0:DN_QKV]
        s_ref[0] = S[...]


def _dn_prompt(x, mod, ng, w_seg, cw, gp, dng, tb):
    b, t, d = x.shape
    nch, nb = tb // CHUNK, DN_HEADS * CHUNK
    return pl.pallas_call(
        _dn_prompt_kernel,
        grid=(b, t // tb),
        in_specs=[pl.BlockSpec((1, tb, d), lambda i, j: (i, j, 0)),
                  pl.BlockSpec((1, 3, d), lambda i, j: (i, 0, 0)),
                  pl.BlockSpec((1, d), lambda i, j: (0, 0)),
                  pl.BlockSpec((d, DN_SEG), lambda i, j: (0, 0)),
                  pl.BlockSpec((4, DN_QKV), lambda i, j: (0, 0)),
                  pl.BlockSpec((2, LANE), lambda i, j: (0, 0)),
                  pl.BlockSpec((1, DN_DV), lambda i, j: (0, 0))],
        out_specs=[pl.BlockSpec((1, tb, 512), lambda i, j: (i, j, 0)),
                   pl.BlockSpec((1, 3, DN_QKV), lambda i, j: (i, 0, 0)),
                   pl.BlockSpec((1, DN_HEADS, DN_DK, DN_DV), lambda i, j: (i, 0, 0, 0))],
        out_shape=[jax.ShapeDtypeStruct((b, t, 512), BF16),
                   jax.ShapeDtypeStruct((b, 3, DN_QKV), F32),
                   jax.ShapeDtypeStruct((b, DN_HEADS, DN_DK, DN_DV), F32)],
        scratch_shapes=[pltpu.VMEM((tb + 8, DN_SEG), F32),
                        pltpu.VMEM((tb, 512), F32), pltpu.VMEM((tb, 512), F32),
                        pltpu.VMEM((tb, 512), F32), pltpu.VMEM((tb, LANE), F32),
                        pltpu.VMEM((tb, LANE), F32), pltpu.VMEM((tb, 512), F32),
                        pltpu.VMEM((nch, nb, DN_DV), F32), pltpu.VMEM((nch, nb, DN_DV), BF16),
                        pltpu.VMEM((nch, nb, DN_DK), BF16), pltpu.VMEM((nch, nb, nb), BF16),
                        pltpu.VMEM((nch, nb, DN_DK), BF16), pltpu.VMEM((nch, 8, LANE), F32),
                        pltpu.VMEM((DN_HEADS, DN_DK, DN_DV), F32)],
        compiler_params=_params(2),
        name="dn_prompt",
    )(x, mod, ng, w_seg, cw, gp, dng)


def _gla_prompt_kernel(x_ref, mod_ref, ng_ref, w_ref, wf_ref, bf_ref, gng_ref,
                       o_ref, s_ref,
                       w_s, pbuf, lfs, os_, QE_s, KH_s, EG_s, ST):
    t = pl.program_id(1)
    tb = x_ref.shape[1]
    m = _bd_masks()

    @pl.when((pl.program_id(0) == 0) & (t == 0))
    def _():
        _load_weight_window(w_ref, w_s, GLA_ROLL)

    n_sub = CHUNK // SUB
    lane = _iota2((CHUNK, LANE), 1)
    head1 = lane >= GLA_DK
    rowblk = _iota2((CHUNK, LANE), 0) >> 4
    rowloc = _iota2((CHUNK, LANE), 0) & (SUB - 1)
    er = _iota2((2 * SUB, SUB * LANE), 0)
    ec = _iota2((2 * SUB, SUB * LANE), 1)
    eblk = (((ec >> 7) == (er & (SUB - 1))) & (((ec & (LANE - 1)) >= GLA_DK) == (er >= SUB))).astype(BF16)
    st_r = _iota2((2 * GLA_DV, LANE), 0)
    st_c = _iota2((2 * GLA_DV, LANE), 1)
    st_mask = (st_r >= GLA_DV) == (st_c >= GLA_DK)

    @pl.when(t == 0)
    def _():
        ST[...] = jnp.zeros_like(ST)

    h = _modulated_norm(x_ref[0], ng_ref[...], mod_ref[0, 0:1, :], mod_ref[0, 1:2, :])
    pbuf[...] = _dot(h, w_s[...])
    pre = _dot(pbuf[:, 1536:GLA_SEG], wf_ref[...]) + bf_ref[...]
    lfs[...] = (jnp.minimum(pre, 0.0) - jnp.log1p(jnp.exp(-jnp.abs(pre)))) * (1.0 / GLA_TAU)

    lfs[...] = _sum_dot(_chunk_tri(tb), lfs[...])

    def intra_matrices(c, p):
        rows = pl.ds(c * CHUNK, CHUNK)
        lo = p * LANE
        G = lfs[rows, lo:lo + LANE]
        q = pbuf[rows, lo:lo + LANE] * (GLA_DK ** -0.5)
        k = pbuf[rows, 256 + lo:256 + lo + LANE]
        G3 = G.reshape(n_sub, SUB, LANE)
        q3 = q.reshape(n_sub, SUB, LANE)

        def pick(a3, i):
            return jnp.broadcast_to(a3[:, i:i + 1, :], (n_sub, SUB, LANE)).reshape(CHUNK, LANE)

        q_t = q * jnp.exp(G - pick(G3, 0))
        q_parts, k_parts = [], []
        for i in range(1, n_sub):
            q_parts.append(jnp.where(rowblk == i, q_t, 0.0))
            ref_i = G[i * SUB:i * SUB + 1, :]
            k_parts.append(k * jnp.exp(jnp.where(rowblk < i, ref_i - G, NEG)))
        q_big = jnp.concatenate(q_parts, axis=1)
        k_big = jnp.concatenate(k_parts, axis=1)
        head1_big = jnp.concatenate([head1] * (n_sub - 1), axis=1)
        q_both = jnp.concatenate([jnp.where(head1_big, 0.0, q_big),
                                  jnp.where(head1_big, q_big, 0.0)], axis=0)
        a_off = _dot_nt(q_both, k_big)

        slabs = []
        for i in range(SUB):
            dec = jnp.exp(jnp.where(rowloc <= i, pick(G3, i) - G, NEG))
            slabs.append((k * pick(q3, i) * dec).astype(BF16))
        a_diag = lax.dot_general(eblk, jnp.concatenate(slabs, axis=1), (((1,), (1,)), ((), ())),
                                 preferred_element_type=F32)

        g_end = G[CHUNK - 1:CHUNK, :]
        QE_s[c, p] = (q * jnp.exp(G)).astype(BF16)
        KH_s[c, p] = (k * jnp.exp(g_end - G)).astype(BF16)
        EG_s[c, p] = jnp.exp(G[CHUNK - 8:CHUNK, :])
        return a_off, a_diag

    def intra_output(c, p, a_off, a_diag):
        rows = pl.ds(c * CHUNK, CHUNK)
        for hl in range(2):
            hh = 2 * p + hl
            a_d = a_diag[hl * SUB:(hl + 1) * SUB, :]
            a_h = a_off[hl * CHUNK:(hl + 1) * CHUNK, :] + jnp.where(
                m["bd16"], jnp.concatenate([a_d] * n_sub, axis=0), 0.0)
            os_[rows, hh * GLA_DV:(hh + 1) * GLA_DV] = _dot(a_h, pbuf[rows, 512 + hh * GLA_DV:512 + (hh + 1) * GLA_DV])

    n_chunks = tb // CHUNK
    problems = [(c, p) for c in range(n_chunks) for p in range(GLA_HEADS // 2)]
    mats = [intra_matrices(c, p) for c, p in problems]
    for (c, p), ab in zip(problems, mats):
        intra_output(c, p, *ab)

    def recurrence(c, carry):
        rows = pl.ds(pl.multiple_of(c * CHUNK, CHUNK), CHUNK)
        pairs = range(GLA_HEADS // 2)
        st = [ST[p] for p in pairs]
        o_inter = [_dot_nt(QE_s[c, p], st[p]) for p in pairs]
        kv = [_dot_tn(pbuf[rows, 512 + 2 * p * LANE:512 + 2 * (p + 1) * LANE], KH_s[c, p]) for p in pairs]
        for p in pairs:
            ST[p] = st[p] * EG_s[c, p][7:8, :] + jnp.where(st_mask, kv[p], 0.0)
            sl = slice(2 * p * LANE, 2 * (p + 1) * LANE)
            os_[rows, sl] = os_[rows, sl] + o_inter[p]
        return carry

    lax.fori_loop(0, n_chunks, recurrence, 0)

    for hh in range(GLA_HEADS):
        sl = slice(hh * GLA_DV, (hh + 1) * GLA_DV)
        z = pbuf[:, 1024 + hh * GLA_DV:1024 + (hh + 1) * GLA_DV]
        o_ref[0, :, sl] = (_rms_rows(os_[:, sl]) * gng_ref[...] * _silu(z)).astype(o_ref.dtype)

    @pl.when(t == pl.num_programs(1) - 1)
    def _():
        s_ref[0] = ST[...]


def _gla_prompt(x, mod, ng, w_seg, wf, bf, gng, tb):
    b, t, d = x.shape
    return pl.pallas_call(
        _gla_prompt_kernel,
        grid=(b, t // tb),
        in_specs=[pl.BlockSpec((1, tb, d), lambda i, j: (i, j, 0)),
                  pl.BlockSpec((1, 3, d), lambda i, j: (i, 0, 0)),
                  pl.BlockSpec((1, d), lambda i, j: (0, 0)),
                  pl.BlockSpec((d, GLA_WIN), lambda i, j: (0, 0)),
                  pl.BlockSpec((LANE, 256), lambda i, j: (0, 0)),
                  pl.BlockSpec((1, 256), lambda i, j: (0, 0)),
                  pl.BlockSpec((1, GLA_DV), lambda i, j: (0, 0))],
        out_specs=[pl.BlockSpec((1, tb, 512), lambda i, j: (i, j, 0)),
                   pl.BlockSpec((1, 2, 2 * GLA_DV, LANE), lambda i, j: (i, 0, 0, 0))],
        out_shape=[jax.ShapeDtypeStruct((b, t, 512), BF16),
                   jax.ShapeDtypeStruct((b, 2, 2 * GLA_DV, LANE), F32)],
        scratch_shapes=[pltpu.VMEM((d, GLA_SEG), BF16),
                        pltpu.VMEM((tb, GLA_SEG), F32), pltpu.VMEM((tb, 256), F32),
                        pltpu.VMEM((tb, 512), F32),
                        pltpu.VMEM((tb // CHUNK, 2, CHUNK, LANE), BF16),
                        pltpu.VMEM((tb // CHUNK, 2, CHUNK, LANE), BF16),
                        pltpu.VMEM((tb // CHUNK, 2, 8, LANE), F32),
                        pltpu.VMEM((2, 2 * GLA_DV, LANE), F32)],
        compiler_params=_params(2),
        name="gla_prompt",
    )(x, mod, ng, w_seg, wf, bf, gng)


def _rw_features(xm, rp_ref, w2_ref, a2_ref, seg_ones):
    r = xm[:, 0:RW_C]
    k = xm[:, RW_C:2 * RW_C]
    v = xm[:, 2 * RW_C:3 * RW_C]
    lo = xm[:, 3 * RW_C:RW_SHIFT]
    w_log = -_softplus(-(rp_ref[0:1, :] + _dot(jnp.tanh(lo), w2_ref[...]))) - 0.5
    logw = -jnp.exp(w_log)
    a = _sigmoid(rp_ref[1:2, :] + _dot(lo, a2_ref[...]))
    kk = k * rp_ref[2:3, :]
    kk = kk * lax.rsqrt(_seg_sum(kk * kk, seg_ones) + EPS)
    k = k * (1.0 + (a - 1.0) * rp_ref[3:4, :])
    return r, k, v, kk, kk * a, logw


def _rw_finish(o, r, k, v, rz, rp_ref, seg_ones):
    mu = _seg_sum(o, seg_ones) * (1.0 / RW_N)
    dev = o - mu
    var = _seg_sum(dev * dev, seg_ones) * (1.0 / RW_N)
    on = dev * lax.rsqrt(var + RW_LN_EPS) * rp_ref[5:6, :] + rp_ref[6:7, :]
    bonus = _seg_sum(r * k * rp_ref[4:5, :], seg_ones) * v
    return (on + bonus) * _silu(rz)


def _seg_ones():
    n = RW_GROUP * RW_N
    return ((_iota2((n, n), 0) >> 6) == (_iota2((n, n), 1) >> 6)).astype(BF16)


def _rw_prompt_kernel(x_ref, mod_ref, ng_ref, w_ref, mu_ref, rp_ref, w2_ref, a2_ref,
                      o_ref, sh_ref, s_ref,
                      w_s, pbuf, rs, ks, vs, kks, bs, lws, gs, os_,
                      P1_s, KT_s, P2_s, RH_s, AB_s, KB_s, E_s, S):
    t = pl.program_id(1)
    tb = x_ref.shape[1]
    gw = RW_GROUP * RW_N

    @pl.when((pl.program_id(0) == 0) & (t == 0))
    def _():
        _load_weight_window(w_ref, w_s, RW_ROLL)

    m = _sbs_masks(RW_GROUP)
    same_head = (_iota2((gw, gw), 0) >> 6) == (_iota2((gw, gw), 1) >> 6)
    seg_ones = _seg_ones()
    mid = CHUNK // 2

    def expand(a):
        return _expand_rows(a.astype(BF16), RW_GROUP, 6)

    @pl.when(t == 0)
    def _():
        pbuf[0:8, :] = jnp.zeros((8, RW_SEG), F32)
        S[...] = jnp.zeros_like(S)

    h = _modulated_norm(x_ref[0], ng_ref[...], mod_ref[0, 0:1, :], mod_ref[0, 1:2, :])
    pbuf[8:8 + tb, :] = _dot(h, w_s[...])
    xs = pbuf[8:8 + tb, 0:RW_SHIFT]
    prev = pbuf[7:7 + tb, 0:RW_SHIFT]
    xm = xs + (prev - xs) * mu_ref[...]
    r, k, v, kk, b, logw = _rw_features(xm, rp_ref, w2_ref, a2_ref, seg_ones)
    rs[...] = r
    ks[...] = k
    vs[...] = v
    kks[...] = kk
    bs[...] = b
    lws[...] = logw
    gs[...] = _sum_dot(_chunk_tri(tb), logw)

    def chunk_matrices(c, g):
        rows = pl.ds(c * CHUNK, CHUNK)
        sl = slice(g * gw, (g + 1) * gw)
        G = gs[rows, sl]
        Gx = G - lws[rows, sl]
        ref = G[mid:mid + 1, :]
        r_, k_, v_, kk_, b_ = rs[rows, sl], ks[rows, sl], vs[rows, sl], kks[rows, sl], bs[rows, sl]
        e_out = jnp.exp(ref - G)
        lhs = jnp.concatenate([kk_ * jnp.exp(Gx - ref), r_ * jnp.exp(G - ref)], axis=0)
        rhs = jnp.concatenate([expand(b_ * e_out), expand(k_ * e_out)], axis=0)
        mm = _dot_nt(lhs, rhs)
        Lb = jnp.where(m["strict"], mm[0:CHUNK, 0:gw], 0.0)
        Mk = jnp.where(m["strict"], mm[0:CHUNK, gw:], 0.0)
        Arb = jnp.where(m["causal"], mm[CHUNK:, 0:gw], 0.0)
        Ark = jnp.where(m["causal"], mm[CHUNK:, gw:], 0.0)
        kv = _dot(jnp.concatenate([Mk, Ark], axis=0), expand(v_))
        P2_s[c, g] = kv[CHUNK:]
        RH_s[c, g] = (r_ * jnp.exp(G)).astype(BF16)
        AB_s[c, g] = Arb.astype(BF16)
        e_end = jnp.exp(G[CHUNK - 1:CHUNK, :] - G)
        KB_s[c, g] = jnp.concatenate([k_ * e_end, -(b_ * e_end)], axis=0).astype(BF16)
        E_s[c, g] = jnp.exp(G[CHUNK - 8:CHUNK, :])
        return Lb, jnp.concatenate([expand(kv[0:CHUNK]), expand(kk_ * jnp.exp(Gx))], axis=1)

    n_chunks = tb // CHUNK
    problems = [(c, g) for c in range(n_chunks) for g in range(RW_HEADS // RW_GROUP)]
    pre = [chunk_matrices(c, g) for c, g in problems]
    inv = _unit_lower_inverse([p[0] for p in pre], m, RW_GROUP)
    for (c, g), X, p in zip(problems, inv, pre):
        sol = _dot(X, p[1])
        P1_s[c, g] = sol[:, 0:gw]
        KT_s[c, g] = sol[:, gw:].astype(BF16)

    def recurrence(c, carry):
        rows = pl.ds(pl.multiple_of(c * CHUNK, CHUNK), CHUNK)
        groups = range(RW_HEADS // RW_GROUP)
        sls = [slice(g * gw, (g + 1) * gw) for g in groups]
        s_old = [S[g] for g in groups]
        khu = [_dot_nt(jnp.concatenate([KT_s[c, g], RH_s[c, g]], axis=0), s_old[g]) for g in groups]
        u_ = [P1_s[c, g] + khu[g][0:CHUNK] for g in groups]
        au = [_dot(AB_s[c, g], expand(u_[g])) for g in groups]
        upd = [_dot_tn(jnp.concatenate([vs[rows, sls[g]], u_[g]], axis=0), KB_s[c, g]) for g in groups]
        for g in groups:
            S[g] = s_old[g] * E_s[c, g][7:8, :] + jnp.where(same_head, upd[g], 0.0)
            os_[rows, sls[g]] = khu[g][CHUNK:2 * CHUNK] + P2_s[c, g] - au[g]
        return carry

    lax.fori_loop(0, n_chunks, recurrence, 0)

    rz = pbuf[8:8 + tb, RW_SHIFT:RW_SEG]
    o_ref[0] = _rw_finish(os_[...], rs[...], ks[...], vs[...], rz, rp_ref, seg_ones).astype(o_ref.dtype)
    pbuf[0:8, :] = pbuf[tb:tb + 8, :]

    @pl.when(t == pl.num_programs(1) - 1)
    def _():
        sh_ref[0] = pbuf[tb + 7:tb + 8, 0:RW_SHIFT]
        s_ref[0] = S[...]


def _rw_prompt(x, mod, ng, w_seg, mu, rp, w2, a2, tb):
    b, t, d = x.shape
    nch, ng_, gw = tb // CHUNK, RW_HEADS // RW_GROUP, RW_GROUP * RW_N
    return pl.pallas_call(
        _rw_prompt_kernel,
        grid=(b, t // tb),
        in_specs=[pl.BlockSpec((1, tb, d), lambda i, j: (i, j, 0)),
                  pl.BlockSpec((1, 3, d), lambda i, j: (i, 0, 0)),
                  pl.BlockSpec((1, d), lambda i, j: (0, 0)),
                  pl.BlockSpec((d, RW_WIN), lambda i, j: (0, 0)),
                  pl.BlockSpec((1, RW_SHIFT), lambda i, j: (0, 0)),
                  pl.BlockSpec((8, RW_C), lambda i, j: (0, 0)),
                  pl.BlockSpec((LANE, RW_C), lambda i, j: (0, 0)),
                  pl.BlockSpec((LANE, RW_C), lambda i, j: (0, 0))],
        out_specs=[pl.BlockSpec((1, tb, RW_C), lambda i, j: (i, j, 0)),
                   pl.BlockSpec((1, 1, RW_SHIFT), lambda i, j: (i, 0, 0)),
                   pl.BlockSpec((1, ng_, gw, gw), lambda i, j: (i, 0, 0, 0))],
        out_shape=[jax.ShapeDtypeStruct((b, t, RW_C), BF16),
                   jax.ShapeDtypeStruct((b, 1, RW_SHIFT), F32),
                   jax.ShapeDtypeStruct((b, ng_, gw, gw), F32)],
        scratch_shapes=[pltpu.VMEM((d, RW_SEG), BF16), pltpu.VMEM((tb + 8, RW_SEG), F32)]
        + [pltpu.VMEM((tb, RW_C), F32) for _ in range(8)]
        + [pltpu.VMEM((nch, ng_, CHUNK, gw), F32), pltpu.VMEM((nch, ng_, CHUNK, gw), BF16),
           pltpu.VMEM((nch, ng_, CHUNK, gw), F32), pltpu.VMEM((nch, ng_, CHUNK, gw), BF16),
           pltpu.VMEM((nch, ng_, CHUNK, gw), BF16), pltpu.VMEM((nch, ng_, 2 * CHUNK, gw), BF16),
           pltpu.VMEM((nch, ng_, 8, gw), F32),
           pltpu.VMEM((ng_, gw, gw), F32)],
        compiler_params=_params(2),
        name="rw_prompt",
    )(x, mod, ng, w_seg, mu, rp, w2, a2)


def _outproj_kernel(od_ref, og_ref, or_ref, x_ref, gate_ref, w_ref, fg_ref, y_ref, *, final):
    acc = (_dot(od_ref[0], w_ref[0:512, :]) + _dot(og_ref[0], w_ref[512:1024, :])
           + _dot(or_ref[0], w_ref[1024:1536, :]))
    xn = x_ref[0] + gate_ref[0] * acc
    if final:
        xn = _rms_rows(xn) * fg_ref[...]
    y_ref[0] = xn


def _outproj(o_dn, o_gla, o_rw, x, gate, w_out, fg, final, tm):
    b, t, d = x.shape
    tg = gate.shape[1]
    gspec = (pl.BlockSpec((1, 1, d), lambda i, j: (i, 0, 0)) if tg == 1
             else pl.BlockSpec((1, tm, d), lambda i, j: (i, j, 0)))
    ospec = pl.BlockSpec((1, tm, 512), lambda i, j: (i, j, 0))
    return pl.pallas_call(
        functools.partial(_outproj_kernel, final=final),
        grid=(b, t // tm),
        in_specs=[ospec, ospec, ospec,
                  pl.BlockSpec((1, tm, d), lambda i, j: (i, j, 0)),
                  gspec,
                  pl.BlockSpec((1536, d), lambda i, j: (0, 0)),
                  pl.BlockSpec((1, d), lambda i, j: (0, 0))],
        out_specs=pl.BlockSpec((1, tm, d), lambda i, j: (i, j, 0)),
        out_shape=jax.ShapeDtypeStruct((b, t, d), F32),
        compiler_params=_params(2),
        name="out_proj",
    )(o_dn, o_gla, o_rw, x, gate, w_out, fg)


def _dec_proj_kernel(x_ref, shift_ref, scale_ref, ng_ref, w_ref, o_ref, *, roll):
    h = _modulated_norm(x_ref[...], ng_ref[...], shift_ref[...], scale_ref[...])
    o_ref[...] = _shift_lanes_left(_dot(h, w_ref[...]), roll)[:, 0:o_ref.shape[1]]


def _dec_proj(x, shift, scale, ng, w_win, roll, seg):
    n, d = x.shape
    win = w_win.shape[1]
    full = lambda shape: pl.BlockSpec(shape, lambda i: (0,) * len(shape))
    return pl.pallas_call(
        functools.partial(_dec_proj_kernel, roll=roll),
        grid=(1,),
        in_specs=[full((n, d)), full((n, d)), full((n, d)), full((1, d)), full((d, win))],
        out_specs=full((n, seg)),
        out_shape=jax.ShapeDtypeStruct((n, seg), F32),
        compiler_params=_params(1),
        name="dec_proj",
    )(x, shift, scale, ng, w_win)


def _dn_dec_kernel(p_ref, cst_ref, s_ref, cw_ref, gp_ref, dng_ref,
                   o_ref, cnew_ref, sn_ref, o_scr):
    bt = p_ref.shape[0]
    qkv = p_ref[:, 0:DN_QKV]
    c0, c1, c2 = cst_ref[:, 0, :], cst_ref[:, 1, :], cst_ref[:, 2, :]
    conv = _silu(c0 * cw_ref[0:1, :] + c1 * cw_ref[1:2, :] + c2 * cw_ref[2:3, :] + qkv * cw_ref[3:4, :])
    cnew_ref[:, 0, :] = c1
    cnew_ref[:, 1, :] = c2
    cnew_ref[:, 2, :] = qkv
    gseg = p_ref[:, 2048:DN_SEG]
    beta_all = _sigmoid(gseg)
    a_all = jnp.exp(-jnp.exp(gp_ref[0:1, :]) * _softplus(gseg + gp_ref[1:2, :]))
    for hh in range(DN_HEADS):
        lo = hh * DN_DK
        q = _l2norm_rows(conv[:, lo:lo + DN_DK]) * (DN_DK ** -0.5)
        k = _l2norm_rows(conv[:, 512 + lo:512 + lo + DN_DK])
        v = conv[:, 1024 + lo:1024 + lo + DN_DV]
        for bi in range(bt):
            k_col = _col_from_row(k[bi:bi + 1, :], DN_DK)
            q_col = _col_from_row(q[bi:bi + 1, :], DN_DK)
            a = a_all[bi:bi + 1, DN_HEADS + hh:DN_HEADS + hh + 1]
            beta = beta_all[bi:bi + 1, hh:hh + 1]
            s_old = s_ref[bi, hh]
            sk = jnp.sum(s_old * k_col, axis=0, keepdims=True)
            v_new = beta * (v[bi:bi + 1, :] - a * sk)
            s_new = s_old * a + k_col * v_new
            sn_ref[bi, hh] = s_new
            o_scr[bi:bi + 1, lo:lo + DN_DV] = jnp.sum(s_new * q_col, axis=0, keepdims=True)
    for hh in range(DN_HEADS):
        lo = hh * DN_DV
        z = p_ref[:, DN_QKV + lo:DN_QKV + lo + DN_DV]
        o_ref[:, lo:lo + DN_DV] = _rms_rows(o_scr[:, lo:lo + DN_DV]) * dng_ref[...] * _silu(z)


def _layer_state_call(kernel_fn, name, l, prev, args, in_specs, o_width, tails, bt):
    n = args[0].shape[0]
    depth = next(a.shape[0] for a in args if a.ndim >= 3)
    first = prev is None
    n_in = len(args)

    def spec(t):
        z = (0,) * len(t)
        if first:
            return pl.BlockSpec((depth, bt) + t, lambda i: (0, i) + z)
        return pl.BlockSpec((None, bt) + t, lambda i: (l, i) + z)

    def body(*refs):
        ins, rest = refs[:n_in], list(refs[n_in if first else n_in + len(tails):])
        if first:
            for j in range(1, 1 + len(tails)):
                if depth > 1:
                    rest[j][1:] = jnp.zeros((depth - 1,) + rest[j].shape[1:], F32)
                rest[j] = rest[j].at[0]
        kernel_fn(*ins, *rest)

    aliases = {}
    if not first:
        aliases = {n_in + j: 1 + j for j in range(len(tails))}
        in_specs = list(in_specs) + [pl.BlockSpec(memory_space=pl.ANY)] * len(tails)
        args = list(args) + list(prev)
    out = pl.pallas_call(
        body,
        grid=(n // bt,),
        in_specs=in_specs,
        out_specs=[pl.BlockSpec((bt, o_width), lambda i: (i, 0))] + [spec(t) for t in tails],
        out_shape=[jax.ShapeDtypeStruct((n, o_width), F32)]
        + [jax.ShapeDtypeStruct((depth, n) + t, F32) for t in tails],
        scratch_shapes=[pltpu.VMEM((bt, o_width), F32)],
        input_output_aliases=aliases,
        compiler_params=_params(1),
        name=name,
    )(*args)
    return out[0], tuple(out[1:])


def _dn_dec(p, cst_all, s_all, cw, gp, dng, l, prev, bt=8):
    in_specs = [pl.BlockSpec((bt, DN_SEG), lambda i: (i, 0)),
                pl.BlockSpec((None, bt, 3, DN_QKV), lambda i: (l, i, 0, 0)),
                pl.BlockSpec((None, bt, DN_HEADS, DN_DK, DN_DV), lambda i: (l, i, 0, 0, 0)),
                pl.BlockSpec((4, DN_QKV), lambda i: (0, 0)),
                pl.BlockSpec((2, LANE), lambda i: (0, 0)),
                pl.BlockSpec((1, DN_DV), lambda i: (0, 0))]
    return _layer_state_call(_dn_dec_kernel, "dn_dec", l, prev, [p, cst_all, s_all, cw, gp, dng], in_specs,
                             512, [(3, DN_QKV), (DN_HEADS, DN_DK, DN_DV)], bt)


def _gla_dec_kernel(p_ref, s_ref, wf_ref, bf_ref, gng_ref, o_ref, sn_ref, o_scr):
    bt = p_ref.shape[0]
    pre = _dot(p_ref[:, 1536:GLA_SEG], wf_ref[...]) + bf_ref[...]
    alpha = jnp.exp((jnp.minimum(pre, 0.0) - jnp.log1p(jnp.exp(-jnp.abs(pre)))) * (1.0 / GLA_TAU))
    q_all = p_ref[:, 0:256] * (GLA_DK ** -0.5)
    k_all = p_ref[:, 256:512]
    for hh in range(GLA_HEADS):
        pair, off = (hh // 2) * LANE, (hh % 2) * GLA_DK
        v = p_ref[:, 512 + hh * GLA_DV:512 + (hh + 1) * GLA_DV]
        for bi in range(bt):
            a_col = _col_from_row(alpha[bi:bi + 1, pair:pair + LANE], GLA_DK, off)
            k_col = _col_from_row(k_all[bi:bi + 1, pair:pair + LANE], GLA_DK, off)
            q_col = _col_from_row(q_all[bi:bi + 1, pair:pair + LANE], GLA_DK, off)
            s_new = s_ref[bi, hh] * a_col + k_col * v[bi:bi + 1, :]
            sn_ref[bi, hh] = s_new
            o_scr[bi:bi + 1, hh * GLA_DV:(hh + 1) * GLA_DV] = jnp.sum(s_new * q_col, axis=0, keepdims=True)
    for hh in range(GLA_HEADS):
        lo = hh * GLA_DV
        z = p_ref[:, 1024 + lo:1024 + lo + GLA_DV]
        o_ref[:, lo:lo + GLA_DV] = _rms_rows(o_scr[:, lo:lo + GLA_DV]) * gng_ref[...] * _silu(z)


def _gla_dec(p, s_all, wf, bf, gng, l, prev, bt=8):
    in_specs = [pl.BlockSpec((bt, GLA_SEG), lambda i: (i, 0)),
                pl.BlockSpec((None, bt, GLA_HEADS, GLA_DK, GLA_DV), lambda i: (l, i, 0, 0, 0)),
                pl.BlockSpec((LANE, 256), lambda i: (0, 0)),
                pl.BlockSpec((1, 256), lambda i: (0, 0)),
                pl.BlockSpec((1, GLA_DV), lambda i: (0, 0))]
    return _layer_state_call(_gla_dec_kernel, "gla_dec", l, prev, [p, s_all, wf, bf, gng], in_specs,
                             512, [(GLA_HEADS, GLA_DK, GLA_DV)], bt)


def _rw_dec_kernel(p_ref, sh_ref, s_ref, mu_ref, rp_ref, w2_ref, a2_ref,
                   o_ref, shn_ref, sn_ref, o_scr):
    bt = p_ref.shape[0]
    seg_ones = _seg_ones()
    xs = p_ref[:, 0:RW_SHIFT]
    shn_ref[...] = xs
    xm = xs + (sh_ref[...] - xs) * mu_ref[...]
    r, k, v, kk, b, logw = _rw_features(xm, rp_ref, w2_ref, a2_ref, seg_ones)
    w = jnp.exp(logw)
    ones = jnp.ones((RW_N, RW_N), BF16)
    eye = _iota2((RW_N, RW_N), 0) == _iota2((RW_N, RW_N), 1)
    seqs = range(bt)

    def row_sums(a):
        hi = a.astype(BF16)
        lo = (a - hi.astype(F32)).astype(BF16)
        return (jnp.dot(hi, ones, preferred_element_type=F32) + jnp.dot(lo, ones, preferred_element_type=F32))

    for hh in range(RW_HEADS):
        sl = slice(hh * RW_N, (hh + 1) * RW_N)
        s_old = [s_ref[bi, hh] for bi in seqs]
        red = row_sums(jnp.concatenate([s_old[bi] * kk[bi:bi + 1, sl] for bi in seqs]
                                       + [jnp.where(eye, v[bi:bi + 1, sl], 0.0) for bi in seqs], axis=0))
        s_new = []
        for bi in seqs:
            row = slice(bi, bi + 1)
            skk = red[bi * RW_N:(bi + 1) * RW_N]
            v_col = red[(bt + bi) * RW_N:(bt + bi + 1) * RW_N]
            s_new.append(s_old[bi] * w[row, sl] - skk * b[row, sl] + v_col * k[row, sl])
            sn_ref[bi, hh] = s_new[bi]
        red = row_sums(jnp.concatenate([s_new[bi] * r[bi:bi + 1, sl] for bi in seqs], axis=0))
        for bi in seqs:
            o_scr[bi:bi + 1, sl] = jnp.sum(jnp.where(eye, red[bi * RW_N:(bi + 1) * RW_N], 0.0),
                                           axis=0, keepdims=True)
    rz = p_ref[:, RW_SHIFT:RW_SEG]
    o_ref[...] = _rw_finish(o_scr[...], r, k, v, rz, rp_ref, seg_ones)


def _rw_dec(p, sh_all, s_all, mu, rp, w2, a2, l, prev, bt=8):
    in_specs = [pl.BlockSpec((bt, RW_SEG), lambda i: (i, 0)),
                pl.BlockSpec((None, bt, RW_SHIFT), lambda i: (l, i, 0)),
                pl.BlockSpec((None, bt, RW_HEADS, RW_N, RW_N), lambda i: (l, i, 0, 0, 0)),
                pl.BlockSpec((1, RW_SHIFT), lambda i: (0, 0)),
                pl.BlockSpec((8, RW_C), lambda i: (0, 0)),
                pl.BlockSpec((LANE, RW_C), lambda i: (0, 0)),
                pl.BlockSpec((LANE, RW_C), lambda i: (0, 0))]
    return _layer_state_call(_rw_dec_kernel, "rw_dec", l, prev, [p, sh_all, s_all, mu, rp, w2, a2], in_specs,
                             RW_C, [(RW_SHIFT,), (RW_HEADS, RW_N, RW_N)], bt)


def _layer_params(l, norm_g, w_in, dn_conv_w, dn_a_log, dn_dt_bias, dn_norm_g, gla_wf, gla_bf,
                  gla_norm_g, rw_mu, rw_w0, rw_w2, rw_a0, rw_a2, rw_k_k, rw_k_a, rw_r_k,
                  rw_ln_w, rw_ln_b, w_out):
    w = w_in[l].astype(BF16)
    gp = jnp.zeros((2, LANE), F32)
    gp = gp.at[0, DN_HEADS:2 * DN_HEADS].set(dn_a_log[l]).at[1, DN_HEADS:2 * DN_HEADS].set(dn_dt_bias[l])
    zeros_r = jnp.zeros((1, RW_C), F32)
    rp = jnp.concatenate([rw_w0[l][None], rw_a0[l][None], rw_k_k[l][None], rw_k_a[l][None],
                          rw_r_k[l].reshape(1, RW_C), rw_ln_w[l][None], rw_ln_b[l][None], zeros_r], axis=0)
    return dict(
        ng=norm_g[l][None],
        w_dn=w[:, 0:DN_SEG],
        w_gla=w[:, GLA_WIN0:GLA_WIN0 + GLA_WIN],
        w_rw=jnp.pad(w[:, RW_WIN0:], ((0, 0), (0, RW_WIN - (D_PROJ - RW_WIN0)))),
        cw=dn_conv_w[l], gp=gp, dng=dn_norm_g[l][None],
        wf=jnp.pad(gla_wf[l], ((0, LANE - GLA_RANK), (0, 0))).astype(BF16),
        bf=gla_bf[l][None], gng=gla_norm_g[l][None],
        mu=rw_mu[l][None], rp=rp,
        w2=jnp.pad(rw_w2[l], ((0, LANE - 64), (0, 0))).astype(BF16),
        a2=jnp.pad(rw_a2[l], ((64, 0), (0, 0))).astype(BF16),
        w_out=w_out[l].astype(BF16))


def _gla_state_from_pairs(st):
    heads = []
    for hh in range(GLA_HEADS):
        p, hl = hh // 2, hh % 2
        blk = st[:, p, hl * GLA_DV:(hl + 1) * GLA_DV, hl * GLA_DK:(hl + 1) * GLA_DK]
        heads.append(jnp.swapaxes(blk, 1, 2))
    return jnp.stack(heads, axis=1)


def _rw_state_from_groups(s):
    heads = []
    for hh in range(RW_HEADS):
        g, hl = hh // RW_GROUP, hh % RW_GROUP
        heads.append(s[:, g, hl * RW_N:(hl + 1) * RW_N, hl * RW_N:(hl + 1) * RW_N])
    return jnp.stack(heads, axis=1)


def _prompt_tiles(t):
    tb = 256 if t % 256 == 0 else CHUNK
    tm = 512 if t % 512 == 0 else tb
    return tb, tm


def _prompt_trunk(x, mods, layers, final_g):
    b, t, _ = x.shape
    tb, tm = _prompt_tiles(t)
    states = []
    n_layers = len(layers)
    for l, lp in enumerate(layers):
        mod = mods[l]
        o_dn, conv_new, dn_new = _dn_prompt(x, mod, lp["ng"], lp["w_dn"], lp["cw"], lp["gp"], lp["dng"], tb)
        o_gla, gla_st = _gla_prompt(x, mod, lp["ng"], lp["w_gla"], lp["wf"], lp["bf"], lp["gng"], tb)
        o_rw, rs_new, rw_st = _rw_prompt(x, mod, lp["ng"], lp["w_rw"], lp["mu"], lp["rp"], lp["w2"], lp["a2"], tb)
        x = _outproj(o_dn, o_gla, o_rw, x, mod[:, 2:3, :], lp["w_out"], final_g, l == n_layers - 1, tm)
        states.append((conv_new, dn_new, _gla_state_from_pairs(gla_st), rs_new[:, 0, :],
                       _rw_state_from_groups(rw_st)))
    return x, states


def _sample_trunk(x, mods, layers, final_g, conv_s, dn_s, gla_s, rs_s, rw_s):
    n = x.shape[0]
    x = x.reshape(n, D_MODEL)
    st_dn = st_gla = st_rw = None
    n_layers = len(layers)
    for l, lp in enumerate(layers):
        shift, scale, gate = mods[l][:, 0, :], mods[l][:, 1, :], mods[l][:, 2, :]
        p_dn = _dec_proj(x, shift, scale, lp["ng"], lp["w_dn"], 0, DN_SEG)
        p_gla = _dec_proj(x, shift, scale, lp["ng"], lp["w_gla"], GLA_ROLL, GLA_SEG)
        p_rw = _dec_proj(x, shift, scale, lp["ng"], lp["w_rw"], RW_ROLL, RW_SEG)
        o_dn, st_dn = _dn_dec(p_dn, conv_s, dn_s, lp["cw"], lp["gp"], lp["dng"], l, st_dn)
        o_gla, st_gla = _gla_dec(p_gla, gla_s, lp["wf"], lp["bf"], lp["gng"], l, st_gla)
        o_rw, st_rw = _rw_dec(p_rw, rs_s, rw_s, lp["mu"], lp["rp"], lp["w2"], lp["a2"], l, st_rw)
        x = _outproj(o_dn[None], o_gla[None], o_rw[None], x[None], gate[None], lp["w_out"], final_g,
                     l == n_layers - 1, n)[0]
    return x.reshape(n, 1, D_MODEL), (st_dn[0], st_dn[1], st_gla[0], st_rw[0], st_rw[1])


def kernel(x_prompt, x_sample, c_prompt, c_sample, state_dn_conv, state_dn, state_gla, state_rwkv_shift, state_rwkv, norm_g, ada_w, ada_b, w_in, dn_conv_w, dn_a_log, dn_dt_bias, dn_norm_g, gla_wf, gla_bf, gla_norm_g, rw_mu, rw_w0, rw_w2, rw_a0, rw_a2, rw_k_k, rw_k_a, rw_r_k, rw_ln_w, rw_ln_b, w_out, final_norm_g):
    depth = w_in.shape[0]
    bp = x_prompt.shape[0]
    layers = [_layer_params(l, norm_g, w_in, dn_conv_w, dn_a_log, dn_dt_bias, dn_norm_g, gla_wf, gla_bf,
                            gla_norm_g, rw_mu, rw_w0, rw_w2, rw_a0, rw_a2, rw_k_k, rw_k_a, rw_r_k,
                            rw_ln_w, rw_ln_b, w_out) for l in range(depth)]
    final_g = final_norm_g[None]
    mod_all = _modulation(jnp.concatenate([c_prompt, c_sample], axis=0), ada_w, ada_b)
    mod_all = mod_all.reshape(depth, -1, 3, D_MODEL)
    mods_p = [mod_all[l, :bp] for l in range(depth)]
    mods_s = [mod_all[l, bp:] for l in range(depth)]

    y_p, st_p = _prompt_trunk(x_prompt, mods_p, layers, final_g)
    y_s, st_s = _sample_trunk(x_sample, mods_s, layers, final_g, state_dn_conv, state_dn, state_gla,
                              state_rwkv_shift, state_rwkv)
    stack = lambda sts, i: jnp.stack([s[i] for s in sts])
    return (y_p, y_s,
            stack(st_p, 0), stack(st_p, 1), stack(st_p, 2), stack(st_p, 3), stack(st_p, 4),
            *st_s)
```

```python
import functools

import jax
import jax.numpy as jnp
from jax import lax
from jax.experimental import pallas as pl
from jax.experimental.pallas import tpu as pltpu

F32 = jnp.float32
BF16 = jnp.bfloat16

D_MODEL = 1024
EPS = 1e-6
CHUNK = 64
SUB = 16
NEG = -1e30

DN_HEADS, DN_DK, DN_DV = 4, 128, 128
GLA_HEADS, GLA_DK, GLA_DV, GLA_RANK, GLA_TAU = 4, 64, 128, 16, 16.0
RW_HEADS, RW_N = 8, 64
RW_GROUP = 4
RW_LN_EPS = 64e-5
DN_QKV = 1536
RW_C = 512
RW_SHIFT = 1664
DN_SEG, GLA_SEG, RW_SEG = 2176, 1664, 2176
DN_PROJ, GLA_PROJ, D_PROJ = 2056, 1552, 5784
GLA_WIN0, RW_WIN0 = (DN_PROJ // 128) * 128, ((DN_PROJ + GLA_PROJ) // 128) * 128
GLA_ROLL, RW_ROLL = DN_PROJ - GLA_WIN0, DN_PROJ + GLA_PROJ - RW_WIN0
GLA_WIN, RW_WIN = GLA_SEG + 128, RW_SEG + 128
LANE = 128
VMEM_LIMIT = 56 * 1024 * 1024


def _dot(a, b):
    return jnp.dot(a.astype(BF16), b.astype(BF16), preferred_element_type=F32)


def _dot_nt(a, b):
    return lax.dot_general(a.astype(BF16), b.astype(BF16), (((1,), (1,)), ((), ())),
                           preferred_element_type=F32)


def _dot_tn(a, b):
    return lax.dot_general(a.astype(BF16), b.astype(BF16), (((0,), (0,)), ((), ())),
                           preferred_element_type=F32)


def _sigmoid(x):
    return jax.nn.sigmoid(x)


def _silu(x):
    return x * jax.nn.sigmoid(x)


def _softplus(x):
    return jnp.maximum(x, 0.0) + jnp.log1p(jnp.exp(-jnp.abs(x)))


def _iota2(shape, dim):
    return lax.broadcasted_iota(jnp.int32, shape, dim)


def _modulated_norm(x, ng, shift, scale):
    ms = jnp.mean(x * x, axis=-1, keepdims=True)
    return x * lax.rsqrt(ms + EPS) * ng * (1.0 + scale) + shift


def _l2norm_rows(x):
    return x * lax.rsqrt(jnp.sum(x * x, axis=-1, keepdims=True) + EPS)


def _rms_rows(x):
    return x * lax.rsqrt(jnp.mean(x * x, axis=-1, keepdims=True) + EPS)


def _seg_sum(x, seg_ones):
    rows, width = x.shape
    hi = x.astype(BF16)
    lo = (x - hi.astype(F32)).astype(BF16)
    out = []
    for j in range(width // seg_ones.shape[0]):
        sl = slice(j * seg_ones.shape[0], (j + 1) * seg_ones.shape[0])
        t = jnp.dot(jnp.concatenate([hi[:, sl], lo[:, sl]], axis=0), seg_ones, preferred_element_type=F32)
        out.append(t[0:rows] + t[rows:])
    return jnp.concatenate(out, axis=1)


def _split3(x):
    hi = x.astype(BF16)
    rest = x - hi.astype(F32)
    mid = rest.astype(BF16)
    return hi, mid, (rest - mid.astype(F32)).astype(BF16)


def _sum_dot(sel01, x):
    return jnp.dot(jnp.concatenate([sel01] * 3, axis=1), jnp.concatenate(_split3(x), axis=0),
                   preferred_element_type=F32)


def _chunk_tri(n):
    r, c = _iota2((n, n), 0), _iota2((n, n), 1)
    return (((r >> 6) == (c >> 6)) & (r >= c)).astype(BF16)


def _bd_masks(n=CHUNK):
    r, c = _iota2((n, n), 0), _iota2((n, n), 1)
    same = (r >> 6) == (c >> 6)
    return dict(
        causal=same & (r >= c), strict=same & (r > c), eye=(r == c).astype(F32), same=same,
        bd16=(r >> 4) == (c >> 4),
        off32=((r >> 5) == (c >> 5)) & ((r >> 4) != (c >> 4)),
        off64=same & ((r >> 5) != (c >> 5)))


def _expand_rows(a, n_heads, shift):
    head = _iota2(a.shape, 1) >> shift
    zero = jnp.zeros_like(a)
    return jnp.concatenate([jnp.where(head == hh, a, zero) for hh in range(n_heads)], axis=0)


def _sbs_masks(n_heads):
    shape = (CHUNK, n_heads * CHUNK)
    r, c = _iota2(shape, 0), _iota2(shape, 1) & (CHUNK - 1)
    return dict(
        causal=r >= c, strict=r > c, eye=(r == c).astype(F32),
        bd16=(r >> 4) == (c >> 4),
        off32=((r >> 5) == (c >> 5)) & ((r >> 4) != (c >> 4)),
        off64=(r >> 5) != (c >> 5))


def _unit_lower_inverse(Ls, m, n_heads):
    bd = lambda a: _expand_rows(a.astype(BF16), n_heads, 6)
    n = range(len(Ls))
    Ld = [jnp.where(m["bd16"], L, 0.0) for L in Ls]
    X = [m["eye"] - a for a in Ld]
    P = [_dot(a, bd(a)) for a in Ld]
    for _ in range(2):
        t = [_dot(jnp.concatenate([P[i], X[i]], axis=0), bd(P[i])) for i in n]
        P = [a[0:CHUNK] for a in t]
        X = [X[i] + t[i][CHUNK:] for i in n]
    X = [X[i] + _dot(X[i], bd(P[i])) for i in n]
    for off in ("off32", "off64"):
        Y = [_dot(X[i], bd(jnp.where(m[off], Ls[i], 0.0))) for i in n]
        X = [X[i] - _dot(Y[i], bd(X[i])) for i in n]
    return X


def _shift_lanes_left(x, shift):
    return pltpu.roll(x, x.shape[1] - shift, 1) if shift else x


def _load_weight_window(w_ref, w_s, shift):
    width = w_s.shape[1]
    for r0 in range(0, w_ref.shape[0], 256):
        blk = _shift_lanes_left(w_ref[r0:r0 + 256, :].astype(F32), shift)
        w_s[r0:r0 + 256, :] = blk[:, 0:width].astype(BF16)


def _col_from_row(row, n, lane_off=0):
    w = row.shape[1]
    sel = (_iota2((n, w), 0) + lane_off) == _iota2((n, w), 1)
    return jnp.sum(jnp.where(sel, row, 0.0), axis=1, keepdims=True)


def _params(n_axes):
    return pltpu.CompilerParams(dimension_semantics=("arbitrary",) * n_axes,
                                vmem_limit_bytes=VMEM_LIMIT)


def _mod_kernel(c_ref, w_ref, b_ref, o_ref):
    o_ref[0] = _dot(_silu(c_ref[...]), w_ref[0]) + b_ref[0]


def _modulation(c_all, ada_w, ada_b):
    depth, d, n = ada_w.shape
    rows = c_all.shape[0]
    tn = 1024
    return pl.pallas_call(
        _mod_kernel,
        grid=(depth, n // tn),
        in_specs=[pl.BlockSpec((rows, d), lambda l, j: (0, 0)),
                  pl.BlockSpec((1, d, tn), lambda l, j: (l, 0, j)),
                  pl.BlockSpec((1, 1, tn), lambda l, j: (l, 0, j))],
        out_specs=pl.BlockSpec((1, rows, tn), lambda l, j: (l, 0, j)),
        out_shape=jax.ShapeDtypeStruct((depth, rows, n), F32),
        compiler_params=_params(2),
        name="adaln_mod",
    )(c_all, ada_w, ada_b.reshape(depth, 1, n))


def _dn_prompt_kernel(x_ref, mod_ref, ng_ref, w_ref, cw_ref, gp_ref, dng_ref,
                      o_ref, conv_ref, s_ref,
                      pbuf, qs, ks, vs, gb, gcs, os_, U_s, W_s, Q_s, A_s, K_s, E_s, S):
    t = pl.program_id(1)
    tb = x_ref.shape[1]
    nb = DN_HEADS * CHUNK
    m = _sbs_masks(DN_HEADS)
    lane_c = _iota2((CHUNK, LANE), 1)
    sbs_head = _iota2((CHUNK, nb), 1) >> 6
    gate_lanes = ((_iota2((8, LANE), 1) >= DN_HEADS) & (_iota2((8, LANE), 1) < 2 * DN_HEADS)).astype(BF16)
    heads = range(DN_HEADS)

    def stack(f):
        return jnp.concatenate([f(hh) for hh in heads], axis=0)

    def stack_heads(a):
        return stack(lambda hh: a[:, hh * DN_DK:(hh + 1) * DN_DK])

    def per_head_lanes(col_of):
        return jnp.concatenate([jnp.broadcast_to(col_of(hh), (CHUNK, DN_DK)) for hh in heads], axis=1)

    @pl.when(t == 0)
    def _():
        pbuf[0:8, :] = jnp.zeros((8, DN_SEG), F32)
        S[...] = jnp.zeros_like(S)

    h = _modulated_norm(x_ref[0], ng_ref[...], mod_ref[0, 0:1, :], mod_ref[0, 1:2, :])
    pbuf[8:8 + tb, :] = _dot(h, w_ref[...])

    conv = pbuf[5:5 + tb, 0:DN_QKV] * cw_ref[0:1, :]
    for j in range(1, 4):
        conv = conv + pbuf[5 + j:5 + j + tb, 0:DN_QKV] * cw_ref[j:j + 1, :]
    conv = _silu(conv)
    for hh in range(DN_HEADS):
        lo = hh * DN_DK
        qs[:, lo:lo + DN_DK] = _l2norm_rows(conv[:, lo:lo + DN_DK]) * (DN_DK ** -0.5)
        ks[:, lo:lo + DN_DK] = _l2norm_rows(conv[:, 512 + lo:512 + lo + DN_DK])
    vs[...] = conv[:, 1024:DN_QKV]
    gseg = pbuf[8:8 + tb, 2048:DN_SEG]
    lane = _iota2((tb, LANE), 1)
    gb[...] = jnp.where(lane < DN_HEADS, _sigmoid(gseg),
                        -jnp.exp(gp_ref[0:1, :]) * _softplus(gseg + gp_ref[1:2, :]))

    gcs[...] = _sum_dot(_chunk_tri(tb), gb[...])

    def chunk_matrices(c):
        rows = pl.ds(c * CHUNK, CHUNK)
        gbc = gb[rows, :]
        gc = gcs[rows, :]
        g_of = lambda hh: gc[:, DN_HEADS + hh:DN_HEADS + hh + 1]
        g_col = jnp.zeros((CHUNK, nb), F32)
        for hh in heads:
            g_col = jnp.where(sbs_head == hh, g_of(hh), g_col)
        g_sel = stack(lambda hh: jnp.where(lane_c == DN_HEADS + hh, gc, 0.0))
        g_row = lax.dot_general(jnp.concatenate([gate_lanes] * 3, axis=1),
                                jnp.concatenate(_split3(g_sel), axis=1),
                                (((1,), (1,)), ((), ())), preferred_element_type=F32)[0:1, :]
        decay = jnp.exp(jnp.where(m["causal"], g_col - g_row, NEG))
        k_all = ks[rows, :]
        q_all = qs[rows, :]
        beta_l = per_head_lanes(lambda hh: gbc[:, hh:hh + 1])
        g_l = per_head_lanes(g_of)
        eg_l = jnp.exp(g_l)
        kb_all = k_all * beta_l
        expand = lambda a: _expand_rows(a.astype(BF16), DN_HEADS, 7)
        kq = _dot_nt(jnp.concatenate([kb_all, q_all], axis=0), expand(k_all))
        L = jnp.where(m["strict"], kq[0:CHUNK] * decay, 0.0)
        A_s[c] = _expand_rows((kq[CHUNK:2 * CHUNK] * decay).astype(BF16), DN_HEADS, 6)
        Q_s[c] = stack_heads(q_all * eg_l).astype(BF16)
        K_s[c] = stack_heads(k_all * jnp.exp(g_l[CHUNK - 1:CHUNK, :] - g_l)).astype(BF16)
        E_s[c] = jnp.exp(gc[CHUNK - 8:CHUNK, :])
        return L, expand(vs[rows, :] * beta_l), expand(kb_all * eg_l)

    n_chunks = tb // CHUNK
    pre = [chunk_matrices(c) for c in range(n_chunks)]
    inv = _unit_lower_inverse([p[0] for p in pre], m, DN_HEADS)
    for c in range(n_chunks):
        X = inv[c].astype(BF16)
        U_s[c] = stack_heads(_dot(X, pre[c][1]))
        W_s[c] = stack_heads(_dot(X, pre[c][2])).astype(BF16)

    def recurrence(c, carry):
        rows = pl.ds(pl.multiple_of(c * CHUNK, CHUNK), CHUNK)
        w, qe, kh = W_s[c], Q_s[c], K_s[c]
        e_last = E_s[c][7:8, :]
        ws = []
        for hh in heads:
            sl = slice(hh * CHUNK, (hh + 1) * CHUNK)
            ws.append(jnp.dot(jnp.concatenate([w[sl], qe[sl]], axis=0), S[hh].astype(BF16),
                              preferred_element_type=F32))
        v_new = U_s[c] - jnp.concatenate([x[0:CHUNK] for x in ws], axis=0)
        vb = v_new.astype(BF16)
        o_st = jnp.concatenate([x[CHUNK:2 * CHUNK] for x in ws], axis=0) + jnp.dot(
            A_s[c], vb, preferred_element_type=F32)
        for hh in heads:
            sl = slice(hh * CHUNK, (hh + 1) * CHUNK)
            S[hh] = S[hh] * e_last[:, DN_HEADS + hh:DN_HEADS + hh + 1] + _dot_tn(kh[sl], vb[sl])
            os_[rows, hh * DN_DV:(hh + 1) * DN_DV] = o_st[sl]
        return carry

    lax.fori_loop(0, n_chunks, recurrence, 0, unroll=True)

    for hh in heads:
        lo = hh * DN_DV
        z = pbuf[8:8 + tb, DN_QKV + lo:DN_QKV + lo + DN_DV]
        o_ref[0, :, lo:lo + DN_DV] = (_rms_rows(os_[:, lo:lo + DN_DV]) * dng_ref[...] * _silu(z)).astype(o_ref.dtype)

    pbuf[0:8, :] = pbuf[tb:tb + 8, :]

    @pl.when(t == pl.num_programs(1) - 1)
    def _():
        conv_ref[0] = pbuf[tb + 5:tb + 8, 0:DN_QKV]
        s_ref[0] = S[...]


def _dn_prompt(x, mod, ng, w_seg, cw, gp, dng, tb):
    b, t, d = x.shape
    nch, nb = tb // CHUNK, DN_HEADS * CHUNK
    return pl.pallas_call(
        _dn_prompt_kernel,
        grid=(b, t // tb),
        in_specs=[pl.BlockSpec((1, tb, d), lambda i, j: (i, j, 0)),
                  pl.BlockSpec((1, 3, d), lambda i, j: (i, 0, 0)),
                  pl.BlockSpec((1, d), lambda i, j: (0, 0)),
                  pl.BlockSpec((d, DN_SEG), lambda i, j: (0, 0)),
                  pl.BlockSpec((4, DN_QKV), lambda i, j: (0, 0)),
                  pl.BlockSpec((2, LANE), lambda i, j: (0, 0)),
                  pl.BlockSpec((1, DN_DV), lambda i, j: (0, 0))],
        out_specs=[pl.BlockSpec((1, tb, 512), lambda i, j: (i, j, 0)),
                   pl.BlockSpec((1, 3, DN_QKV), lambda i, j: (i, 0, 0)),
                   pl.BlockSpec((1, DN_HEADS, DN_DK, DN_DV), lambda i, j: (i, 0, 0, 0))],
        out_shape=[jax.ShapeDtypeStruct((b, t, 512), BF16),
                   jax.ShapeDtypeStruct((b, 3, DN_QKV), F32),
                   jax.ShapeDtypeStruct((b, DN_HEADS, DN_DK, DN_DV), F32)],
        scratch_shapes=[pltpu.VMEM((tb + 8, DN_SEG), F32),
                        pltpu.VMEM((tb, 512), F32), pltpu.VMEM((tb, 512), F32),
                        pltpu.VMEM((tb, 512), F32), pltpu.VMEM((tb, LANE), F32),
                        pltpu.VMEM((tb, LANE), F32), pltpu.VMEM((tb, 512), F32),
                        pltpu.VMEM((nch, nb, DN_DV), F32), pltpu.VMEM((nch, nb, DN_DV), BF16),
                        pltpu.VMEM((nch, nb, DN_DK), BF16), pltpu.VMEM((nch, nb, nb), BF16),
                        pltpu.VMEM((nch, nb, DN_DK), BF16), pltpu.VMEM((nch, 8, LANE), F32),
                        pltpu.VMEM((DN_HEADS, DN_DK, DN_DV), F32)],
        compiler_params=_params(2),
        name="dn_prompt",
    )(x, mod, ng, w_seg, cw, gp, dng)


def _gla_prompt_kernel(x_ref, mod_ref, ng_ref, w_ref, wf_ref, bf_ref, gng_ref,
                       o_ref, s_ref,
                       w_s, pbuf, lfs, os_, QE_s, KH_s, EG_s, ST):
    t = pl.program_id(1)
    tb = x_ref.shape[1]
    m = _bd_masks()

    @pl.when((pl.program_id(0) == 0) & (t == 0))
    def _():
        _load_weight_window(w_ref, w_s, GLA_ROLL)

    n_sub = CHUNK // SUB
    lane = _iota2((CHUNK, LANE), 1)
    head1 = lane >= GLA_DK
    rowblk = _iota2((CHUNK, LANE), 0) >> 4
    rowloc = _iota2((CHUNK, LANE), 0) & (SUB - 1)
    er = _iota2((2 * SUB, SUB * LANE), 0)
    ec = _iota2((2 * SUB, SUB * LANE), 1)
    eblk = (((ec >> 7) == (er & (SUB - 1))) & (((ec & (LANE - 1)) >= GLA_DK) == (er >= SUB))).astype(BF16)
    st_r = _iota2((2 * GLA_DV, LANE), 0)
    st_c = _iota2((2 * GLA_DV, LANE), 1)
    st_mask = (st_r >= GLA_DV) == (st_c >= GLA_DK)

    @pl.when(t == 0)
    def _():
        ST[...] = jnp.zeros_like(ST)

    h = _modulated_norm(x_ref[0], ng_ref[...], mod_ref[0, 0:1, :], mod_ref[0, 1:2, :])
    pbuf[...] = _dot(h, w_s[...])
    pre = _dot(pbuf[:, 1536:GLA_SEG], wf_ref[...]) + bf_ref[...]
    lfs[...] = (jnp.minimum(pre, 0.0) - jnp.log1p(jnp.exp(-jnp.abs(pre)))) * (1.0 / GLA_TAU)

    lfs[...] = _sum_dot(_chunk_tri(tb), lfs[...])

    def intra_matrices(c, p):
        rows = pl.ds(c * CHUNK, CHUNK)
        lo = p * LANE
        G = lfs[rows, lo:lo + LANE]
        q = pbuf[rows, lo:lo + LANE] * (GLA_DK ** -0.5)
        k = pbuf[rows, 256 + lo:256 + lo + LANE]
        G3 = G.reshape(n_sub, SUB, LANE)
        q3 = q.reshape(n_sub, SUB, LANE)

        def pick(a3, i):
            return jnp.broadcast_to(a3[:, i:i + 1, :], (n_sub, SUB, LANE)).reshape(CHUNK, LANE)

        q_t = q * jnp.exp(G - pick(G3, 0))
        q_parts, k_parts = [], []
        for i in range(1, n_sub):
            q_parts.append(jnp.where(rowblk == i, q_t, 0.0))
            ref_i = G[i * SUB:i * SUB + 1, :]
            k_parts.append(k * jnp.exp(jnp.where(rowblk < i, ref_i - G, NEG)))
        q_big = jnp.concatenate(q_parts, axis=1)
        k_big = jnp.concatenate(k_parts, axis=1)
        head1_big = jnp.concatenate([head1] * (n_sub - 1), axis=1)
        q_both = jnp.concatenate([jnp.where(head1_big, 0.0, q_big),
                                  jnp.where(head1_big, q_big, 0.0)], axis=0)
        a_off = _dot_nt(q_both, k_big)

        slabs = []
        for i in range(SUB):
            dec = jnp.exp(jnp.where(rowloc <= i, pick(G3, i) - G, NEG))
            slabs.append((k * pick(q3, i) * dec).astype(BF16))
        a_diag = lax.dot_general(eblk, jnp.concatenate(slabs, axis=1), (((1,), (1,)), ((), ())),
                                 preferred_element_type=F32)

        g_end = G[CHUNK - 1:CHUNK, :]
        QE_s[c, p] = (q * jnp.exp(G)).astype(BF16)
        KH_s[c, p] = (k * jnp.exp(g_end - G)).astype(BF16)
        EG_s[c, p] = jnp.exp(G[CHUNK - 8:CHUNK, :])
        return a_off, a_diag

    def intra_output(c, p, a_off, a_diag):
        rows = pl.ds(c * CHUNK, CHUNK)
        for hl in range(2):
            hh = 2 * p + hl
            a_d = a_diag[hl * SUB:(hl + 1) * SUB, :]
            a_h = a_off[hl * CHUNK:(hl + 1) * CHUNK, :] + jnp.where(
                m["bd16"], jnp.concatenate([a_d] * n_sub, axis=0), 0.0)
            os_[rows, hh * GLA_DV:(hh + 1) * GLA_DV] = _dot(a_h, pbuf[rows, 512 + hh * GLA_DV:512 + (hh + 1) * GLA_DV])

    n_chunks = tb // CHUNK
    problems = [(c, p) for c in range(n_chunks) for p in range(GLA_HEADS // 2)]
    mats = [intra_matrices(c, p) for c, p in problems]
    for (c, p), ab in zip(problems, mats):
        intra_output(c, p, *ab)

    def recurrence(c, carry):
        rows = pl.ds(pl.multiple_of(c * CHUNK, CHUNK), CHUNK)
        pairs = range(GLA_HEADS // 2)
        st = [ST[p] for p in pairs]
        o_inter = [_dot_nt(QE_s[c, p], st[p]) for p in pairs]
        kv = [_dot_tn(pbuf[rows, 512 + 2 * p * LANE:512 + 2 * (p + 1) * LANE], KH_s[c, p]) for p in pairs]
        for p in pairs:
            ST[p] = st[p] * EG_s[c, p][7:8, :] + jnp.where(st_mask, kv[p], 0.0)
            sl = slice(2 * p * LANE, 2 * (p + 1) * LANE)
            os_[rows, sl] = os_[rows, sl] + o_inter[p]
        return carry

    lax.fori_loop(0, n_chunks, recurrence, 0, unroll=True)

    for hh in range(GLA_HEADS):
        sl = slice(hh * GLA_DV, (hh + 1) * GLA_DV)
        z = pbuf[:, 1024 + hh * GLA_DV:1024 + (hh + 1) * GLA_DV]
        o_ref[0, :, sl] = (_rms_rows(os_[:, sl]) * gng_ref[...] * _silu(z)).astype(o_ref.dtype)

    @pl.when(t == pl.num_programs(1) - 1)
    def _():
        s_ref[0] = ST[...]


def _gla_prompt(x, mod, ng, w_seg, wf, bf, gng, tb):
    b, t, d = x.shape
    return pl.pallas_call(
        _gla_prompt_kernel,
        grid=(b, t // tb),
        in_specs=[pl.BlockSpec((1, tb, d), lambda i, j: (i, j, 0)),
                  pl.BlockSpec((1, 3, d), lambda i, j: (i, 0, 0)),
                  pl.BlockSpec((1, d), lambda i, j: (0, 0)),
                  pl.BlockSpec((d, GLA_WIN), lambda i, j: (0, 0)),
                  pl.BlockSpec((LANE, 256), lambda i, j: (0, 0)),
                  pl.BlockSpec((1, 256), lambda i, j: (0, 0)),
                  pl.BlockSpec((1, GLA_DV), lambda i, j: (0, 0))],
        out_specs=[pl.BlockSpec((1, tb, 512), lambda i, j: (i, j, 0)),
                   pl.BlockSpec((1, 2, 2 * GLA_DV, LANE), lambda i, j: (i, 0, 0, 0))],
        out_shape=[jax.ShapeDtypeStruct((b, t, 512), BF16),
                   jax.ShapeDtypeStruct((b, 2, 2 * GLA_DV, LANE), F32)],
        scratch_shapes=[pltpu.VMEM((d, GLA_SEG), BF16),
                        pltpu.VMEM((tb, GLA_SEG), F32), pltpu.VMEM((tb, 256), F32),
                        pltpu.VMEM((tb, 512), F32),
                        pltpu.VMEM((tb // CHUNK, 2, CHUNK, LANE), BF16),
                        pltpu.VMEM((tb // CHUNK, 2, CHUNK, LANE), BF16),
                        pltpu.VMEM((tb // CHUNK, 2, 8, LANE), F32),
                        pltpu.VMEM((2, 2 * GLA_DV, LANE), F32)],
        compiler_params=_params(2),
        name="gla_prompt",
    )(x, mod, ng, w_seg, wf, bf, gng)


def _rw_features(xm, rp_ref, w2_ref, a2_ref, seg_ones):
    r = xm[:, 0:RW_C]
    k = xm[:, RW_C:2 * RW_C]
    v = xm[:, 2 * RW_C:3 * RW_C]
    lo = xm[:, 3 * RW_C:RW_SHIFT]
    w_log = -_softplus(-(rp_ref[0:1, :] + _dot(jnp.tanh(lo), w2_ref[...]))) - 0.5
    logw = -jnp.exp(w_log)
    a = _sigmoid(rp_ref[1:2, :] + _dot(lo, a2_ref[...]))
    kk = k * rp_ref[2:3, :]
    kk = kk * lax.rsqrt(_seg_sum(kk * kk, seg_ones) + EPS)
    k = k * (1.0 + (a - 1.0) * rp_ref[3:4, :])
    return r, k, v, kk, kk * a, logw


def _rw_finish(o, r, k, v, rz, rp_ref, seg_ones):
    mu = _seg_sum(o, seg_ones) * (1.0 / RW_N)
    dev = o - mu
    var = _seg_sum(dev * dev, seg_ones) * (1.0 / RW_N)
    on = dev * lax.rsqrt(var + RW_LN_EPS) * rp_ref[5:6, :] + rp_ref[6:7, :]
    bonus = _seg_sum(r * k * rp_ref[4:5, :], seg_ones) * v
    return (on + bonus) * _silu(rz)


def _seg_ones():
    n = RW_GROUP * RW_N
    return ((_iota2((n, n), 0) >> 6) == (_iota2((n, n), 1) >> 6)).astype(BF16)


def _rw_prompt_kernel(x_ref, mod_ref, ng_ref, w_ref, mu_ref, rp_ref, w2_ref, a2_ref,
                      o_ref, sh_ref, s_ref,
                      w_s, pbuf, rs, ks, vs, kks, bs, lws, gs, os_,
                      P1_s, KT_s, P2_s, RH_s, AB_s, KB_s, E_s, S):
    t = pl.program_id(1)
    tb = x_ref.shape[1]
    gw = RW_GROUP * RW_N

    @pl.when((pl.program_id(0) == 0) & (t == 0))
    def _():
        _load_weight_window(w_ref, w_s, RW_ROLL)

    m = _sbs_masks(RW_GROUP)
    same_head = (_iota2((gw, gw), 0) >> 6) == (_iota2((gw, gw), 1) >> 6)
    seg_ones = _seg_ones()
    mid = CHUNK // 2

    def expand(a):
        return _expand_rows(a.astype(BF16), RW_GROUP, 6)

    @pl.when(t == 0)
    def _():
        pbuf[0:8, :] = jnp.zeros((8, RW_SEG), F32)
        S[...] = jnp.zeros_like(S)

    h = _modulated_norm(x_ref[0], ng_ref[...], mod_ref[0, 0:1, :], mod_ref[0, 1:2, :])
    pbuf[8:8 + tb, :] = _dot(h, w_s[...])
    xs = pbuf[8:8 + tb, 0:RW_SHIFT]
    prev = pbuf[7:7 + tb, 0:RW_SHIFT]
    xm = xs + (prev - xs) * mu_ref[...]
    r, k, v, kk, b, logw = _rw_features(xm, rp_ref, w2_ref, a2_ref, seg_ones)
    rs[...] = r
    ks[...] = k
    vs[...] = v
    kks[...] = kk
    bs[...] = b
    lws[...] = logw
    gs[...] = _sum_dot(_chunk_tri(tb), logw)

    def chunk_matrices(c, g):
        rows = pl.ds(c * CHUNK, CHUNK)
        sl = slice(g * gw, (g + 1) * gw)
        G = gs[rows, sl]
        Gx = G - lws[rows, sl]
        ref = G[mid:mid + 1, :]
        r_, k_, v_, kk_, b_ = rs[rows, sl], ks[rows, sl], vs[rows, sl], kks[rows, sl], bs[rows, sl]
        e_out = jnp.exp(ref - G)
        lhs = jnp.concatenate([kk_ * jnp.exp(Gx - ref), r_ * jnp.exp(G - ref)], axis=0)
        rhs = jnp.concatenate([expand(b_ * e_out), expand(k_ * e_out)], axis=0)
        mm = _dot_nt(lhs, rhs)
        Lb = jnp.where(m["strict"], mm[0:CHUNK, 0:gw], 0.0)
        Mk = jnp.where(m["strict"], mm[0:CHUNK, gw:], 0.0)
        Arb = jnp.where(m["causal"], mm[CHUNK:, 0:gw], 0.0)
        Ark = jnp.where(m["causal"], mm[CHUNK:, gw:], 0.0)
        kv = _dot(jnp.concatenate([Mk, Ark], axis=0), expand(v_))
        P2_s[c, g] = kv[CHUNK:]
        RH_s[c, g] = (r_ * jnp.exp(G)).astype(BF16)
        AB_s[c, g] = Arb.astype(BF16)
        e_end = jnp.exp(G[CHUNK - 1:CHUNK, :] - G)
        KB_s[c, g] = jnp.concatenate([k_ * e_end, -(b_ * e_end)], axis=0).astype(BF16)
        E_s[c, g] = jnp.exp(G[CHUNK - 8:CHUNK, :])
        return Lb, jnp.concatenate([expand(kv[0:CHUNK]), expand(kk_ * jnp.exp(Gx))], axis=1)

    n_chunks = tb // CHUNK
    problems = [(c, g) for c in range(n_chunks) for g in range(RW_HEADS // RW_GROUP)]
    pre = [chunk_matrices(c, g) for c, g in problems]
    inv = _unit_lower_inverse([p[0] for p in pre], m, RW_GROUP)
    for (c, g), X, p in zip(problems, inv, pre):
        sol = _dot(X, p[1])
        P1_s[c, g] = sol[:, 0:gw]
        KT_s[c, g] = sol[:, gw:].astype(BF16)

    def recurrence(c, carry):
        rows = pl.ds(pl.multiple_of(c * CHUNK, CHUNK), CHUNK)
        groups = range(RW_HEADS // RW_GROUP)
        sls = [slice(g * gw, (g + 1) * gw) for g in groups]
        s_old = [S[g] for g in groups]
        khu = [_dot_nt(jnp.concatenate([KT_s[c, g], RH_s[c, g]], axis=0), s_old[g]) for g in groups]
        u_ = [P1_s[c, g] + khu[g][0:CHUNK] for g in groups]
        au = [_dot(AB_s[c, g], expand(u_[g])) for g in groups]
        upd = [_dot_tn(jnp.concatenate([vs[rows, sls[g]], u_[g]], axis=0), KB_s[c, g]) for g in groups]
        for g in groups:
            S[g] = s_old[g] * E_s[c, g][7:8, :] + jnp.where(same_head, upd[g], 0.0)
            os_[rows, sls[g]] = khu[g][CHUNK:2 * CHUNK] + P2_s[c, g] - au[g]
        return carry

    lax.fori_loop(0, n_chunks, recurrence, 0, unroll=True)

    rz = pbuf[8:8 + tb, RW_SHIFT:RW_SEG]
    o_ref[0] = _rw_finish(os_[...], rs[...], ks[...], vs[...], rz, rp_ref, seg_ones).astype(o_ref.dtype)
    pbuf[0:8, :] = pbuf[tb:tb + 8, :]

    @pl.when(t == pl.num_programs(1) - 1)
    def _():
        sh_ref[0] = pbuf[tb + 7:tb + 8, 0:RW_SHIFT]
        s_ref[0] = S[...]


def _rw_prompt(x, mod, ng, w_seg, mu, rp, w2, a2, tb):
    b, t, d = x.shape
    nch, ng_, gw = tb // CHUNK, RW_HEADS // RW_GROUP, RW_GROUP * RW_N
    return pl.pallas_call(
        _rw_prompt_kernel,
        grid=(b, t // tb),
        in_specs=[pl.BlockSpec((1, tb, d), lambda i, j: (i, j, 0)),
                  pl.BlockSpec((1, 3, d), lambda i, j: (i, 0, 0)),
                  pl.BlockSpec((1, d), lambda i, j: (0, 0)),
                  pl.BlockSpec((d, RW_WIN), lambda i, j: (0, 0)),
                  pl.BlockSpec((1, RW_SHIFT), lambda i, j: (0, 0)),
                  pl.BlockSpec((8, RW_C), lambda i, j: (0, 0)),
                  pl.BlockSpec((LANE, RW_C), lambda i, j: (0, 0)),
                  pl.BlockSpec((LANE, RW_C), lambda i, j: (0, 0))],
        out_specs=[pl.BlockSpec((1, tb, RW_C), lambda i, j: (i, j, 0)),
                   pl.BlockSpec((1, 1, RW_SHIFT), lambda i, j: (i, 0, 0)),
                   pl.BlockSpec((1, ng_, gw, gw), lambda i, j: (i, 0, 0, 0))],
        out_shape=[jax.ShapeDtypeStruct((b, t, RW_C), BF16),
                   jax.ShapeDtypeStruct((b, 1, RW_SHIFT), F32),
                   jax.ShapeDtypeStruct((b, ng_, gw, gw), F32)],
        scratch_shapes=[pltpu.VMEM((d, RW_SEG), BF16), pltpu.VMEM((tb + 8, RW_SEG), F32)]
        + [pltpu.VMEM((tb, RW_C), F32) for _ in range(8)]
        + [pltpu.VMEM((nch, ng_, CHUNK, gw), F32), pltpu.VMEM((nch, ng_, CHUNK, gw), BF16),
           pltpu.VMEM((nch, ng_, CHUNK, gw), F32), pltpu.VMEM((nch, ng_, CHUNK, gw), BF16),
           pltpu.VMEM((nch, ng_, CHUNK, gw), BF16), pltpu.VMEM((nch, ng_, 2 * CHUNK, gw), BF16),
           pltpu.VMEM((nch, ng_, 8, gw), F32),
           pltpu.VMEM((ng_, gw, gw), F32)],
        compiler_params=_params(2),
        name="rw_prompt",
    )(x, mod, ng, w_seg, mu, rp, w2, a2)


def _outproj_kernel(od_ref, og_ref, or_ref, x_ref, gate_ref, w_ref, fg_ref, y_ref, *, final):
    acc = (_dot(od_ref[0], w_ref[0:512, :]) + _dot(og_ref[0], w_ref[512:1024, :])
           + _dot(or_ref[0], w_ref[1024:1536, :]))
    xn = x_ref[0] + gate_ref[0] * acc
    if final:
        xn = _rms_rows(xn) * fg_ref[...]
    y_ref[0] = xn


def _outproj(o_dn, o_gla, o_rw, x, gate, w_out, fg, final, tm):
    b, t, d = x.shape
    tg = gate.shape[1]
    gspec = (pl.BlockSpec((1, 1, d), lambda i, j: (i, 0, 0)) if tg == 1
             else pl.BlockSpec((1, tm, d), lambda i, j: (i, j, 0)))
    ospec = pl.BlockSpec((1, tm, 512), lambda i, j: (i, j, 0))
    return pl.pallas_call(
        functools.partial(_outproj_kernel, final=final),
        grid=(b, t // tm),
        in_specs=[ospec, ospec, ospec,
                  pl.BlockSpec((1, tm, d), lambda i, j: (i, j, 0)),
                  gspec,
                  pl.BlockSpec((1536, d), lambda i, j: (0, 0)),
                  pl.BlockSpec((1, d), lambda i, j: (0, 0))],
        out_specs=pl.BlockSpec((1, tm, d), lambda i, j: (i, j, 0)),
        out_shape=jax.ShapeDtypeStruct((b, t, d), F32),
        compiler_params=_params(2),
        name="out_proj",
    )(o_dn, o_gla, o_rw, x, gate, w_out, fg)


def _dec_proj_kernel(x_ref, shift_ref, scale_ref, ng_ref, w_ref, o_ref, *, roll):
    h = _modulated_norm(x_ref[...], ng_ref[...], shift_ref[...], scale_ref[...])
    o_ref[...] = _shift_lanes_left(_dot(h, w_ref[...]), roll)[:, 0:o_ref.shape[1]]


def _dec_proj(x, shift, scale, ng, w_win, roll, seg):
    n, d = x.shape
    win = w_win.shape[1]
    full = lambda shape: pl.BlockSpec(shape, lambda i: (0,) * len(shape))
    return pl.pallas_call(
        functools.partial(_dec_proj_kernel, roll=roll),
        grid=(1,),
        in_specs=[full((n, d)), full((n, d)), full((n, d)), full((1, d)), full((d, win))],
        out_specs=full((n, seg)),
        out_shape=jax.ShapeDtypeStruct((n, seg), F32),
        compiler_params=_params(1),
        name="dec_proj",
    )(x, shift, scale, ng, w_win)


def _dn_dec_kernel(p_ref, cst_ref, s_ref, cw_ref, gp_ref, dng_ref,
                   o_ref, cnew_ref, sn_ref, o_scr):
    bt = p_ref.shape[0]
    qkv = p_ref[:, 0:DN_QKV]
    c0, c1, c2 = cst_ref[:, 0, :], cst_ref[:, 1, :], cst_ref[:, 2, :]
    conv = _silu(c0 * cw_ref[0:1, :] + c1 * cw_ref[1:2, :] + c2 * cw_ref[2:3, :] + qkv * cw_ref[3:4, :])
    cnew_ref[:, 0, :] = c1
    cnew_ref[:, 1, :] = c2
    cnew_ref[:, 2, :] = qkv
    gseg = p_ref[:, 2048:DN_SEG]
    beta_all = _sigmoid(gseg)
    a_all = jnp.exp(-jnp.exp(gp_ref[0:1, :]) * _softplus(gseg + gp_ref[1:2, :]))
    for hh in range(DN_HEADS):
        lo = hh * DN_DK
        q = _l2norm_rows(conv[:, lo:lo + DN_DK]) * (DN_DK ** -0.5)
        k = _l2norm_rows(conv[:, 512 + lo:512 + lo + DN_DK])
        v = conv[:, 1024 + lo:1024 + lo + DN_DV]
        for bi in range(bt):
            k_col = _col_from_row(k[bi:bi + 1, :], DN_DK)
            q_col = _col_from_row(q[bi:bi + 1, :], DN_DK)
            a = a_all[bi:bi + 1, DN_HEADS + hh:DN_HEADS + hh + 1]
            beta = beta_all[bi:bi + 1, hh:hh + 1]
            s_old = s_ref[bi, hh]
            sk = jnp.sum(s_old * k_col, axis=0, keepdims=True)
            v_new = beta * (v[bi:bi + 1, :] - a * sk)
            s_new = s_old * a + k_col * v_new
            sn_ref[bi, hh] = s_new
            o_scr[bi:bi + 1, lo:lo + DN_DV] = jnp.sum(s_new * q_col, axis=0, keepdims=True)
    for hh in range(DN_HEADS):
        lo = hh * DN_DV
        z = p_ref[:, DN_QKV + lo:DN_QKV + lo + DN_DV]
        o_ref[:, lo:lo + DN_DV] = _rms_rows(o_scr[:, lo:lo + DN_DV]) * dng_ref[...] * _silu(z)


def _layer_state_call(kernel_fn, name, l, prev, args, in_specs, o_width, tails, bt):
    n = args[0].shape[0]
    depth = next(a.shape[0] for a in args if a.ndim >= 3)
    first = prev is None
    n_in = len(args)

    def spec(t):
        z = (0,) * len(t)
        if first:
            return pl.BlockSpec((depth, bt) + t, lambda i: (0, i) + z)
        return pl.BlockSpec((None, bt) + t, lambda i: (l, i) + z)

    def body(*refs):
        ins, rest = refs[:n_in], list(refs[n_in if first else n_in + len(tails):])
        if first:
            for j in range(1, 1 + len(tails)):
                if depth > 1:
                    rest[j][1:] = jnp.zeros((depth - 1,) + rest[j].shape[1:], F32)
                rest[j] = rest[j].at[0]
        kernel_fn(*ins, *rest)

    aliases = {}
    if not first:
        aliases = {n_in + j: 1 + j for j in range(len(tails))}
        in_specs = list(in_specs) + [pl.BlockSpec(memory_space=pl.ANY)] * len(tails)
        args = list(args) + list(prev)
    out = pl.pallas_call(
        body,
        grid=(n // bt,),
        in_specs=in_specs,
        out_specs=[pl.BlockSpec((bt, o_width), lambda i: (i, 0))] + [spec(t) for t in tails],
        out_shape=[jax.ShapeDtypeStruct((n, o_width), F32)]
        + [jax.ShapeDtypeStruct((depth, n) + t, F32) for t in tails],
        scratch_shapes=[pltpu.VMEM((bt, o_width), F32)],
        input_output_aliases=aliases,
        compiler_params=_params(1),
        name=name,
    )(*args)
    return out[0], tuple(out[1:])


def _dn_dec(p, cst_all, s_all, cw, gp, dng, l, prev, bt=8):
    in_specs = [pl.BlockSpec((bt, DN_SEG), lambda i: (i, 0)),
                pl.BlockSpec((None, bt, 3, DN_QKV), lambda i: (l, i, 0, 0)),
                pl.BlockSpec((None, bt, DN_HEADS, DN_DK, DN_DV), lambda i: (l, i, 0, 0, 0)),
                pl.BlockSpec((4, DN_QKV), lambda i: (0, 0)),
                pl.BlockSpec((2, LANE), lambda i: (0, 0)),
                pl.BlockSpec((1, DN_DV), lambda i: (0, 0))]
    return _layer_state_call(_dn_dec_kernel, "dn_dec", l, prev, [p, cst_all, s_all, cw, gp, dng], in_specs,
                             512, [(3, DN_QKV), (DN_HEADS, DN_DK, DN_DV)], bt)


def _gla_dec_kernel(p_ref, s_ref, wf_ref, bf_ref, gng_ref, o_ref, sn_ref, o_scr):
    bt = p_ref.shape[0]
    pre = _dot(p_ref[:, 1536:GLA_SEG], wf_ref[...]) + bf_ref[...]
    alpha = jnp.exp((jnp.minimum(pre, 0.0) - jnp.log1p(jnp.exp(-jnp.abs(pre)))) * (1.0 / GLA_TAU))
    q_all = p_ref[:, 0:256] * (GLA_DK ** -0.5)
    k_all = p_ref[:, 256:512]
    for hh in range(GLA_HEADS):
        pair, off = (hh // 2) * LANE, (hh % 2) * GLA_DK
        v = p_ref[:, 512 + hh * GLA_DV:512 + (hh + 1) * GLA_DV]
        for bi in range(bt):
            a_col = _col_from_row(alpha[bi:bi + 1, pair:pair + LANE], GLA_DK, off)
            k_col = _col_from_row(k_all[bi:bi + 1, pair:pair + LANE], GLA_DK, off)
            q_col = _col_from_row(q_all[bi:bi + 1, pair:pair + LANE], GLA_DK, off)
            s_new = s_ref[bi, hh] * a_col + k_col * v[bi:bi + 1, :]
            sn_ref[bi, hh] = s_new
            o_scr[bi:bi + 1, hh * GLA_DV:(hh + 1) * GLA_DV] = jnp.sum(s_new * q_col, axis=0, keepdims=True)
    for hh in range(GLA_HEADS):
        lo = hh * GLA_DV
        z = p_ref[:, 1024 + lo:1024 + lo + GLA_DV]
        o_ref[:, lo:lo + GLA_DV] = _rms_rows(o_scr[:, lo:lo + GLA_DV]) * gng_ref[...] * _silu(z)


def _gla_dec(p, s_all, wf, bf, gng, l, prev, bt=8):
    in_specs = [pl.BlockSpec((bt, GLA_SEG), lambda i: (i, 0)),
                pl.BlockSpec((None, bt, GLA_HEADS, GLA_DK, GLA_DV), lambda i: (l, i, 0, 0, 0)),
                pl.BlockSpec((LANE, 256), lambda i: (0, 0)),
                pl.BlockSpec((1, 256), lambda i: (0, 0)),
                pl.BlockSpec((1, GLA_DV), lambda i: (0, 0))]
    return _layer_state_call(_gla_dec_kernel, "gla_dec", l, prev, [p, s_all, wf, bf, gng], in_specs,
                             512, [(GLA_HEADS, GLA_DK, GLA_DV)], bt)


def _rw_dec_kernel(p_ref, sh_ref, s_ref, mu_ref, rp_ref, w2_ref, a2_ref,
                   o_ref, shn_ref, sn_ref, o_scr):
    bt = p_ref.shape[0]
    seg_ones = _seg_ones()
    xs = p_ref[:, 0:RW_SHIFT]
    shn_ref[...] = xs
    xm = xs + (sh_ref[...] - xs) * mu_ref[...]
    r, k, v, kk, b, logw = _rw_features(xm, rp_ref, w2_ref, a2_ref, seg_ones)
    w = jnp.exp(logw)
    ones = jnp.ones((RW_N, RW_N), BF16)
    eye = _iota2((RW_N, RW_N), 0) == _iota2((RW_N, RW_N), 1)
    seqs = range(bt)

    def row_sums(a):
        hi = a.astype(BF16)
        lo = (a - hi.astype(F32)).astype(BF16)
        return (jnp.dot(hi, ones, preferred_element_type=F32) + jnp.dot(lo, ones, preferred_element_type=F32))

    for hh in range(RW_HEADS):
        sl = slice(hh * RW_N, (hh + 1) * RW_N)
        s_old = [s_ref[bi, hh] for bi in seqs]
        red = row_sums(jnp.concatenate([s_old[bi] * kk[bi:bi + 1, sl] for bi in seqs]
                                       + [jnp.where(eye, v[bi:bi + 1, sl], 0.0) for bi in seqs], axis=0))
        s_new = []
        for bi in seqs:
            row = slice(bi, bi + 1)
            skk = red[bi * RW_N:(bi + 1) * RW_N]
            v_col = red[(bt + bi) * RW_N:(bt + bi + 1) * RW_N]
            s_new.append(s_old[bi] * w[row, sl] - skk * b[row, sl] + v_col * k[row, sl])
            sn_ref[bi, hh] = s_new[bi]
        red = row_sums(jnp.concatenate([s_new[bi] * r[bi:bi + 1, sl] for bi in seqs], axis=0))
        for bi in seqs:
            o_scr[bi:bi + 1, sl] = jnp.sum(jnp.where(eye, red[bi * RW_N:(bi + 1) * RW_N], 0.0),
                                           axis=0, keepdims=True)
    rz = p_ref[:, RW_SHIFT:RW_SEG]
    o_ref[...] = _rw_finish(o_scr[...], r, k, v, rz, rp_ref, seg_ones)


def _rw_dec(p, sh_all, s_all, mu, rp, w2, a2, l, prev, bt=8):
    in_specs = [pl.BlockSpec((bt, RW_SEG), lambda i: (i, 0)),
                pl.BlockSpec((None, bt, RW_SHIFT), lambda i: (l, i, 0)),
                pl.BlockSpec((None, bt, RW_HEADS, RW_N, RW_N), lambda i: (l, i, 0, 0, 0)),
                pl.BlockSpec((1, RW_SHIFT), lambda i: (0, 0)),
                pl.BlockSpec((8, RW_C), lambda i: (0, 0)),
                pl.BlockSpec((LANE, RW_C), lambda i: (0, 0)),
                pl.BlockSpec((LANE, RW_C), lambda i: (0, 0))]
    return _layer_state_call(_rw_dec_kernel, "rw_dec", l, prev, [p, sh_all, s_all, mu, rp, w2, a2], in_specs,
                             RW_C, [(RW_SHIFT,), (RW_HEADS, RW_N, RW_N)], bt)


def _layer_params(l, norm_g, w_in, dn_conv_w, dn_a_log, dn_dt_bias, dn_norm_g, gla_wf, gla_bf,
                  gla_norm_g, rw_mu, rw_w0, rw_w2, rw_a0, rw_a2, rw_k_k, rw_k_a, rw_r_k,
                  rw_ln_w, rw_ln_b, w_out):
    w = w_in[l].astype(BF16)
    gp = jnp.zeros((2, LANE), F32)
    gp = gp.at[0, DN_HEADS:2 * DN_HEADS].set(dn_a_log[l]).at[1, DN_HEADS:2 * DN_HEADS].set(dn_dt_bias[l])
    zeros_r = jnp.zeros((1, RW_C), F32)
    rp = jnp.concatenate([rw_w0[l][None], rw_a0[l][None], rw_k_k[l][None], rw_k_a[l][None],
                          rw_r_k[l].reshape(1, RW_C), rw_ln_w[l][None], rw_ln_b[l][None], zeros_r], axis=0)
    return dict(
        ng=norm_g[l][None],
        w_dn=w[:, 0:DN_SEG],
        w_gla=w[:, GLA_WIN0:GLA_WIN0 + GLA_WIN],
        w_rw=jnp.pad(w[:, RW_WIN0:], ((0, 0), (0, RW_WIN - (D_PROJ - RW_WIN0)))),
        cw=dn_conv_w[l], gp=gp, dng=dn_norm_g[l][None],
        wf=jnp.pad(gla_wf[l], ((0, LANE - GLA_RANK), (0, 0))).astype(BF16),
        bf=gla_bf[l][None], gng=gla_norm_g[l][None],
        mu=rw_mu[l][None], rp=rp,
        w2=jnp.pad(rw_w2[l], ((0, LANE - 64), (0, 0))).astype(BF16),
        a2=jnp.pad(rw_a2[l], ((64, 0), (0, 0))).astype(BF16),
        w_out=w_out[l].astype(BF16))


def _gla_state_from_pairs(st):
    heads = []
    for hh in range(GLA_HEADS):
        p, hl = hh // 2, hh % 2
        blk = st[:, p, hl * GLA_DV:(hl + 1) * GLA_DV, hl * GLA_DK:(hl + 1) * GLA_DK]
        heads.append(jnp.swapaxes(blk, 1, 2))
    return jnp.stack(heads, axis=1)


def _rw_state_from_groups(s):
    heads = []
    for hh in range(RW_HEADS):
        g, hl = hh // RW_GROUP, hh % RW_GROUP
        heads.append(s[:, g, hl * RW_N:(hl + 1) * RW_N, hl * RW_N:(hl + 1) * RW_N])
    return jnp.stack(heads, axis=1)


def _prompt_tiles(t):
    tb = 512 if t % 512 == 0 else CHUNK
    tm = 512 if t % 512 == 0 else tb
    return tb, tm


def _prompt_trunk(x, mods, layers, final_g):
    b, t, _ = x.shape
    tb, tm = _prompt_tiles(t)
    states = []
    n_layers = len(layers)
    for l, lp in enumerate(layers):
        mod = mods[l]
        o_dn, conv_new, dn_new = _dn_prompt(x, mod, lp["ng"], lp["w_dn"], lp["cw"], lp["gp"], lp["dng"], tb)
        o_gla, gla_st = _gla_prompt(x, mod, lp["ng"], lp["w_gla"], lp["wf"], lp["bf"], lp["gng"], tb)
        o_rw, rs_new, rw_st = _rw_prompt(x, mod, lp["ng"], lp["w_rw"], lp["mu"], lp["rp"], lp["w2"], lp["a2"], tb)
        x = _outproj(o_dn, o_gla, o_rw, x, mod[:, 2:3, :], lp["w_out"], final_g, l == n_layers - 1, tm)
        states.append((conv_new, dn_new, _gla_state_from_pairs(gla_st), rs_new[:, 0, :],
                       _rw_state_from_groups(rw_st)))
    return x, states


def _sample_trunk(x, mods, layers, final_g, conv_s, dn_s, gla_s, rs_s, rw_s):
    n = x.shape[0]
    x = x.reshape(n, D_MODEL)
    st_dn = st_gla = st_rw = None
    n_layers = len(layers)
    for l, lp in enumerate(layers):
        shift, scale, gate = mods[l][:, 0, :], mods[l][:, 1, :], mods[l][:, 2, :]
        p_dn = _dec_proj(x, shift, scale, lp["ng"], lp["w_dn"], 0, DN_SEG)
        p_gla = _dec_proj(x, shift, scale, lp["ng"], lp["w_gla"], GLA_ROLL, GLA_SEG)
        p_rw = _dec_proj(x, shift, scale, lp["ng"], lp["w_rw"], RW_ROLL, RW_SEG)
        o_dn, st_dn = _dn_dec(p_dn, conv_s, dn_s, lp["cw"], lp["gp"], lp["dng"], l, st_dn)
        o_gla, st_gla = _gla_dec(p_gla, gla_s, lp["wf"], lp["bf"], lp["gng"], l, st_gla)
        o_rw, st_rw = _rw_dec(p_rw, rs_s, rw_s, lp["mu"], lp["rp"], lp["w2"], lp["a2"], l, st_rw)
        x = _outproj(o_dn[None], o_gla[None], o_rw[None], x[None], gate[None], lp["w_out"], final_g,
                     l == n_layers - 1, n)[0]
    return x.reshape(n, 1, D_MODEL), (st_dn[0], st_dn[1], st_gla[0], st_rw[0], st_rw[1])


def kernel(x_prompt, x_sample, c_prompt, c_sample, state_dn_conv, state_dn, state_gla, state_rwkv_shift, state_rwkv, norm_g, ada_w, ada_b, w_in, dn_conv_w, dn_a_log, dn_dt_bias, dn_norm_g, gla_wf, gla_bf, gla_norm_g, rw_mu, rw_w0, rw_w2, rw_a0, rw_a2, rw_k_k, rw_k_a, rw_r_k, rw_ln_w, rw_ln_b, w_out, final_norm_g):
    depth = w_in.shape[0]
    bp = x_prompt.shape[0]
    layers = [_layer_params(l, norm_g, w_in, dn_conv_w, dn_a_log, dn_dt_bias, dn_norm_g, gla_wf, gla_bf,
                            gla_norm_g, rw_mu, rw_w0, rw_w2, rw_a0, rw_a2, rw_k_k, rw_k_a, rw_r_k,
                            rw_ln_w, rw_ln_b, w_out) for l in range(depth)]
    final_g = final_norm_g[None]
    mod_all = _modulation(jnp.concatenate([c_prompt, c_sample], axis=0), ada_w, ada_b)
    mod_all = mod_all.reshape(depth, -1, 3, D_MODEL)
    mods_p = [mod_all[l, :bp] for l in range(depth)]
    mods_s = [mod_all[l, bp:] for l in range(depth)]

    y_p, st_p = _prompt_trunk(x_prompt, mods_p, layers, final_g)
    y_s, st_s = _sample_trunk(x_sample, mods_s, layers, final_g, state_dn_conv, state_dn, state_gla,
                              state_rwkv_shift, state_rwkv)
    stack = lambda sts, i: jnp.stack([s[i] for s in sts])
    return (y_p, y_s,
            stack(st_p, 0), stack(st_p, 1), stack(st_p, 2), stack(st_p, 3), stack(st_p, 4),
            *st_s)
```

```python
import functools

import jax
import jax.numpy as jnp
from jax import lax
from jax.experimental import pallas as pl
from jax.experimental.pallas import tpu as pltpu

F32 = jnp.float32
BF16 = jnp.bfloat16

D_MODEL = 1024
EPS = 1e-6
CHUNK = 64
SUB = 16
NEG = -1e30

DN_HEADS, DN_DK, DN_DV = 4, 128, 128
GLA_HEADS, GLA_DK, GLA_DV, GLA_RANK, GLA_TAU = 4, 64, 128, 16, 16.0
RW_HEADS, RW_N = 8, 64
RW_GROUP = 4
RW_LN_EPS = 64e-5
DN_QKV = 1536
RW_C = 512
RW_SHIFT = 1664
DN_SEG, GLA_SEG, RW_SEG = 2176, 1664, 2176
DN_PROJ, GLA_PROJ, D_PROJ = 2056, 1552, 5784
GLA_WIN0, RW_WIN0 = (DN_PROJ // 128) * 128, ((DN_PROJ + GLA_PROJ) // 128) * 128
GLA_ROLL, RW_ROLL = DN_PROJ - GLA_WIN0, DN_PROJ + GLA_PROJ - RW_WIN0
GLA_WIN, RW_WIN = GLA_SEG + 128, RW_SEG + 128
LANE = 128
VMEM_LIMIT = 56 * 1024 * 1024


def _dot(a, b):
    return jnp.dot(a.astype(BF16), b.astype(BF16), preferred_element_type=F32)


def _dot_nt(a, b):
    return lax.dot_general(a.astype(BF16), b.astype(BF16), (((1,), (1,)), ((), ())),
                           preferred_element_type=F32)


def _dot_tn(a, b):
    return lax.dot_general(a.astype(BF16), b.astype(BF16), (((0,), (0,)), ((), ())),
                           preferred_element_type=F32)


def _sigmoid(x):
    return jax.nn.sigmoid(x)


def _silu(x):
    return x * jax.nn.sigmoid(x)


def _softplus(x):
    return jnp.maximum(x, 0.0) + jnp.log1p(jnp.exp(-jnp.abs(x)))


def _iota2(shape, dim):
    return lax.broadcasted_iota(jnp.int32, shape, dim)


def _modulated_norm(x, ng, shift, scale):
    ms = jnp.mean(x * x, axis=-1, keepdims=True)
    return x * lax.rsqrt(ms + EPS) * ng * (1.0 + scale) + shift


def _l2norm_rows(x):
    return x * lax.rsqrt(jnp.sum(x * x, axis=-1, keepdims=True) + EPS)


def _rms_rows(x):
    return x * lax.rsqrt(jnp.mean(x * x, axis=-1, keepdims=True) + EPS)


def _seg_sum(x, seg_ones):
    rows, width = x.shape
    hi = x.astype(BF16)
    lo = (x - hi.astype(F32)).astype(BF16)
    out = []
    for j in range(width // seg_ones.shape[0]):
        sl = slice(j * seg_ones.shape[0], (j + 1) * seg_ones.shape[0])
        t = jnp.dot(jnp.concatenate([hi[:, sl], lo[:, sl]], axis=0), seg_ones, preferred_element_type=F32)
        out.append(t[0:rows] + t[rows:])
    return jnp.concatenate(out, axis=1)


def _split3(x):
    hi = x.astype(BF16)
    rest = x - hi.astype(F32)
    mid = rest.astype(BF16)
    return hi, mid, (rest - mid.astype(F32)).astype(BF16)


def _chunk_cumsum(x):
    r, c = _iota2((CHUNK, 3 * CHUNK), 0), _iota2((CHUNK, 3 * CHUNK), 1) & (CHUNK - 1)
    tri3 = (r >= c).astype(BF16)
    terms = _split3(x)
    out = []
    for c0 in range(0, x.shape[0], CHUNK):
        stacked = jnp.concatenate([t[c0:c0 + CHUNK] for t in terms], axis=0)
        out.append(jnp.dot(tri3, stacked, preferred_element_type=F32))
    return jnp.concatenate(out, axis=0)


def _bd_masks(n=CHUNK):
    r, c = _iota2((n, n), 0), _iota2((n, n), 1)
    same = (r >> 6) == (c >> 6)
    return dict(
        causal=same & (r >= c), strict=same & (r > c), eye=(r == c).astype(F32), same=same,
        bd16=(r >> 4) == (c >> 4),
        off32=((r >> 5) == (c >> 5)) & ((r >> 4) != (c >> 4)),
        off64=same & ((r >> 5) != (c >> 5)))


def _expand_rows(a, n_heads, shift):
    head = _iota2(a.shape, 1) >> shift
    zero = jnp.zeros_like(a)
    return jnp.concatenate([jnp.where(head == hh, a, zero) for hh in range(n_heads)], axis=0)


def _sbs_masks(n_heads):
    shape = (CHUNK, n_heads * CHUNK)
    r, c = _iota2(shape, 0), _iota2(shape, 1) & (CHUNK - 1)
    return dict(
        causal=r >= c, strict=r > c, eye=(r == c).astype(F32),
        bd16=(r >> 4) == (c >> 4),
        off32=((r >> 5) == (c >> 5)) & ((r >> 4) != (c >> 4)),
        off64=(r >> 5) != (c >> 5))


def _unit_lower_inverse(Ls, m, n_heads):
    bd = lambda a: _expand_rows(a.astype(BF16), n_heads, 6)
    n = range(len(Ls))
    Ld = [jnp.where(m["bd16"], L, 0.0) for L in Ls]
    X = [m["eye"] - a for a in Ld]
    P = [_dot(a, bd(a)) for a in Ld]
    for _ in range(2):
        t = [_dot(jnp.concatenate([P[i], X[i]], axis=0), bd(P[i])) for i in n]
        P = [a[0:CHUNK] for a in t]
        X = [X[i] + t[i][CHUNK:] for i in n]
    X = [X[i] + _dot(X[i], bd(P[i])) for i in n]
    for off in ("off32", "off64"):
        Y = [_dot(X[i], bd(jnp.where(m[off], Ls[i], 0.0))) for i in n]
        X = [X[i] - _dot(Y[i], bd(X[i])) for i in n]
    return X


def _shift_lanes_left(x, shift):
    return pltpu.roll(x, x.shape[1] - shift, 1) if shift else x


def _load_weight_window(w_ref, w_s, shift):
    width = w_s.shape[1]
    for r0 in range(0, w_ref.shape[0], 256):
        blk = _shift_lanes_left(w_ref[r0:r0 + 256, :].astype(F32), shift)
        w_s[r0:r0 + 256, :] = blk[:, 0:width].astype(BF16)


def _col_from_row(row, n, lane_off=0):
    w = row.shape[1]
    sel = (_iota2((n, w), 0) + lane_off) == _iota2((n, w), 1)
    return jnp.sum(jnp.where(sel, row, 0.0), axis=1, keepdims=True)


def _params(n_axes):
    return pltpu.CompilerParams(dimension_semantics=("arbitrary",) * n_axes,
                                vmem_limit_bytes=VMEM_LIMIT)


def _mod_kernel(c_ref, w_ref, b_ref, o_ref):
    o_ref[0] = _dot(_silu(c_ref[...]), w_ref[0]) + b_ref[0]


def _modulation(c_all, ada_w, ada_b):
    depth, d, n = ada_w.shape
    rows = c_all.shape[0]
    tn = 1024
    return pl.pallas_call(
        _mod_kernel,
        grid=(depth, n // tn),
        in_specs=[pl.BlockSpec((rows, d), lambda l, j: (0, 0)),
                  pl.BlockSpec((1, d, tn), lambda l, j: (l, 0, j)),
                  pl.BlockSpec((1, 1, tn), lambda l, j: (l, 0, j))],
        out_specs=pl.BlockSpec((1, rows, tn), lambda l, j: (l, 0, j)),
        out_shape=jax.ShapeDtypeStruct((depth, rows, n), F32),
        compiler_params=_params(2),
        name="adaln_mod",
    )(c_all, ada_w, ada_b.reshape(depth, 1, n))


def _dn_prompt_kernel(x_ref, mod_ref, ng_ref, w_ref, cw_ref, gp_ref, dng_ref,
                      o_ref, conv_ref, s_ref,
                      pbuf, qs, ks, vs, gb, gcs, os_, U_s, W_s, Q_s, A_s, K_s, E_s, S):
    t = pl.program_id(1)
    tb = x_ref.shape[1]
    nb = DN_HEADS * CHUNK
    m = _sbs_masks(DN_HEADS)
    lane_c = _iota2((CHUNK, LANE), 1)
    sbs_head = _iota2((CHUNK, nb), 1) >> 6
    gate_lanes = ((_iota2((8, LANE), 1) >= DN_HEADS) & (_iota2((8, LANE), 1) < 2 * DN_HEADS)).astype(BF16)
    heads = range(DN_HEADS)

    def stack(f):
        return jnp.concatenate([f(hh) for hh in heads], axis=0)

    def stack_heads(a):
        return stack(lambda hh: a[:, hh * DN_DK:(hh + 1) * DN_DK])

    def per_head_lanes(col_of):
        return jnp.concatenate([jnp.broadcast_to(col_of(hh), (CHUNK, DN_DK)) for hh in heads], axis=1)

    @pl.when(t == 0)
    def _():
        pbuf[0:8, :] = jnp.zeros((8, DN_SEG), F32)
        S[...] = jnp.zeros_like(S)

    h = _modulated_norm(x_ref[0], ng_ref[...], mod_ref[0, 0:1, :], mod_ref[0, 1:2, :])
    pbuf[8:8 + tb, :] = _dot(h, w_ref[...])

    conv = pbuf[5:5 + tb, 0:DN_QKV] * cw_ref[0:1, :]
    for j in range(1, 4):
        conv = conv + pbuf[5 + j:5 + j + tb, 0:DN_QKV] * cw_ref[j:j + 1, :]
    conv = _silu(conv)
    for hh in range(DN_HEADS):
        lo = hh * DN_DK
        qs[:, lo:lo + DN_DK] = _l2norm_rows(conv[:, lo:lo + DN_DK]) * (DN_DK ** -0.5)
        ks[:, lo:lo + DN_DK] = _l2norm_rows(conv[:, 512 + lo:512 + lo + DN_DK])
    vs[...] = conv[:, 1024:DN_QKV]
    gseg = pbuf[8:8 + tb, 2048:DN_SEG]
    lane = _iota2((tb, LANE), 1)
    gb[...] = jnp.where(lane < DN_HEADS, _sigmoid(gseg),
                        -jnp.exp(gp_ref[0:1, :]) * _softplus(gseg + gp_ref[1:2, :]))

    gcs[...] = _chunk_cumsum(gb[...])

    def chunk_matrices(c):
        rows = pl.ds(c * CHUNK, CHUNK)
        gbc = gb[rows, :]
        gc = gcs[rows, :]
        g_of = lambda hh: gc[:, DN_HEADS + hh:DN_HEADS + hh + 1]
        g_col = jnp.zeros((CHUNK, nb), F32)
        for hh in heads:
            g_col = jnp.where(sbs_head == hh, g_of(hh), g_col)
        g_sel = stack(lambda hh: jnp.where(lane_c == DN_HEADS + hh, gc, 0.0))
        g_row = lax.dot_general(jnp.concatenate([gate_lanes] * 3, axis=1),
                                jnp.concatenate(_split3(g_sel), axis=1),
                                (((1,), (1,)), ((), ())), preferred_element_type=F32)[0:1, :]
        decay = jnp.exp(jnp.where(m["causal"], g_col - g_row, NEG))
        k_all = ks[rows, :]
        q_all = qs[rows, :]
        beta_l = per_head_lanes(lambda hh: gbc[:, hh:hh + 1])
        g_l = per_head_lanes(g_of)
        eg_l = jnp.exp(g_l)
        kb_all = k_all * beta_l
        expand = lambda a: _expand_rows(a.astype(BF16), DN_HEADS, 7)
        kq = _dot_nt(jnp.concatenate([kb_all, q_all], axis=0), expand(k_all))
        L = jnp.where(m["strict"], kq[0:CHUNK] * decay, 0.0)
        A_s[c] = _expand_rows((kq[CHUNK:2 * CHUNK] * decay).astype(BF16), DN_HEADS, 6)
        Q_s[c] = stack_heads(q_all * eg_l).astype(BF16)
        K_s[c] = stack_heads(k_all * jnp.exp(g_l[CHUNK - 1:CHUNK, :] - g_l)).astype(BF16)
        E_s[c] = jnp.exp(gc[CHUNK - 8:CHUNK, :])
        return L, expand(vs[rows, :] * beta_l), expand(kb_all * eg_l)

    n_chunks = tb // CHUNK
    pre = [chunk_matrices(c) for c in range(n_chunks)]
    inv = _unit_lower_inverse([p[0] for p in pre], m, DN_HEADS)
    for c in range(n_chunks):
        X = inv[c].astype(BF16)
        U_s[c] = stack_heads(_dot(X, pre[c][1]))
        W_s[c] = stack_heads(_dot(X, pre[c][2])).astype(BF16)

    def recurrence(c, carry):
        rows = pl.ds(pl.multiple_of(c * CHUNK, CHUNK), CHUNK)
        w, qe, kh = W_s[c], Q_s[c], K_s[c]
        e_last = E_s[c][7:8, :]
        ws = []
        for hh in heads:
            sl = slice(hh * CHUNK, (hh + 1) * CHUNK)
            ws.append(jnp.dot(jnp.concatenate([w[sl], qe[sl]], axis=0), S[hh].astype(BF16),
                              preferred_element_type=F32))
        v_new = U_s[c] - jnp.concatenate([x[0:CHUNK] for x in ws], axis=0)
        vb = v_new.astype(BF16)
        o_st = jnp.concatenate([x[CHUNK:2 * CHUNK] for x in ws], axis=0) + jnp.dot(
            A_s[c], vb, preferred_element_type=F32)
        for hh in heads:
            sl = slice(hh * CHUNK, (hh + 1) * CHUNK)
            S[hh] = S[hh] * e_last[:, DN_HEADS + hh:DN_HEADS + hh + 1] + _dot_tn(kh[sl], vb[sl])
            os_[rows, hh * DN_DV:(hh + 1) * DN_DV] = o_st[sl]
        return carry

    lax.fori_loop(0, n_chunks, recurrence, 0, unroll=True)

    for hh in heads:
        lo = hh * DN_DV
        z = pbuf[8:8 + tb, DN_QKV + lo:DN_QKV + lo + DN_DV]
        o_ref[0, :, lo:lo + DN_DV] = (_rms_rows(os_[:, lo:lo + DN_DV]) * dng_ref[...] * _silu(z)).astype(o_ref.dtype)

    pbuf[0:8, :] = pbuf[tb:tb + 8, :]

    @pl.when(t == pl.num_programs(1) - 1)
    def _():
        conv_ref[0] = pbuf[tb + 5:tb + 8, 0:DN_QKV]
        s_ref[0] = S[...]


def _dn_prompt(x, mod, ng, w_seg, cw, gp, dng, tb):
    b, t, d = x.shape
    nch, nb = tb // CHUNK, DN_HEADS * CHUNK
    return pl.pallas_call(
        _dn_prompt_kernel,
        grid=(b, t // tb),
        in_specs=[pl.BlockSpec((1, tb, d), lambda i, j: (i, j, 0)),
                  pl.BlockSpec((1, 3, d), lambda i, j: (i, 0, 0)),
                  pl.BlockSpec((1, d), lambda i, j: (0, 0)),
                  pl.BlockSpec((d, DN_SEG), lambda i, j: (0, 0)),
                  pl.BlockSpec((4, DN_QKV), lambda i, j: (0, 0)),
                  pl.BlockSpec((2, LANE), lambda i, j: (0, 0)),
                  pl.BlockSpec((1, DN_DV), lambda i, j: (0, 0))],
        out_specs=[pl.BlockSpec((1, tb, 512), lambda i, j: (i, j, 0)),
                   pl.BlockSpec((1, 3, DN_QKV), lambda i, j: (i, 0, 0)),
                   pl.BlockSpec((1, DN_HEADS, DN_DK, DN_DV), lambda i, j: (i, 0, 0, 0))],
        out_shape=[jax.ShapeDtypeStruct((b, t, 512), BF16),
                   jax.ShapeDtypeStruct((b, 3, DN_QKV), F32),
                   jax.ShapeDtypeStruct((b, DN_HEADS, DN_DK, DN_DV), F32)],
        scratch_shapes=[pltpu.VMEM((tb + 8, DN_SEG), F32),
                        pltpu.VMEM((tb, 512), F32), pltpu.VMEM((tb, 512), F32),
                        pltpu.VMEM((tb, 512), F32), pltpu.VMEM((tb, LANE), F32),
                        pltpu.VMEM((tb, LANE), F32), pltpu.VMEM((tb, 512), F32),
                        pltpu.VMEM((nch, nb, DN_DV), F32), pltpu.VMEM((nch, nb, DN_DV), BF16),
                        pltpu.VMEM((nch, nb, DN_DK), BF16), pltpu.VMEM((nch, nb, nb), BF16),
                        pltpu.VMEM((nch, nb, DN_DK), BF16), pltpu.VMEM((nch, 8, LANE), F32),
                        pltpu.VMEM((DN_HEADS, DN_DK, DN_DV), F32)],
        compiler_params=_params(2),
        name="dn_prompt",
    )(x, mod, ng, w_seg, cw, gp, dng)


def _gla_prompt_kernel(x_ref, mod_ref, ng_ref, w_ref, wf_ref, bf_ref, gng_ref,
                       o_ref, s_ref,
                       w_s, pbuf, lfs, os_, QE_s, KH_s, EG_s, ST):
    t = pl.program_id(1)
    tb = x_ref.shape[1]
    m = _bd_masks()

    @pl.when((pl.program_id(0) == 0) & (t == 0))
    def _():
        _load_weight_window(w_ref, w_s, GLA_ROLL)

    n_sub = CHUNK // SUB
    lane = _iota2((CHUNK, LANE), 1)
    head1 = lane >= GLA_DK
    rowblk = _iota2((CHUNK, LANE), 0) >> 4
    rowloc = _iota2((CHUNK, LANE), 0) & (SUB - 1)
    er = _iota2((2 * SUB, SUB * LANE), 0)
    ec = _iota2((2 * SUB, SUB * LANE), 1)
    eblk = (((ec >> 7) == (er & (SUB - 1))) & (((ec & (LANE - 1)) >= GLA_DK) == (er >= SUB))).astype(BF16)
    st_r = _iota2((2 * GLA_DV, LANE), 0)
    st_c = _iota2((2 * GLA_DV, LANE), 1)
    st_mask = (st_r >= GLA_DV) == (st_c >= GLA_DK)

    @pl.when(t == 0)
    def _():
        ST[...] = jnp.zeros_like(ST)

    h = _modulated_norm(x_ref[0], ng_ref[...], mod_ref[0, 0:1, :], mod_ref[0, 1:2, :])
    pbuf[...] = _dot(h, w_s[...])
    pre = _dot(pbuf[:, 1536:GLA_SEG], wf_ref[...]) + bf_ref[...]
    lfs[...] = (jnp.minimum(pre, 0.0) - jnp.log1p(jnp.exp(-jnp.abs(pre)))) * (1.0 / GLA_TAU)

    lfs[...] = _chunk_cumsum(lfs[...])

    def intra_matrices(c, p):
        rows = pl.ds(c * CHUNK, CHUNK)
        lo = p * LANE
        G = lfs[rows, lo:lo + LANE]
        q = pbuf[rows, lo:lo + LANE] * (GLA_DK ** -0.5)
        k = pbuf[rows, 256 + lo:256 + lo + LANE]
        G3 = G.reshape(n_sub, SUB, LANE)
        q3 = q.reshape(n_sub, SUB, LANE)

        def pick(a3, i):
            return jnp.broadcast_to(a3[:, i:i + 1, :], (n_sub, SUB, LANE)).reshape(CHUNK, LANE)

        q_t = q * jnp.exp(G - pick(G3, 0))
        q_parts, k_parts = [], []
        for i in range(1, n_sub):
            q_parts.append(jnp.where(rowblk == i, q_t, 0.0))
            ref_i = G[i * SUB:i * SUB + 1, :]
            k_parts.append(k * jnp.exp(jnp.where(rowblk < i, ref_i - G, NEG)))
        q_big = jnp.concatenate(q_parts, axis=1)
        k_big = jnp.concatenate(k_parts, axis=1)
        head1_big = jnp.concatenate([head1] * (n_sub - 1), axis=1)
        q_both = jnp.concatenate([jnp.where(head1_big, 0.0, q_big),
                                  jnp.where(head1_big, q_big, 0.0)], axis=0)
        a_off = _dot_nt(q_both, k_big)

        slabs = []
        for i in range(SUB):
            dec = jnp.exp(jnp.where(rowloc <= i, pick(G3, i) - G, NEG))
            slabs.append((k * pick(q3, i) * dec).astype(BF16))
        a_diag = lax.dot_general(eblk, jnp.concatenate(slabs, axis=1), (((1,), (1,)), ((), ())),
                                 preferred_element_type=F32)

        g_end = G[CHUNK - 1:CHUNK, :]
        QE_s[c, p] = (q * jnp.exp(G)).astype(BF16)
        KH_s[c, p] = (k * jnp.exp(g_end - G)).astype(BF16)
        EG_s[c, p] = jnp.exp(G[CHUNK - 8:CHUNK, :])
        return a_off, a_diag

    def intra_output(c, p, a_off, a_diag):
        rows = pl.ds(c * CHUNK, CHUNK)
        for hl in range(2):
            hh = 2 * p + hl
            a_d = a_diag[hl * SUB:(hl + 1) * SUB, :]
            a_h = a_off[hl * CHUNK:(hl + 1) * CHUNK, :] + jnp.where(
                m["bd16"], jnp.concatenate([a_d] * n_sub, axis=0), 0.0)
            os_[rows, hh * GLA_DV:(hh + 1) * GLA_DV] = _dot(a_h, pbuf[rows, 512 + hh * GLA_DV:512 + (hh + 1) * GLA_DV])

    n_chunks = tb // CHUNK
    problems = [(c, p) for c in range(n_chunks) for p in range(GLA_HEADS // 2)]
    mats = [intra_matrices(c, p) for c, p in problems]
    for (c, p), ab in zip(problems, mats):
        intra_output(c, p, *ab)

    def recurrence(c, carry):
        rows = pl.ds(pl.multiple_of(c * CHUNK, CHUNK), CHUNK)
        pairs = range(GLA_HEADS // 2)
        st = [ST[p] for p in pairs]
        o_inter = [_dot_nt(QE_s[c, p], st[p]) for p in pairs]
        kv = [_dot_tn(pbuf[rows, 512 + 2 * p * LANE:512 + 2 * (p + 1) * LANE], KH_s[c, p]) for p in pairs]
        for p in pairs:
            ST[p] = st[p] * EG_s[c, p][7:8, :] + jnp.where(st_mask, kv[p], 0.0)
            sl = slice(2 * p * LANE, 2 * (p + 1) * LANE)
            os_[rows, sl] = os_[rows, sl] + o_inter[p]
        return carry

    lax.fori_loop(0, n_chunks, recurrence, 0, unroll=True)

    for hh in range(GLA_HEADS):
        sl = slice(hh * GLA_DV, (hh + 1) * GLA_DV)
        z = pbuf[:, 1024 + hh * GLA_DV:1024 + (hh + 1) * GLA_DV]
        o_ref[0, :, sl] = (_rms_rows(os_[:, sl]) * gng_ref[...] * _silu(z)).astype(o_ref.dtype)

    @pl.when(t == pl.num_programs(1) - 1)
    def _():
        s_ref[0] = ST[...]


def _gla_prompt(x, mod, ng, w_seg, wf, bf, gng, tb):
    b, t, d = x.shape
    return pl.pallas_call(
        _gla_prompt_kernel,
        grid=(b, t // tb),
        in_specs=[pl.BlockSpec((1, tb, d), lambda i, j: (i, j, 0)),
                  pl.BlockSpec((1, 3, d), lambda i, j: (i, 0, 0)),
                  pl.BlockSpec((1, d), lambda i, j: (0, 0)),
                  pl.BlockSpec((d, GLA_WIN), lambda i, j: (0, 0)),
                  pl.BlockSpec((LANE, 256), lambda i, j: (0, 0)),
                  pl.BlockSpec((1, 256), lambda i, j: (0, 0)),
                  pl.BlockSpec((1, GLA_DV), lambda i, j: (0, 0))],
        out_specs=[pl.BlockSpec((1, tb, 512), lambda i, j: (i, j, 0)),
                   pl.BlockSpec((1, 2, 2 * GLA_DV, LANE), lambda i, j: (i, 0, 0, 0))],
        out_shape=[jax.ShapeDtypeStruct((b, t, 512), BF16),
                   jax.ShapeDtypeStruct((b, 2, 2 * GLA_DV, LANE), F32)],
        scratch_shapes=[pltpu.VMEM((d, GLA_SEG), BF16),
                        pltpu.VMEM((tb, GLA_SEG), F32), pltpu.VMEM((tb, 256), F32),
                        pltpu.VMEM((tb, 512), F32),
                        pltpu.VMEM((tb // CHUNK, 2, CHUNK, LANE), BF16),
                        pltpu.VMEM((tb // CHUNK, 2, CHUNK, LANE), BF16),
                        pltpu.VMEM((tb // CHUNK, 2, 8, LANE), F32),
                        pltpu.VMEM((2, 2 * GLA_DV, LANE), F32)],
        compiler_params=_params(2),
        name="gla_prompt",
    )(x, mod, ng, w_seg, wf, bf, gng)


def _rw_features(xm, rp_ref, w2_ref, a2_ref, seg_ones):
    r = xm[:, 0:RW_C]
    k = xm[:, RW_C:2 * RW_C]
    v = xm[:, 2 * RW_C:3 * RW_C]
    lo = xm[:, 3 * RW_C:RW_SHIFT]
    w_log = -_softplus(-(rp_ref[0:1, :] + _dot(jnp.tanh(lo), w2_ref[...]))) - 0.5
    logw = -jnp.exp(w_log)
    a = _sigmoid(rp_ref[1:2, :] + _dot(lo, a2_ref[...]))
    kk = k * rp_ref[2:3, :]
    kk = kk * lax.rsqrt(_seg_sum(kk * kk, seg_ones) + EPS)
    k = k * (1.0 + (a - 1.0) * rp_ref[3:4, :])
    return r, k, v, kk, kk * a, logw


def _rw_finish(o, r, k, v, rz, rp_ref, seg_ones):
    mu = _seg_sum(o, seg_ones) * (1.0 / RW_N)
    dev = o - mu
    var = _seg_sum(dev * dev, seg_ones) * (1.0 / RW_N)
    on = dev * lax.rsqrt(var + RW_LN_EPS) * rp_ref[5:6, :] + rp_ref[6:7, :]
    bonus = _seg_sum(r * k * rp_ref[4:5, :], seg_ones) * v
    return (on + bonus) * _silu(rz)


def _seg_ones():
    n = RW_GROUP * RW_N
    return ((_iota2((n, n), 0) >> 6) == (_iota2((n, n), 1) >> 6)).astype(BF16)


def _rw_prompt_kernel(x_ref, mod_ref, ng_ref, w_ref, mu_ref, rp_ref, w2_ref, a2_ref,
                      o_ref, sh_ref, s_ref,
                      w_s, pbuf, rs, ks, vs, kks, bs, lws, gs, os_,
                      P1_s, KT_s, P2_s, RH_s, AB_s, KB_s, E_s, S):
    t = pl.program_id(1)
    tb = x_ref.shape[1]
    gw = RW_GROUP * RW_N

    @pl.when((pl.program_id(0) == 0) & (t == 0))
    def _():
        _load_weight_window(w_ref, w_s, RW_ROLL)

    m = _sbs_masks(RW_GROUP)
    same_head = (_iota2((gw, gw), 0) >> 6) == (_iota2((gw, gw), 1) >> 6)
    seg_ones = _seg_ones()
    mid = CHUNK // 2

    def expand(a):
        return _expand_rows(a.astype(BF16), RW_GROUP, 6)

    @pl.when(t == 0)
    def _():
        pbuf[0:8, :] = jnp.zeros((8, RW_SEG), F32)
        S[...] = jnp.zeros_like(S)

    h = _modulated_norm(x_ref[0], ng_ref[...], mod_ref[0, 0:1, :], mod_ref[0, 1:2, :])
    pbuf[8:8 + tb, :] = _dot(h, w_s[...])
    xs = pbuf[8:8 + tb, 0:RW_SHIFT]
    prev = pbuf[7:7 + tb, 0:RW_SHIFT]
    xm = xs + (prev - xs) * mu_ref[...]
    r, k, v, kk, b, logw = _rw_features(xm, rp_ref, w2_ref, a2_ref, seg_ones)
    rs[...] = r
    ks[...] = k
    vs[...] = v
    kks[...] = kk
    bs[...] = b
    lws[...] = logw
    gs[...] = _chunk_cumsum(logw)

    def chunk_matrices(c, g):
        rows = pl.ds(c * CHUNK, CHUNK)
        sl = slice(g * gw, (g + 1) * gw)
        G = gs[rows, sl]
        Gx = G - lws[rows, sl]
        ref = G[mid:mid + 1, :]
        r_, k_, v_, kk_, b_ = rs[rows, sl], ks[rows, sl], vs[rows, sl], kks[rows, sl], bs[rows, sl]
        e_out = jnp.exp(ref - G)
        lhs = jnp.concatenate([kk_ * jnp.exp(Gx - ref), r_ * jnp.exp(G - ref)], axis=0)
        rhs = jnp.concatenate([expand(b_ * e_out), expand(k_ * e_out)], axis=0)
        mm = _dot_nt(lhs, rhs)
        Lb = jnp.where(m["strict"], mm[0:CHUNK, 0:gw], 0.0)
        Mk = jnp.where(m["strict"], mm[0:CHUNK, gw:], 0.0)
        Arb = jnp.where(m["causal"], mm[CHUNK:, 0:gw], 0.0)
        Ark = jnp.where(m["causal"], mm[CHUNK:, gw:], 0.0)
        kv = _dot(jnp.concatenate([Mk, Ark], axis=0), expand(v_))
        P2_s[c, g] = kv[CHUNK:]
        RH_s[c, g] = (r_ * jnp.exp(G)).astype(BF16)
        AB_s[c, g] = Arb.astype(BF16)
        e_end = jnp.exp(G[CHUNK - 1:CHUNK, :] - G)
        KB_s[c, g] = jnp.concatenate([k_ * e_end, -(b_ * e_end)], axis=0).astype(BF16)
        E_s[c, g] = jnp.exp(G[CHUNK - 8:CHUNK, :])
        return Lb, jnp.concatenate([expand(kv[0:CHUNK]), expand(kk_ * jnp.exp(Gx))], axis=1)

    n_chunks = tb // CHUNK
    problems = [(c, g) for c in range(n_chunks) for g in range(RW_HEADS // RW_GROUP)]
    pre = [chunk_matrices(c, g) for c, g in problems]
    inv = _unit_lower_inverse([p[0] for p in pre], m, RW_GROUP)
    for (c, g), X, p in zip(problems, inv, pre):
        sol = _dot(X, p[1])
        P1_s[c, g] = sol[:, 0:gw]
        KT_s[c, g] = sol[:, gw:].astype(BF16)

    def recurrence(c, carry):
        rows = pl.ds(pl.multiple_of(c * CHUNK, CHUNK), CHUNK)
        groups = range(RW_HEADS // RW_GROUP)
        sls = [slice(g * gw, (g + 1) * gw) for g in groups]
        s_old = [S[g] for g in groups]
        khu = [_dot_nt(jnp.concatenate([KT_s[c, g], RH_s[c, g]], axis=0), s_old[g]) for g in groups]
        u_ = [P1_s[c, g] + khu[g][0:CHUNK] for g in groups]
        au = [_dot(AB_s[c, g], expand(u_[g])) for g in groups]
        upd = [_dot_tn(jnp.concatenate([vs[rows, sls[g]], u_[g]], axis=0), KB_s[c, g]) for g in groups]
        for g in groups:
            S[g] = s_old[g] * E_s[c, g][7:8, :] + jnp.where(same_head, upd[g], 0.0)
            os_[rows, sls[g]] = khu[g][CHUNK:2 * CHUNK] + P2_s[c, g] - au[g]
        return carry

    lax.fori_loop(0, n_chunks, recurrence, 0, unroll=True)

    rz = pbuf[8:8 + tb, RW_SHIFT:RW_SEG]
    o_ref[0] = _rw_finish(os_[...], rs[...], ks[...], vs[...], rz, rp_ref, seg_ones).astype(o_ref.dtype)
    pbuf[0:8, :] = pbuf[tb:tb + 8, :]

    @pl.when(t == pl.num_programs(1) - 1)
    def _():
        sh_ref[0] = pbuf[tb + 7:tb + 8, 0:RW_SHIFT]
        s_ref[0] = S[...]


def _rw_prompt(x, mod, ng, w_seg, mu, rp, w2, a2, tb):
    b, t, d = x.shape
    nch, ng_, gw = tb // CHUNK, RW_HEADS // RW_GROUP, RW_GROUP * RW_N
    return pl.pallas_call(
        _rw_prompt_kernel,
        grid=(b, t // tb),
        in_specs=[pl.BlockSpec((1, tb, d), lambda i, j: (i, j, 0)),
                  pl.BlockSpec((1, 3, d), lambda i, j: (i, 0, 0)),
                  pl.BlockSpec((1, d), lambda i, j: (0, 0)),
                  pl.BlockSpec((d, RW_WIN), lambda i, j: (0, 0)),
                  pl.BlockSpec((1, RW_SHIFT), lambda i, j: (0, 0)),
                  pl.BlockSpec((8, RW_C), lambda i, j: (0, 0)),
                  pl.BlockSpec((LANE, RW_C), lambda i, j: (0, 0)),
                  pl.BlockSpec((LANE, RW_C), lambda i, j: (0, 0))],
        out_specs=[pl.BlockSpec((1, tb, RW_C), lambda i, j: (i, j, 0)),
                   pl.BlockSpec((1, 1, RW_SHIFT), lambda i, j: (i, 0, 0)),
                   pl.BlockSpec((1, ng_, gw, gw), lambda i, j: (i, 0, 0, 0))],
        out_shape=[jax.ShapeDtypeStruct((b, t, RW_C), BF16),
                   jax.ShapeDtypeStruct((b, 1, RW_SHIFT), F32),
                   jax.ShapeDtypeStruct((b, ng_, gw, gw), F32)],
        scratch_shapes=[pltpu.VMEM((d, RW_SEG), BF16), pltpu.VMEM((tb + 8, RW_SEG), F32)]
        + [pltpu.VMEM((tb, RW_C), F32) for _ in range(8)]
        + [pltpu.VMEM((nch, ng_, CHUNK, gw), F32), pltpu.VMEM((nch, ng_, CHUNK, gw), BF16),
           pltpu.VMEM((nch, ng_, CHUNK, gw), F32), pltpu.VMEM((nch, ng_, CHUNK, gw), BF16),
           pltpu.VMEM((nch, ng_, CHUNK, gw), BF16), pltpu.VMEM((nch, ng_, 2 * CHUNK, gw), BF16),
           pltpu.VMEM((nch, ng_, 8, gw), F32),
           pltpu.VMEM((ng_, gw, gw), F32)],
        compiler_params=_params(2),
        name="rw_prompt",
    )(x, mod, ng, w_seg, mu, rp, w2, a2)


def _outproj_kernel(od_ref, og_ref, or_ref, x_ref, gate_ref, w_ref, fg_ref, y_ref, *, final):
    acc = (_dot(od_ref[0], w_ref[0:512, :]) + _dot(og_ref[0], w_ref[512:1024, :])
           + _dot(or_ref[0], w_ref[1024:1536, :]))
    xn = x_ref[0] + gate_ref[0] * acc
    if final:
        xn = _rms_rows(xn) * fg_ref[...]
    y_ref[0] = xn


def _outproj(o_dn, o_gla, o_rw, x, gate, w_out, fg, final, tm):
    b, t, d = x.shape
    tg = gate.shape[1]
    gspec = (pl.BlockSpec((1, 1, d), lambda i, j: (i, 0, 0)) if tg == 1
             else pl.BlockSpec((1, tm, d), lambda i, j: (i, j, 0)))
    ospec = pl.BlockSpec((1, tm, 512), lambda i, j: (i, j, 0))
    return pl.pallas_call(
        functools.partial(_outproj_kernel, final=final),
        grid=(b, t // tm),
        in_specs=[ospec, ospec, ospec,
                  pl.BlockSpec((1, tm, d), lambda i, j: (i, j, 0)),
                  gspec,
                  pl.BlockSpec((1536, d), lambda i, j: (0, 0)),
                  pl.BlockSpec((1, d), lambda i, j: (0, 0))],
        out_specs=pl.BlockSpec((1, tm, d), lambda i, j: (i, j, 0)),
        out_shape=jax.ShapeDtypeStruct((b, t, d), F32),
        compiler_params=_params(2),
        name="out_proj",
    )(o_dn, o_gla, o_rw, x, gate, w_out, fg)


def _dec_proj_kernel(x_ref, shift_ref, scale_ref, ng_ref, w_ref, o_ref, *, roll):
    h = _modulated_norm(x_ref[...], ng_ref[...], shift_ref[...], scale_ref[...])
    o_ref[...] = _shift_lanes_left(_dot(h, w_ref[...]), roll)[:, 0:o_ref.shape[1]]


def _dec_proj(x, shift, scale, ng, w_win, roll, seg):
    n, d = x.shape
    win = w_win.shape[1]
    full = lambda shape: pl.BlockSpec(shape, lambda i: (0,) * len(shape))
    return pl.pallas_call(
        functools.partial(_dec_proj_kernel, roll=roll),
        grid=(1,),
        in_specs=[full((n, d)), full((n, d)), full((n, d)), full((1, d)), full((d, win))],
        out_specs=full((n, seg)),
        out_shape=jax.ShapeDtypeStruct((n, seg), F32),
        compiler_params=_params(1),
        name="dec_proj",
    )(x, shift, scale, ng, w_win)


def _dn_dec_kernel(p_ref, cst_ref, s_ref, cw_ref, gp_ref, dng_ref,
                   o_ref, cnew_ref, sn_ref, o_scr):
    bt = p_ref.shape[0]
    qkv = p_ref[:, 0:DN_QKV]
    c0, c1, c2 = cst_ref[:, 0, :], cst_ref[:, 1, :], cst_ref[:, 2, :]
    conv = _silu(c0 * cw_ref[0:1, :] + c1 * cw_ref[1:2, :] + c2 * cw_ref[2:3, :] + qkv * cw_ref[3:4, :])
    cnew_ref[:, 0, :] = c1
    cnew_ref[:, 1, :] = c2
    cnew_ref[:, 2, :] = qkv
    gseg = p_ref[:, 2048:DN_SEG]
    beta_all = _sigmoid(gseg)
    a_all = jnp.exp(-jnp.exp(gp_ref[0:1, :]) * _softplus(gseg + gp_ref[1:2, :]))
    for hh in range(DN_HEADS):
        lo = hh * DN_DK
        q = _l2norm_rows(conv[:, lo:lo + DN_DK]) * (DN_DK ** -0.5)
        k = _l2norm_rows(conv[:, 512 + lo:512 + lo + DN_DK])
        v = conv[:, 1024 + lo:1024 + lo + DN_DV]
        for bi in range(bt):
            k_col = _col_from_row(k[bi:bi + 1, :], DN_DK)
            q_col = _col_from_row(q[bi:bi + 1, :], DN_DK)
            a = a_all[bi:bi + 1, DN_HEADS + hh:DN_HEADS + hh + 1]
            beta = beta_all[bi:bi + 1, hh:hh + 1]
            s_old = s_ref[bi, hh]
            sk = jnp.sum(s_old * k_col, axis=0, keepdims=True)
            v_new = beta * (v[bi:bi + 1, :] - a * sk)
            s_new = s_old * a + k_col * v_new
            sn_ref[bi, hh] = s_new
            o_scr[bi:bi + 1, lo:lo + DN_DV] = jnp.sum(s_new * q_col, axis=0, keepdims=True)
    for hh in range(DN_HEADS):
        lo = hh * DN_DV
        z = p_ref[:, DN_QKV + lo:DN_QKV + lo + DN_DV]
        o_ref[:, lo:lo + DN_DV] = _rms_rows(o_scr[:, lo:lo + DN_DV]) * dng_ref[...] * _silu(z)


def _layer_state_call(kernel_fn, name, l, prev, args, in_specs, o_width, tails, bt):
    n = args[0].shape[0]
    depth = next(a.shape[0] for a in args if a.ndim >= 3)
    first = prev is None
    n_in = len(args)

    def spec(t):
        z = (0,) * len(t)
        if first:
            return pl.BlockSpec((depth, bt) + t, lambda i: (0, i) + z)
        return pl.BlockSpec((None, bt) + t, lambda i: (l, i) + z)

    def body(*refs):
        ins, rest = refs[:n_in], list(refs[n_in if first else n_in + len(tails):])
        if first:
            for j in range(1, 1 + len(tails)):
                if depth > 1:
                    rest[j][1:] = jnp.zeros((depth - 1,) + rest[j].shape[1:], F32)
                rest[j] = rest[j].at[0]
        kernel_fn(*ins, *rest)

    aliases = {}
    if not first:
        aliases = {n_in + j: 1 + j for j in range(len(tails))}
        in_specs = list(in_specs) + [pl.BlockSpec(memory_space=pl.ANY)] * len(tails)
        args = list(args) + list(prev)
    out = pl.pallas_call(
        body,
        grid=(n // bt,),
        in_specs=in_specs,
        out_specs=[pl.BlockSpec((bt, o_width), lambda i: (i, 0))] + [spec(t) for t in tails],
        out_shape=[jax.ShapeDtypeStruct((n, o_width), F32)]
        + [jax.ShapeDtypeStruct((depth, n) + t, F32) for t in tails],
        scratch_shapes=[pltpu.VMEM((bt, o_width), F32)],
        input_output_aliases=aliases,
        compiler_params=_params(1),
        name=name,
    )(*args)
    return out[0], tuple(out[1:])


def _dn_dec(p, cst_all, s_all, cw, gp, dng, l, prev, bt=8):
    in_specs = [pl.BlockSpec((bt, DN_SEG), lambda i: (i, 0)),
                pl.BlockSpec((None, bt, 3, DN_QKV), lambda i: (l, i, 0, 0)),
                pl.BlockSpec((None, bt, DN_HEADS, DN_DK, DN_DV), lambda i: (l, i, 0, 0, 0)),
                pl.BlockSpec((4, DN_QKV), lambda i: (0, 0)),
                pl.BlockSpec((2, LANE), lambda i: (0, 0)),
                pl.BlockSpec((1, DN_DV), lambda i: (0, 0))]
    return _layer_state_call(_dn_dec_kernel, "dn_dec", l, prev, [p, cst_all, s_all, cw, gp, dng], in_specs,
                             512, [(3, DN_QKV), (DN_HEADS, DN_DK, DN_DV)], bt)


def _gla_dec_kernel(p_ref, s_ref, wf_ref, bf_ref, gng_ref, o_ref, sn_ref, o_scr):
    bt = p_ref.shape[0]
    pre = _dot(p_ref[:, 1536:GLA_SEG], wf_ref[...]) + bf_ref[...]
    alpha = jnp.exp((jnp.minimum(pre, 0.0) - jnp.log1p(jnp.exp(-jnp.abs(pre)))) * (1.0 / GLA_TAU))
    q_all = p_ref[:, 0:256] * (GLA_DK ** -0.5)
    k_all = p_ref[:, 256:512]
    for hh in range(GLA_HEADS):
        pair, off = (hh // 2) * LANE, (hh % 2) * GLA_DK
        v = p_ref[:, 512 + hh * GLA_DV:512 + (hh + 1) * GLA_DV]
        for bi in range(bt):
            a_col = _col_from_row(alpha[bi:bi + 1, pair:pair + LANE], GLA_DK, off)
            k_col = _col_from_row(k_all[bi:bi + 1, pair:pair + LANE], GLA_DK, off)
            q_col = _col_from_row(q_all[bi:bi + 1, pair:pair + LANE], GLA_DK, off)
            s_new = s_ref[bi, hh] * a_col + k_col * v[bi:bi + 1, :]
            sn_ref[bi, hh] = s_new
            o_scr[bi:bi + 1, hh * GLA_DV:(hh + 1) * GLA_DV] = jnp.sum(s_new * q_col, axis=0, keepdims=True)
    for hh in range(GLA_HEADS):
        lo = hh * GLA_DV
        z = p_ref[:, 1024 + lo:1024 + lo + GLA_DV]
        o_ref[:, lo:lo + GLA_DV] = _rms_rows(o_scr[:, lo:lo + GLA_DV]) * gng_ref[...] * _silu(z)


def _gla_dec(p, s_all, wf, bf, gng, l, prev, bt=8):
    in_specs = [pl.BlockSpec((bt, GLA_SEG), lambda i: (i, 0)),
                pl.BlockSpec((None, bt, GLA_HEADS, GLA_DK, GLA_DV), lambda i: (l, i, 0, 0, 0)),
                pl.BlockSpec((LANE, 256), lambda i: (0, 0)),
                pl.BlockSpec((1, 256), lambda i: (0, 0)),
                pl.BlockSpec((1, GLA_DV), lambda i: (0, 0))]
    return _layer_state_call(_gla_dec_kernel, "gla_dec", l, prev, [p, s_all, wf, bf, gng], in_specs,
                             512, [(GLA_HEADS, GLA_DK, GLA_DV)], bt)


def _rw_dec_kernel(p_ref, sh_ref, s_ref, mu_ref, rp_ref, w2_ref, a2_ref,
                   o_ref, shn_ref, sn_ref, o_scr):
    bt = p_ref.shape[0]
    seg_ones = _seg_ones()
    xs = p_ref[:, 0:RW_SHIFT]
    shn_ref[...] = xs
    xm = xs + (sh_ref[...] - xs) * mu_ref[...]
    r, k, v, kk, b, logw = _rw_features(xm, rp_ref, w2_ref, a2_ref, seg_ones)
    w = jnp.exp(logw)
    ones = jnp.ones((RW_N, RW_N), BF16)
    eye = _iota2((RW_N, RW_N), 0) == _iota2((RW_N, RW_N), 1)
    seqs = range(bt)

    def row_sums(a):
        hi = a.astype(BF16)
        lo = (a - hi.astype(F32)).astype(BF16)
        return (jnp.dot(hi, ones, preferred_element_type=F32) + jnp.dot(lo, ones, preferred_element_type=F32))

    for hh in range(RW_HEADS):
        sl = slice(hh * RW_N, (hh + 1) * RW_N)
        s_old = [s_ref[bi, hh] for bi in seqs]
        red = row_sums(jnp.concatenate([s_old[bi] * kk[bi:bi + 1, sl] for bi in seqs]
                                       + [jnp.where(eye, v[bi:bi + 1, sl], 0.0) for bi in seqs], axis=0))
        s_new = []
        for bi in seqs:
            row = slice(bi, bi + 1)
            skk = red[bi * RW_N:(bi + 1) * RW_N]
            v_col = red[(bt + bi) * RW_N:(bt + bi + 1) * RW_N]
            s_new.append(s_old[bi] * w[row, sl] - skk * b[row, sl] + v_col * k[row, sl])
            sn_ref[bi, hh] = s_new[bi]
        red = row_sums(jnp.concatenate([s_new[bi] * r[bi:bi + 1, sl] for bi in seqs], axis=0))
        for bi in seqs:
            o_scr[bi:bi + 1, sl] = jnp.sum(jnp.where(eye, red[bi * RW_N:(bi + 1) * RW_N], 0.0),
                                           axis=0, keepdims=True)
    rz = p_ref[:, RW_SHIFT:RW_SEG]
    o_ref[...] = _rw_finish(o_scr[...], r, k, v, rz, rp_ref, seg_ones)


def _rw_dec(p, sh_all, s_all, mu, rp, w2, a2, l, prev, bt=8):
    in_specs = [pl.BlockSpec((bt, RW_SEG), lambda i: (i, 0)),
                pl.BlockSpec((None, bt, RW_SHIFT), lambda i: (l, i, 0)),
                pl.BlockSpec((None, bt, RW_HEADS, RW_N, RW_N), lambda i: (l, i, 0, 0, 0)),
                pl.BlockSpec((1, RW_SHIFT), lambda i: (0, 0)),
                pl.BlockSpec((8, RW_C), lambda i: (0, 0)),
                pl.BlockSpec((LANE, RW_C), lambda i: (0, 0)),
                pl.BlockSpec((LANE, RW_C), lambda i: (0, 0))]
    return _layer_state_call(_rw_dec_kernel, "rw_dec", l, prev, [p, sh_all, s_all, mu, rp, w2, a2], in_specs,
                             RW_C, [(RW_SHIFT,), (RW_HEADS, RW_N, RW_N)], bt)


def _layer_params(l, norm_g, w_in, dn_conv_w, dn_a_log, dn_dt_bias, dn_norm_g, gla_wf, gla_bf,
                  gla_norm_g, rw_mu, rw_w0, rw_w2, rw_a0, rw_a2, rw_k_k, rw_k_a, rw_r_k,
                  rw_ln_w, rw_ln_b, w_out):
    w = w_in[l].astype(BF16)
    gp = jnp.zeros((2, LANE), F32)
    gp = gp.at[0, DN_HEADS:2 * DN_HEADS].set(dn_a_log[l]).at[1, DN_HEADS:2 * DN_HEADS].set(dn_dt_bias[l])
    zeros_r = jnp.zeros((1, RW_C), F32)
    rp = jnp.concatenate([rw_w0[l][None], rw_a0[l][None], rw_k_k[l][None], rw_k_a[l][None],
                          rw_r_k[l].reshape(1, RW_C), rw_ln_w[l][None], rw_ln_b[l][None], zeros_r], axis=0)
    return dict(
        ng=norm_g[l][None],
        w_dn=w[:, 0:DN_SEG],
        w_gla=w[:, GLA_WIN0:GLA_WIN0 + GLA_WIN],
        w_rw=jnp.pad(w[:, RW_WIN0:], ((0, 0), (0, RW_WIN - (D_PROJ - RW_WIN0)))),
        cw=dn_conv_w[l], gp=gp, dng=dn_norm_g[l][None],
        wf=jnp.pad(gla_wf[l], ((0, LANE - GLA_RANK), (0, 0))).astype(BF16),
        bf=gla_bf[l][None], gng=gla_norm_g[l][None],
        mu=rw_mu[l][None], rp=rp,
        w2=jnp.pad(rw_w2[l], ((0, LANE - 64), (0, 0))).astype(BF16),
        a2=jnp.pad(rw_a2[l], ((64, 0), (0, 0))).astype(BF16),
        w_out=w_out[l].astype(BF16))


def _gla_state_from_pairs(st):
    heads = []
    for hh in range(GLA_HEADS):
        p, hl = hh // 2, hh % 2
        blk = st[:, p, hl * GLA_DV:(hl + 1) * GLA_DV, hl * GLA_DK:(hl + 1) * GLA_DK]
        heads.append(jnp.swapaxes(blk, 1, 2))
    return jnp.stack(heads, axis=1)


def _rw_state_from_groups(s):
    heads = []
    for hh in range(RW_HEADS):
        g, hl = hh // RW_GROUP, hh % RW_GROUP
        heads.append(s[:, g, hl * RW_N:(hl + 1) * RW_N, hl * RW_N:(hl + 1) * RW_N])
    return jnp.stack(heads, axis=1)


def _prompt_tiles(t):
    tb = 512 if t % 512 == 0 else CHUNK
    tm = 512 if t % 512 == 0 else tb
    return tb, tm


def _prompt_trunk(x, mods, layers, final_g):
    b, t, _ = x.shape
    tb, tm = _prompt_tiles(t)
    states = []
    n_layers = len(layers)
    for l, lp in enumerate(layers):
        mod = mods[l]
        o_dn, conv_new, dn_new = _dn_prompt(x, mod, lp["ng"], lp["w_dn"], lp["cw"], lp["gp"], lp["dng"], tb)
        o_gla, gla_st = _gla_prompt(x, mod, lp["ng"], lp["w_gla"], lp["wf"], lp["bf"], lp["gng"], tb)
        o_rw, rs_new, rw_st = _rw_prompt(x, mod, lp["ng"], lp["w_rw"], lp["mu"], lp["rp"], lp["w2"], lp["a2"], tb)
        x = _outproj(o_dn, o_gla, o_rw, x, mod[:, 2:3, :], lp["w_out"], final_g, l == n_layers - 1, tm)
        states.append((conv_new, dn_new, _gla_state_from_pairs(gla_st), rs_new[:, 0, :],
                       _rw_state_from_groups(rw_st)))
    return x, states


def _sample_trunk(x, mods, layers, final_g, conv_s, dn_s, gla_s, rs_s, rw_s):
    n = x.shape[0]
    x = x.reshape(n, D_MODEL)
    st_dn = st_gla = st_rw = None
    n_layers = len(layers)
    for l, lp in enumerate(layers):
        shift, scale, gate = mods[l][:, 0, :], mods[l][:, 1, :], mods[l][:, 2, :]
        p_dn = _dec_proj(x, shift, scale, lp["ng"], lp["w_dn"], 0, DN_SEG)
        p_gla = _dec_proj(x, shift, scale, lp["ng"], lp["w_gla"], GLA_ROLL, GLA_SEG)
        p_rw = _dec_proj(x, shift, scale, lp["ng"], lp["w_rw"], RW_ROLL, RW_SEG)
        o_dn, st_dn = _dn_dec(p_dn, conv_s, dn_s, lp["cw"], lp["gp"], lp["dng"], l, st_dn)
        o_gla, st_gla = _gla_dec(p_gla, gla_s, lp["wf"], lp["bf"], lp["gng"], l, st_gla)
        o_rw, st_rw = _rw_dec(p_rw, rs_s, rw_s, lp["mu"], lp["rp"], lp["w2"], lp["a2"], l, st_rw)
        x = _outproj(o_dn[None], o_gla[None], o_rw[None], x[None], gate[None], lp["w_out"], final_g,
                     l == n_layers - 1, n)[0]
    return x.reshape(n, 1, D_MODEL), (st_dn[0], st_dn[1], st_gla[0], st_rw[0], st_rw[1])


def kernel(x_prompt, x_sample, c_prompt, c_sample, state_dn_conv, state_dn, state_gla, state_rwkv_shift, state_rwkv, norm_g, ada_w, ada_b, w_in, dn_conv_w, dn_a_log, dn_dt_bias, dn_norm_g, gla_wf, gla_bf, gla_norm_g, rw_mu, rw_w0, rw_w2, rw_a0, rw_a2, rw_k_k, rw_k_a, rw_r_k, rw_ln_w, rw_ln_b, w_out, final_norm_g):
    depth = w_in.shape[0]
    bp = x_prompt.shape[0]
    layers = [_layer_params(l, norm_g, w_in, dn_conv_w, dn_a_log, dn_dt_bias, dn_norm_g, gla_wf, gla_bf,
                            gla_norm_g, rw_mu, rw_w0, rw_w2, rw_a0, rw_a2, rw_k_k, rw_k_a, rw_r_k,
                            rw_ln_w, rw_ln_b, w_out) for l in range(depth)]
    final_g = final_norm_g[None]
    mod_all = _modulation(jnp.concatenate([c_prompt, c_sample], axis=0), ada_w, ada_b)
    mod_all = mod_all.reshape(depth, -1, 3, D_MODEL)
    mods_p = [mod_all[l, :bp] for l in range(depth)]
    mods_s = [mod_all[l, bp:] for l in range(depth)]

    y_p, st_p = _prompt_trunk(x_prompt, mods_p, layers, final_g)
    y_s, st_s = _sample_trunk(x_sample, mods_s, layers, final_g, state_dn_conv, state_dn, state_gla,
                              state_rwkv_shift, state_rwkv)
    stack = lambda sts, i: jnp.stack([s[i] for s in sts])
    return (y_p, y_s,
            stack(st_p, 0), stack(st_p, 1), stack(st_p, 2), stack(st_p, 3), stack(st_p, 4),
            *st_s)
```

```python
import functools

import jax
import jax.numpy as jnp
from jax import lax
from jax.experimental import pallas as pl
from jax.experimental.pallas import tpu as pltpu

F32 = jnp.float32
BF16 = jnp.bfloat16

D_MODEL = 1024
EPS = 1e-6
CHUNK = 64
SUB = 16
NEG = -1e30

DN_HEADS, DN_DK, DN_DV = 4, 128, 128
GLA_HEADS, GLA_DK, GLA_DV, GLA_RANK, GLA_TAU = 4, 64, 128, 16, 16.0
RW_HEADS, RW_N = 8, 64
RW_GROUP = 4
RW_LN_EPS = 64e-5
DN_QKV = 1536
RW_C = 512
RW_SHIFT = 1664
DN_SEG, GLA_SEG, RW_SEG = 2176, 1664, 2176
DN_PROJ, GLA_PROJ, D_PROJ = 2056, 1552, 5784
GLA_WIN0, RW_WIN0 = (DN_PROJ // 128) * 128, ((DN_PROJ + GLA_PROJ) // 128) * 128
GLA_ROLL, RW_ROLL = DN_PROJ - GLA_WIN0, DN_PROJ + GLA_PROJ - RW_WIN0
GLA_WIN, RW_WIN = GLA_SEG + 128, RW_SEG + 128
LANE = 128
VMEM_LIMIT = 56 * 1024 * 1024


def _dot(a, b):
    return jnp.dot(a.astype(BF16), b.astype(BF16), preferred_element_type=F32)


def _dot_nt(a, b):
    return lax.dot_general(a.astype(BF16), b.astype(BF16), (((1,), (1,)), ((), ())),
                           preferred_element_type=F32)


def _dot_tn(a, b):
    return lax.dot_general(a.astype(BF16), b.astype(BF16), (((0,), (0,)), ((), ())),
                           preferred_element_type=F32)


def _sigmoid(x):
    return jax.nn.sigmoid(x)


def _silu(x):
    return x * jax.nn.sigmoid(x)


def _softplus(x):
    return jnp.maximum(x, 0.0) + jnp.log1p(jnp.exp(-jnp.abs(x)))


def _iota2(shape, dim):
    return lax.broadcasted_iota(jnp.int32, shape, dim)


def _modulated_norm(x, ng, shift, scale):
    ms = jnp.mean(x * x, axis=-1, keepdims=True)
    return x * lax.rsqrt(ms + EPS) * ng * (1.0 + scale) + shift


def _l2norm_rows(x):
    return x * lax.rsqrt(jnp.sum(x * x, axis=-1, keepdims=True) + EPS)


def _rms_rows(x):
    return x * lax.rsqrt(jnp.mean(x * x, axis=-1, keepdims=True) + EPS)


def _seg_sum(x, seg_ones):
    rows, width = x.shape
    hi = x.astype(BF16)
    lo = (x - hi.astype(F32)).astype(BF16)
    out = []
    for j in range(width // seg_ones.shape[0]):
        sl = slice(j * seg_ones.shape[0], (j + 1) * seg_ones.shape[0])
        t = jnp.dot(jnp.concatenate([hi[:, sl], lo[:, sl]], axis=0), seg_ones, preferred_element_type=F32)
        out.append(t[0:rows] + t[rows:])
    return jnp.concatenate(out, axis=1)


def _split3(x):
    hi = x.astype(BF16)
    rest = x - hi.astype(F32)
    mid = rest.astype(BF16)
    return hi, mid, (rest - mid.astype(F32)).astype(BF16)


def _chunk_cumsum(x):
    r, c = _iota2((CHUNK, 3 * CHUNK), 0), _iota2((CHUNK, 3 * CHUNK), 1) & (CHUNK - 1)
    tri3 = (r >= c).astype(BF16)
    terms = _split3(x)
    out = []
    for c0 in range(0, x.shape[0], CHUNK):
        stacked = jnp.concatenate([t[c0:c0 + CHUNK] for t in terms], axis=0)
        out.append(jnp.dot(tri3, stacked, preferred_element_type=F32))
    return jnp.concatenate(out, axis=0)


def _bd_masks(n=CHUNK):
    r, c = _iota2((n, n), 0), _iota2((n, n), 1)
    same = (r >> 6) == (c >> 6)
    return dict(
        causal=same & (r >= c), strict=same & (r > c), eye=(r == c).astype(F32), same=same,
        bd16=(r >> 4) == (c >> 4),
        off32=((r >> 5) == (c >> 5)) & ((r >> 4) != (c >> 4)),
        off64=same & ((r >> 5) != (c >> 5)))


def _expand_rows(a, n_heads, shift):
    head = _iota2(a.shape, 1) >> shift
    zero = jnp.zeros_like(a)
    return jnp.concatenate([jnp.where(head == hh, a, zero) for hh in range(n_heads)], axis=0)


def _sbs_masks(n_heads):
    shape = (CHUNK, n_heads * CHUNK)
    r, c = _iota2(shape, 0), _iota2(shape, 1) & (CHUNK - 1)
    return dict(
        causal=r >= c, strict=r > c, eye=(r == c).astype(F32),
        bd16=(r >> 4) == (c >> 4),
        off32=((r >> 5) == (c >> 5)) & ((r >> 4) != (c >> 4)),
        off64=(r >> 5) != (c >> 5))


def _unit_lower_inverse(Ls, m, n_heads):
    bd = lambda a: _expand_rows(a.astype(BF16), n_heads, 6)
    n = range(len(Ls))
    Ld = [jnp.where(m["bd16"], L, 0.0) for L in Ls]
    X = [m["eye"] - a for a in Ld]
    P = [_dot(a, bd(a)) for a in Ld]
    for _ in range(2):
        t = [_dot(jnp.concatenate([P[i], X[i]], axis=0), bd(P[i])) for i in n]
        P = [a[0:CHUNK] for a in t]
        X = [X[i] + t[i][CHUNK:] for i in n]
    X = [X[i] + _dot(X[i], bd(P[i])) for i in n]
    for off in ("off32", "off64"):
        Y = [_dot(X[i], bd(jnp.where(m[off], Ls[i], 0.0))) for i in n]
        X = [X[i] - _dot(Y[i], bd(X[i])) for i in n]
    return X


def _shift_lanes_left(x, shift):
    return pltpu.roll(x, x.shape[1] - shift, 1) if shift else x


def _load_weight_window(w_ref, w_s, shift):
    width = w_s.shape[1]
    for r0 in range(0, w_ref.shape[0], 256):
        blk = _shift_lanes_left(w_ref[r0:r0 + 256, :].astype(F32), shift)
        w_s[r0:r0 + 256, :] = blk[:, 0:width].astype(BF16)


def _col_from_row(row, n, lane_off=0):
    w = row.shape[1]
    sel = (_iota2((n, w), 0) + lane_off) == _iota2((n, w), 1)
    return jnp.sum(jnp.where(sel, row, 0.0), axis=1, keepdims=True)


def _params(n_axes):
    return pltpu.CompilerParams(dimension_semantics=("arbitrary",) * n_axes,
                                vmem_limit_bytes=VMEM_LIMIT)


def _mod_kernel(c_ref, w_ref, b_ref, o_ref):
    o_ref[0] = _dot(_silu(c_ref[...]), w_ref[0]) + b_ref[0]


def _modulation(c_all, ada_w, ada_b):
    depth, d, n = ada_w.shape
    rows = c_all.shape[0]
    tn = 1024
    return pl.pallas_call(
        _mod_kernel,
        grid=(depth, n // tn),
        in_specs=[pl.BlockSpec((rows, d), lambda l, j: (0, 0)),
                  pl.BlockSpec((1, d, tn), lambda l, j: (l, 0, j)),
                  pl.BlockSpec((1, 1, tn), lambda l, j: (l, 0, j))],
        out_specs=pl.BlockSpec((1, rows, tn), lambda l, j: (l, 0, j)),
        out_shape=jax.ShapeDtypeStruct((depth, rows, n), F32),
        compiler_params=_params(2),
        name="adaln_mod",
    )(c_all, ada_w, ada_b.reshape(depth, 1, n))


def _dn_prompt_kernel(x_ref, mod_ref, ng_ref, w_ref, cw_ref, gp_ref, dng_ref,
                      o_ref, conv_ref, s_ref,
                      pbuf, qs, ks, vs, gb, gcs, os_, U_s, W_s, Q_s, A_s, K_s, E_s, S):
    t = pl.program_id(1)
    tb = x_ref.shape[1]
    nb = DN_HEADS * CHUNK
    m = _sbs_masks(DN_HEADS)
    lane_c = _iota2((CHUNK, LANE), 1)
    sbs_head = _iota2((CHUNK, nb), 1) >> 6
    gate_lanes = ((_iota2((8, LANE), 1) >= DN_HEADS) & (_iota2((8, LANE), 1) < 2 * DN_HEADS)).astype(BF16)
    heads = range(DN_HEADS)

    def stack(f):
        return jnp.concatenate([f(hh) for hh in heads], axis=0)

    def stack_heads(a):
        return stack(lambda hh: a[:, hh * DN_DK:(hh + 1) * DN_DK])

    def per_head_lanes(col_of):
        return jnp.concatenate([jnp.broadcast_to(col_of(hh), (CHUNK, DN_DK)) for hh in heads], axis=1)

    @pl.when(t == 0)
    def _():
        pbuf[0:8, :] = jnp.zeros((8, DN_SEG), F32)
        S[...] = jnp.zeros_like(S)

    h = _modulated_norm(x_ref[0], ng_ref[...], mod_ref[0, 0:1, :], mod_ref[0, 1:2, :])
    pbuf[8:8 + tb, :] = _dot(h, w_ref[...])

    conv = pbuf[5:5 + tb, 0:DN_QKV] * cw_ref[0:1, :]
    for j in range(1, 4):
        conv = conv + pbuf[5 + j:5 + j + tb, 0:DN_QKV] * cw_ref[j:j + 1, :]
    conv = _silu(conv)
    for hh in range(DN_HEADS):
        lo = hh * DN_DK
        qs[:, lo:lo + DN_DK] = _l2norm_rows(conv[:, lo:lo + DN_DK]) * (DN_DK ** -0.5)
        ks[:, lo:lo + DN_DK] = _l2norm_rows(conv[:, 512 + lo:512 + lo + DN_DK])
    vs[...] = conv[:, 1024:DN_QKV]
    gseg = pbuf[8:8 + tb, 2048:DN_SEG]
    lane = _iota2((tb, LANE), 1)
    gb[...] = jnp.where(lane < DN_HEADS, _sigmoid(gseg),
                        -jnp.exp(gp_ref[0:1, :]) * _softplus(gseg + gp_ref[1:2, :]))

    gcs[...] = _chunk_cumsum(gb[...])

    def chunk_matrices(c):
        rows = pl.ds(c * CHUNK, CHUNK)
        gbc = gb[rows, :]
        gc = gcs[rows, :]
        g_of = lambda hh: gc[:, DN_HEADS + hh:DN_HEADS + hh + 1]
        g_col = jnp.zeros((CHUNK, nb), F32)
        for hh in heads:
            g_col = jnp.where(sbs_head == hh, g_of(hh), g_col)
        g_sel = stack(lambda hh: jnp.where(lane_c == DN_HEADS + hh, gc, 0.0))
        g_row = lax.dot_general(jnp.concatenate([gate_lanes] * 3, axis=1),
                                jnp.concatenate(_split3(g_sel), axis=1),
                                (((1,), (1,)), ((), ())), preferred_element_type=F32)[0:1, :]
        decay = jnp.exp(jnp.where(m["causal"], g_col - g_row, NEG))
        k_all = ks[rows, :]
        q_all = qs[rows, :]
        beta_l = per_head_lanes(lambda hh: gbc[:, hh:hh + 1])
        g_l = per_head_lanes(g_of)
        eg_l = jnp.exp(g_l)
        kb_all = k_all * beta_l
        expand = lambda a: _expand_rows(a.astype(BF16), DN_HEADS, 7)
        kq = _dot_nt(jnp.concatenate([kb_all, q_all], axis=0), expand(k_all))
        L = jnp.where(m["strict"], kq[0:CHUNK] * decay, 0.0)
        A_s[c] = _expand_rows((kq[CHUNK:2 * CHUNK] * decay).astype(BF16), DN_HEADS, 6)
        Q_s[c] = stack_heads(q_all * eg_l).astype(BF16)
        K_s[c] = stack_heads(k_all * jnp.exp(g_l[CHUNK - 1:CHUNK, :] - g_l)).astype(BF16)
        E_s[c] = jnp.exp(gc[CHUNK - 8:CHUNK, :])
        return L, expand(vs[rows, :] * beta_l), expand(kb_all * eg_l)

    n_chunks = tb // CHUNK
    pre = [chunk_matrices(c) for c in range(n_chunks)]
    inv = _unit_lower_inverse([p[0] for p in pre], m, DN_HEADS)
    for c in range(n_chunks):
        X = inv[c].astype(BF16)
        U_s[c] = stack_heads(_dot(X, pre[c][1]))
        W_s[c] = stack_heads(_dot(X, pre[c][2])).astype(BF16)

    def recurrence(c, carry):
        rows = pl.ds(pl.multiple_of(c * CHUNK, CHUNK), CHUNK)
        w, qe, kh = W_s[c], Q_s[c], K_s[c]
        e_last = E_s[c][7:8, :]
        ws = []
        for hh in heads:
            sl = slice(hh * CHUNK, (hh + 1) * CHUNK)
            ws.append(jnp.dot(jnp.concatenate([w[sl], qe[sl]], axis=0), S[hh].astype(BF16),
                              preferred_element_type=F32))
        v_new = U_s[c] - jnp.concatenate([x[0:CHUNK] for x in ws], axis=0)
        vb = v_new.astype(BF16)
        o_st = jnp.concatenate([x[CHUNK:2 * CHUNK] for x in ws], axis=0) + jnp.dot(
            A_s[c], vb, preferred_element_type=F32)
        for hh in heads:
            sl = slice(hh * CHUNK, (hh + 1) * CHUNK)
            S[hh] = S[hh] * e_last[:, DN_HEADS + hh:DN_HEADS + hh + 1] + _dot_tn(kh[sl], vb[sl])
            os_[rows, hh * DN_DV:(hh + 1) * DN_DV] = o_st[sl]
        return carry

    lax.fori_loop(0, n_chunks, recurrence, 0, unroll=True)

    for hh in heads:
        lo = hh * DN_DV
        z = pbuf[8:8 + tb, DN_QKV + lo:DN_QKV + lo + DN_DV]
        o_ref[0, :, lo:lo + DN_DV] = (_rms_rows(os_[:, lo:lo + DN_DV]) * dng_ref[...] * _silu(z)).astype(o_ref.dtype)

    pbuf[0:8, :] = pbuf[tb:tb + 8, :]

    @pl.when(t == pl.num_programs(1) - 1)
    def _():
        conv_ref[0] = pbuf[tb + 5:tb + 8, 0:DN_QKV]
        s_ref[0] = S[...]


def _dn_prompt(x, mod, ng, w_seg, cw, gp, dng, tb):
    b, t, d = x.shape
    nch, nb = tb // CHUNK, DN_HEADS * CHUNK
    return pl.pallas_call(
        _dn_prompt_kernel,
        grid=(b, t // tb),
        in_specs=[pl.BlockSpec((1, tb, d), lambda i, j: (i, j, 0)),
                  pl.BlockSpec((1, 3, d), lambda i, j: (i, 0, 0)),
                  pl.BlockSpec((1, d), lambda i, j: (0, 0)),
                  pl.BlockSpec((d, DN_SEG), lambda i, j: (0, 0)),
                  pl.BlockSpec((4, DN_QKV), lambda i, j: (0, 0)),
                  pl.BlockSpec((2, LANE), lambda i, j: (0, 0)),
                  pl.BlockSpec((1, DN_DV), lambda i, j: (0, 0))],
        out_specs=[pl.BlockSpec((1, tb, 512), lambda i, j: (i, j, 0)),
                   pl.BlockSpec((1, 3, DN_QKV), lambda i, j: (i, 0, 0)),
                   pl.BlockSpec((1, DN_HEADS, DN_DK, DN_DV), lambda i, j: (i, 0, 0, 0))],
        out_shape=[jax.ShapeDtypeStruct((b, t, 512), BF16),
                   jax.ShapeDtypeStruct((b, 3, DN_QKV), F32),
                   jax.ShapeDtypeStruct((b, DN_HEADS, DN_DK, DN_DV), F32)],
        scratch_shapes=[pltpu.VMEM((tb + 8, DN_SEG), F32),
                        pltpu.VMEM((tb, 512), F32), pltpu.VMEM((tb, 512), F32),
                        pltpu.VMEM((tb, 512), F32), pltpu.VMEM((tb, LANE), F32),
                        pltpu.VMEM((tb, LANE), F32), pltpu.VMEM((tb, 512), F32),
                        pltpu.VMEM((nch, nb, DN_DV), F32), pltpu.VMEM((nch, nb, DN_DV), BF16),
                        pltpu.VMEM((nch, nb, DN_DK), BF16), pltpu.VMEM((nch, nb, nb), BF16),
                        pltpu.VMEM((nch, nb, DN_DK), BF16), pltpu.VMEM((nch, 8, LANE), F32),
                        pltpu.VMEM((DN_HEADS, DN_DK, DN_DV), F32)],
        compiler_params=_params(2),
        name="dn_prompt",
    )(x, mod, ng, w_seg, cw, gp, dng)


def _gla_prompt_kernel(x_ref, mod_ref, ng_ref, w_ref, wf_ref, bf_ref, gng_ref,
                       o_ref, s_ref,
                       w_s, pbuf, lfs, os_, QE_s, KH_s, EG_s, ST):
    t = pl.program_id(1)
    tb = x_ref.shape[1]
    m = _bd_masks()

    @pl.when((pl.program_id(0) == 0) & (t == 0))
    def _():
        _load_weight_window(w_ref, w_s, GLA_ROLL)

    n_sub = CHUNK // SUB
    lane = _iota2((CHUNK, LANE), 1)
    head1 = lane >= GLA_DK
    rowblk = _iota2((CHUNK, LANE), 0) >> 4
    rowloc = _iota2((CHUNK, LANE), 0) & (SUB - 1)
    er = _iota2((2 * SUB, SUB * LANE), 0)
    ec = _iota2((2 * SUB, SUB * LANE), 1)
    eblk = (((ec >> 7) == (er & (SUB - 1))) & (((ec & (LANE - 1)) >= GLA_DK) == (er >= SUB))).astype(BF16)
    st_r = _iota2((2 * GLA_DV, LANE), 0)
    st_c = _iota2((2 * GLA_DV, LANE), 1)
    st_mask = (st_r >= GLA_DV) == (st_c >= GLA_DK)

    @pl.when(t == 0)
    def _():
        ST[...] = jnp.zeros_like(ST)

    h = _modulated_norm(x_ref[0], ng_ref[...], mod_ref[0, 0:1, :], mod_ref[0, 1:2, :])
    pbuf[...] = _dot(h, w_s[...])
    pre = _dot(pbuf[:, 1536:GLA_SEG], wf_ref[...]) + bf_ref[...]
    lfs[...] = (jnp.minimum(pre, 0.0) - jnp.log1p(jnp.exp(-jnp.abs(pre)))) * (1.0 / GLA_TAU)

    lfs[...] = _chunk_cumsum(lfs[...])

    def intra_matrices(c, p):
        rows = pl.ds(c * CHUNK, CHUNK)
        lo = p * LANE
        G = lfs[rows, lo:lo + LANE]
        q = pbuf[rows, lo:lo + LANE] * (GLA_DK ** -0.5)
        k = pbuf[rows, 256 + lo:256 + lo + LANE]
        G3 = G.reshape(n_sub, SUB, LANE)
        q3 = q.reshape(n_sub, SUB, LANE)

        def pick(a3, i):
            return jnp.broadcast_to(a3[:, i:i + 1, :], (n_sub, SUB, LANE)).reshape(CHUNK, LANE)

        q_t = q * jnp.exp(G - pick(G3, 0))
        q_parts, k_parts = [], []
        for i in range(1, n_sub):
            q_parts.append(jnp.where(rowblk == i, q_t, 0.0))
            ref_i = G[i * SUB:i * SUB + 1, :]
            k_parts.append(k * jnp.exp(jnp.where(rowblk < i, ref_i - G, NEG)))
        q_big = jnp.concatenate(q_parts, axis=1)
        k_big = jnp.concatenate(k_parts, axis=1)
        head1_big = jnp.concatenate([head1] * (n_sub - 1), axis=1)
        q_both = jnp.concatenate([jnp.where(head1_big, 0.0, q_big),
                                  jnp.where(head1_big, q_big, 0.0)], axis=0)
        a_off = _dot_nt(q_both, k_big)

        slabs = []
        for i in range(SUB):
            dec = jnp.exp(jnp.where(rowloc <= i, pick(G3, i) - G, NEG))
            slabs.append((k * pick(q3, i) * dec).astype(BF16))
        a_diag = lax.dot_general(eblk, jnp.concatenate(slabs, axis=1), (((1,), (1,)), ((), ())),
                                 preferred_element_type=F32)

        g_end = G[CHUNK - 1:CHUNK, :]
        QE_s[c, p] = (q * jnp.exp(G)).astype(BF16)
        KH_s[c, p] = (k * jnp.exp(g_end - G)).astype(BF16)
        EG_s[c, p] = jnp.exp(G[CHUNK - 8:CHUNK, :])
        return a_off, a_diag

    def intra_output(c, p, a_off, a_diag):
        rows = pl.ds(c * CHUNK, CHUNK)
        for hl in range(2):
            hh = 2 * p + hl
            a_d = a_diag[hl * SUB:(hl + 1) * SUB, :]
            a_h = a_off[hl * CHUNK:(hl + 1) * CHUNK, :] + jnp.where(
                m["bd16"], jnp.concatenate([a_d] * n_sub, axis=0), 0.0)
            os_[rows, hh * GLA_DV:(hh + 1) * GLA_DV] = _dot(a_h, pbuf[rows, 512 + hh * GLA_DV:512 + (hh + 1) * GLA_DV])

    n_chunks = tb // CHUNK
    problems = [(c, p) for c in range(n_chunks) for p in range(GLA_HEADS // 2)]
    mats = [intra_matrices(c, p) for c, p in problems]
    for (c, p), ab in zip(problems, mats):
        intra_output(c, p, *ab)

    def recurrence(c, carry):
        rows = pl.ds(pl.multiple_of(c * CHUNK, CHUNK), CHUNK)
        pairs = range(GLA_HEADS // 2)
        st = [ST[p] for p in pairs]
        o_inter = [_dot_nt(QE_s[c, p], st[p]) for p in pairs]
        kv = [_dot_tn(pbuf[rows, 512 + 2 * p * LANE:512 + 2 * (p + 1) * LANE], KH_s[c, p]) for p in pairs]
        for p in pairs:
            ST[p] = st[p] * EG_s[c, p][7:8, :] + jnp.where(st_mask, kv[p], 0.0)
            sl = slice(2 * p * LANE, 2 * (p + 1) * LANE)
            os_[rows, sl] = os_[rows, sl] + o_inter[p]
        return carry

    lax.fori_loop(0, n_chunks, recurrence, 0, unroll=True)

    for hh in range(GLA_HEADS):
        sl = slice(hh * GLA_DV, (hh + 1) * GLA_DV)
        z = pbuf[:, 1024 + hh * GLA_DV:1024 + (hh + 1) * GLA_DV]
        o_ref[0, :, sl] = (_rms_rows(os_[:, sl]) * gng_ref[...] * _silu(z)).astype(o_ref.dtype)

    @pl.when(t == pl.num_programs(1) - 1)
    def _():
        s_ref[0] = ST[...]


def _gla_prompt(x, mod, ng, w_seg, wf, bf, gng, tb):
    b, t, d = x.shape
    return pl.pallas_call(
        _gla_prompt_kernel,
        grid=(b, t // tb),
        in_specs=[pl.BlockSpec((1, tb, d), lambda i, j: (i, j, 0)),
                  pl.BlockSpec((1, 3, d), lambda i, j: (i, 0, 0)),
                  pl.BlockSpec((1, d), lambda i, j: (0, 0)),
                  pl.BlockSpec((d, GLA_WIN), lambda i, j: (0, 0)),
                  pl.BlockSpec((LANE, 256), lambda i, j: (0, 0)),
                  pl.BlockSpec((1, 256), lambda i, j: (0, 0)),
                  pl.BlockSpec((1, GLA_DV), lambda i, j: (0, 0))],
        out_specs=[pl.BlockSpec((1, tb, 512), lambda i, j: (i, j, 0)),
                   pl.BlockSpec((1, 2, 2 * GLA_DV, LANE), lambda i, j: (i, 0, 0, 0))],
        out_shape=[jax.ShapeDtypeStruct((b, t, 512), BF16),
                   jax.ShapeDtypeStruct((b, 2, 2 * GLA_DV, LANE), F32)],
        scratch_shapes=[pltpu.VMEM((d, GLA_SEG), BF16),
                        pltpu.VMEM((tb, GLA_SEG), F32), pltpu.VMEM((tb, 256), F32),
                        pltpu.VMEM((tb, 512), F32),
                        pltpu.VMEM((tb // CHUNK, 2, CHUNK, LANE), BF16),
                        pltpu.VMEM((tb // CHUNK, 2, CHUNK, LANE), BF16),
                        pltpu.VMEM((tb // CHUNK, 2, 8, LANE), F32),
                        pltpu.VMEM((2, 2 * GLA_DV, LANE), F32)],
        compiler_params=_params(2),
        name="gla_prompt",
    )(x, mod, ng, w_seg, wf, bf, gng)


def _rw_features(xm, rp_ref, w2_ref, a2_ref, seg_ones):
    r = xm[:, 0:RW_C]
    k = xm[:, RW_C:2 * RW_C]
    v = xm[:, 2 * RW_C:3 * RW_C]
    lo = xm[:, 3 * RW_C:RW_SHIFT]
    w_log = -_softplus(-(rp_ref[0:1, :] + _dot(jnp.tanh(lo), w2_ref[...]))) - 0.5
    logw = -jnp.exp(w_log)
    a = _sigmoid(rp_ref[1:2, :] + _dot(lo, a2_ref[...]))
    kk = k * rp_ref[2:3, :]
    kk = kk * lax.rsqrt(_seg_sum(kk * kk, seg_ones) + EPS)
    k = k * (1.0 + (a - 1.0) * rp_ref[3:4, :])
    return r, k, v, kk, kk * a, logw


def _rw_finish(o, r, k, v, rz, rp_ref, seg_ones):
    mu = _seg_sum(o, seg_ones) * (1.0 / RW_N)
    dev = o - mu
    var = _seg_sum(dev * dev, seg_ones) * (1.0 / RW_N)
    on = dev * lax.rsqrt(var + RW_LN_EPS) * rp_ref[5:6, :] + rp_ref[6:7, :]
    bonus = _seg_sum(r * k * rp_ref[4:5, :], seg_ones) * v
    return (on + bonus) * _silu(rz)


def _seg_ones():
    n = RW_GROUP * RW_N
    return ((_iota2((n, n), 0) >> 6) == (_iota2((n, n), 1) >> 6)).astype(BF16)


def _rw_prompt_kernel(x_ref, mod_ref, ng_ref, w_ref, mu_ref, rp_ref, w2_ref, a2_ref,
                      o_ref, sh_ref, s_ref,
                      w_s, pbuf, rs, ks, vs, kks, bs, lws, gs, os_,
                      P1_s, KT_s, P2_s, RH_s, AB_s, KB_s, E_s, S):
    t = pl.program_id(1)
    tb = x_ref.shape[1]
    gw = RW_GROUP * RW_N

    @pl.when((pl.program_id(0) == 0) & (t == 0))
    def _():
        _load_weight_window(w_ref, w_s, RW_ROLL)

    m = _sbs_masks(RW_GROUP)
    same_head = (_iota2((gw, gw), 0) >> 6) == (_iota2((gw, gw), 1) >> 6)
    seg_ones = _seg_ones()
    mid = CHUNK // 2

    def expand(a):
        return _expand_rows(a.astype(BF16), RW_GROUP, 6)

    @pl.when(t == 0)
    def _():
        pbuf[0:8, :] = jnp.zeros((8, RW_SEG), F32)
        S[...] = jnp.zeros_like(S)

    h = _modulated_norm(x_ref[0], ng_ref[...], mod_ref[0, 0:1, :], mod_ref[0, 1:2, :])
    pbuf[8:8 + tb, :] = _dot(h, w_s[...])
    xs = pbuf[8:8 + tb, 0:RW_SHIFT]
    prev = pbuf[7:7 + tb, 0:RW_SHIFT]
    xm = xs + (prev - xs) * mu_ref[...]
    r, k, v, kk, b, logw = _rw_features(xm, rp_ref, w2_ref, a2_ref, seg_ones)
    rs[...] = r
    ks[...] = k
    vs[...] = v
    kks[...] = kk
    bs[...] = b
    lws[...] = logw
    gs[...] = _chunk_cumsum(logw)

    def chunk_matrices(c, g):
        rows = pl.ds(c * CHUNK, CHUNK)
        sl = slice(g * gw, (g + 1) * gw)
        G = gs[rows, sl]
        Gx = G - lws[rows, sl]
        ref = G[mid:mid + 1, :]
        r_, k_, v_, kk_, b_ = rs[rows, sl], ks[rows, sl], vs[rows, sl], kks[rows, sl], bs[rows, sl]
        e_out = jnp.exp(ref - G)
        lhs = jnp.concatenate([kk_ * jnp.exp(Gx - ref), r_ * jnp.exp(G - ref)], axis=0)
        rhs = jnp.concatenate([expand(b_ * e_out), expand(k_ * e_out)], axis=0)
        mm = _dot_nt(lhs, rhs)
        Lb = jnp.where(m["strict"], mm[0:CHUNK, 0:gw], 0.0)
        Mk = jnp.where(m["strict"], mm[0:CHUNK, gw:], 0.0)
        Arb = jnp.where(m["causal"], mm[CHUNK:, 0:gw], 0.0)
        Ark = jnp.where(m["causal"], mm[CHUNK:, gw:], 0.0)
        kv = _dot(jnp.concatenate([Mk, Ark], axis=0), expand(v_))
        P2_s[c, g] = kv[CHUNK:]
        RH_s[c, g] = (r_ * jnp.exp(G)).astype(BF16)
        AB_s[c, g] = Arb.astype(BF16)
        e_end = jnp.exp(G[CHUNK - 1:CHUNK, :] - G)
        KB_s[c, g] = jnp.concatenate([k_ * e_end, -(b_ * e_end)], axis=0).astype(BF16)
        E_s[c, g] = jnp.exp(G[CHUNK - 8:CHUNK, :])
        return Lb, jnp.concatenate([expand(kv[0:CHUNK]), expand(kk_ * jnp.exp(Gx))], axis=1)

    n_chunks = tb // CHUNK
    problems = [(c, g) for c in range(n_chunks) for g in range(RW_HEADS // RW_GROUP)]
    pre = [chunk_matrices(c, g) for c, g in problems]
    inv = _unit_lower_inverse([p[0] for p in pre], m, RW_GROUP)
    for (c, g), X, p in zip(problems, inv, pre):
        sol = _dot(X, p[1])
        P1_s[c, g] = sol[:, 0:gw]
        KT_s[c, g] = sol[:, gw:].astype(BF16)

    def recurrence(c, carry):
        rows = pl.ds(pl.multiple_of(c * CHUNK, CHUNK), CHUNK)
        groups = range(RW_HEADS // RW_GROUP)
        sls = [slice(g * gw, (g + 1) * gw) for g in groups]
        s_old = [S[g] for g in groups]
        khu = [_dot_nt(jnp.concatenate([KT_s[c, g], RH_s[c, g]], axis=0), s_old[g]) for g in groups]
        u_ = [P1_s[c, g] + khu[g][0:CHUNK] for g in groups]
        au = [_dot(AB_s[c, g], expand(u_[g])) for g in groups]
        upd = [_dot_tn(jnp.concatenate([vs[rows, sls[g]], u_[g]], axis=0), KB_s[c, g]) for g in groups]
        for g in groups:
            S[g] = s_old[g] * E_s[c, g][7:8, :] + jnp.where(same_head, upd[g], 0.0)
            os_[rows, sls[g]] = khu[g][CHUNK:2 * CHUNK] + P2_s[c, g] - au[g]
        return carry

    lax.fori_loop(0, n_chunks, recurrence, 0, unroll=True)

    rz = pbuf[8:8 + tb, RW_SHIFT:RW_SEG]
    o_ref[0] = _rw_finish(os_[...], rs[...], ks[...], vs[...], rz, rp_ref, seg_ones).astype(o_ref.dtype)
    pbuf[0:8, :] = pbuf[tb:tb + 8, :]

    @pl.when(t == pl.num_programs(1) - 1)
    def _():
        sh_ref[0] = pbuf[tb + 7:tb + 8, 0:RW_SHIFT]
        s_ref[0] = S[...]


def _rw_prompt(x, mod, ng, w_seg, mu, rp, w2, a2, tb):
    b, t, d = x.shape
    nch, ng_, gw = tb // CHUNK, RW_HEADS // RW_GROUP, RW_GROUP * RW_N
    return pl.pallas_call(
        _rw_prompt_kernel,
        grid=(b, t // tb),
        in_specs=[pl.BlockSpec((1, tb, d), lambda i, j: (i, j, 0)),
                  pl.BlockSpec((1, 3, d), lambda i, j: (i, 0, 0)),
                  pl.BlockSpec((1, d), lambda i, j: (0, 0)),
                  pl.BlockSpec((d, RW_WIN), lambda i, j: (0, 0)),
                  pl.BlockSpec((1, RW_SHIFT), lambda i, j: (0, 0)),
                  pl.BlockSpec((8, RW_C), lambda i, j: (0, 0)),
                  pl.BlockSpec((LANE, RW_C), lambda i, j: (0, 0)),
                  pl.BlockSpec((LANE, RW_C), lambda i, j: (0, 0))],
        out_specs=[pl.BlockSpec((1, tb, RW_C), lambda i, j: (i, j, 0)),
                   pl.BlockSpec((1, 1, RW_SHIFT), lambda i, j: (i, 0, 0)),
                   pl.BlockSpec((1, ng_, gw, gw), lambda i, j: (i, 0, 0, 0))],
        out_shape=[jax.ShapeDtypeStruct((b, t, RW_C), BF16),
                   jax.ShapeDtypeStruct((b, 1, RW_SHIFT), F32),
                   jax.ShapeDtypeStruct((b, ng_, gw, gw), F32)],
        scratch_shapes=[pltpu.VMEM((d, RW_SEG), BF16), pltpu.VMEM((tb + 8, RW_SEG), F32)]
        + [pltpu.VMEM((tb, RW_C), F32) for _ in range(8)]
        + [pltpu.VMEM((nch, ng_, CHUNK, gw), F32), pltpu.VMEM((nch, ng_, CHUNK, gw), BF16),
           pltpu.VMEM((nch, ng_, CHUNK, gw), F32), pltpu.VMEM((nch, ng_, CHUNK, gw), BF16),
           pltpu.VMEM((nch, ng_, CHUNK, gw), BF16), pltpu.VMEM((nch, ng_, 2 * CHUNK, gw), BF16),
           pltpu.VMEM((nch, ng_, 8, gw), F32),
           pltpu.VMEM((ng_, gw, gw), F32)],
        compiler_params=_params(2),
        name="rw_prompt",
    )(x, mod, ng, w_seg, mu, rp, w2, a2)


def _outproj_kernel(od_ref, og_ref, or_ref, x_ref, gate_ref, w_ref, fg_ref, y_ref, *, final):
    acc = (_dot(od_ref[0], w_ref[0:512, :]) + _dot(og_ref[0], w_ref[512:1024, :])
           + _dot(or_ref[0], w_ref[1024:1536, :]))
    xn = x_ref[0] + gate_ref[0] * acc
    if final:
        xn = _rms_rows(xn) * fg_ref[...]
    y_ref[0] = xn


def _outproj(o_dn, o_gla, o_rw, x, gate, w_out, fg, final, tm):
    b, t, d = x.shape
    tg = gate.shape[1]
    gspec = (pl.BlockSpec((1, 1, d), lambda i, j: (i, 0, 0)) if tg == 1
             else pl.BlockSpec((1, tm, d), lambda i, j: (i, j, 0)))
    ospec = pl.BlockSpec((1, tm, 512), lambda i, j: (i, j, 0))
    return pl.pallas_call(
        functools.partial(_outproj_kernel, final=final),
        grid=(b, t // tm),
        in_specs=[ospec, ospec, ospec,
                  pl.BlockSpec((1, tm, d), lambda i, j: (i, j, 0)),
                  gspec,
                  pl.BlockSpec((1536, d), lambda i, j: (0, 0)),
                  pl.BlockSpec((1, d), lambda i, j: (0, 0))],
        out_specs=pl.BlockSpec((1, tm, d), lambda i, j: (i, j, 0)),
        out_shape=jax.ShapeDtypeStruct((b, t, d), F32),
        compiler_params=_params(2),
        name="out_proj",
    )(o_dn, o_gla, o_rw, x, gate, w_out, fg)


def _dec_proj_kernel(x_ref, shift_ref, scale_ref, ng_ref, w_ref, o_ref, *, roll):
    h = _modulated_norm(x_ref[...], ng_ref[...], shift_ref[...], scale_ref[...])
    o_ref[...] = _shift_lanes_left(_dot(h, w_ref[...]), roll)[:, 0:o_ref.shape[1]]


def _dec_proj(x, shift, scale, ng, w_win, roll, seg):
    n, d = x.shape
    win = w_win.shape[1]
    full = lambda shape: pl.BlockSpec(shape, lambda i: (0,) * len(shape))
    return pl.pallas_call(
        functools.partial(_dec_proj_kernel, roll=roll),
        grid=(1,),
        in_specs=[full((n, d)), full((n, d)), full((n, d)), full((1, d)), full((d, win))],
        out_specs=full((n, seg)),
        out_shape=jax.ShapeDtypeStruct((n, seg), F32),
        compiler_params=_params(1),
        name="dec_proj",
    )(x, shift, scale, ng, w_win)


def _dn_dec_kernel(p_ref, cst_ref, s_ref, cw_ref, gp_ref, dng_ref,
                   o_ref, cnew_ref, sn_ref, o_scr):
    bt = p_ref.shape[0]
    qkv = p_ref[:, 0:DN_QKV]
    c0, c1, c2 = cst_ref[:, 0, :], cst_ref[:, 1, :], cst_ref[:, 2, :]
    conv = _silu(c0 * cw_ref[0:1, :] + c1 * cw_ref[1:2, :] + c2 * cw_ref[2:3, :] + qkv * cw_ref[3:4, :])
    cnew_ref[:, 0, :] = c1
    cnew_ref[:, 1, :] = c2
    cnew_ref[:, 2, :] = qkv
    gseg = p_ref[:, 2048:DN_SEG]
    beta_all = _sigmoid(gseg)
    a_all = jnp.exp(-jnp.exp(gp_ref[0:1, :]) * _softplus(gseg + gp_ref[1:2, :]))
    for hh in range(DN_HEADS):
        lo = hh * DN_DK
        q = _l2norm_rows(conv[:, lo:lo + DN_DK]) * (DN_DK ** -0.5)
        k = _l2norm_rows(conv[:, 512 + lo:512 + lo + DN_DK])
        v = conv[:, 1024 + lo:1024 + lo + DN_DV]
        for bi in range(bt):
            k_col = _col_from_row(k[bi:bi + 1, :], DN_DK)
            q_col = _col_from_row(q[bi:bi + 1, :], DN_DK)
            a = a_all[bi:bi + 1, DN_HEADS + hh:DN_HEADS + hh + 1]
            beta = beta_all[bi:bi + 1, hh:hh + 1]
            s_old = s_ref[bi, hh]
            sk = jnp.sum(s_old * k_col, axis=0, keepdims=True)
            v_new = beta * (v[bi:bi + 1, :] - a * sk)
            s_new = s_old * a + k_col * v_new
            sn_ref[bi, hh] = s_new
            o_scr[bi:bi + 1, lo:lo + DN_DV] = jnp.sum(s_new * q_col, axis=0, keepdims=True)
    for hh in range(DN_HEADS):
        lo = hh * DN_DV
        z = p_ref[:, DN_QKV + lo:DN_QKV + lo + DN_DV]
        o_ref[:, lo:lo + DN_DV] = _rms_rows(o_scr[:, lo:lo + DN_DV]) * dng_ref[...] * _silu(z)


def _layer_state_call(kernel_fn, name, l, prev, args, in_specs, o_width, tails, bt):
    n = args[0].shape[0]
    depth = next(a.shape[0] for a in args if a.ndim >= 3)
    first = prev is None
    n_in = len(args)

    def spec(t):
        z = (0,) * len(t)
        if first:
            return pl.BlockSpec((depth, bt) + t, lambda i: (0, i) + z)
        return pl.BlockSpec((None, bt) + t, lambda i: (l, i) + z)

    def body(*refs):
        ins, rest = refs[:n_in], list(refs[n_in if first else n_in + len(tails):])
        if first:
            for j in range(1, 1 + len(tails)):
                if depth > 1:
                    rest[j][1:] = jnp.zeros((depth - 1,) + rest[j].shape[1:], F32)
                rest[j] = rest[j].at[0]
        kernel_fn(*ins, *rest)

    aliases = {}
    if not first:
        aliases = {n_in + j: 1 + j for j in range(len(tails))}
        in_specs = list(in_specs) + [pl.BlockSpec(memory_space=pl.ANY)] * len(tails)
        args = list(args) + list(prev)
    out = pl.pallas_call(
        body,
        grid=(n // bt,),
        in_specs=in_specs,
        out_specs=[pl.BlockSpec((bt, o_width), lambda i: (i, 0))] + [spec(t) for t in tails],
        out_shape=[jax.ShapeDtypeStruct((n, o_width), F32)]
        + [jax.ShapeDtypeStruct((depth, n) + t, F32) for t in tails],
        scratch_shapes=[pltpu.VMEM((bt, o_width), F32)],
        input_output_aliases=aliases,
        compiler_params=_params(1),
        name=name,
    )(*args)
    return out[0], tuple(out[1:])


def _dn_dec(p, cst_all, s_all, cw, gp, dng, l, prev, bt=8):
    in_specs = [pl.BlockSpec((bt, DN_SEG), lambda i: (i, 0)),
                pl.BlockSpec((None, bt, 3, DN_QKV), lambda i: (l, i, 0, 0)),
                pl.BlockSpec((None, bt, DN_HEADS, DN_DK, DN_DV), lambda i: (l, i, 0, 0, 0)),
                pl.BlockSpec((4, DN_QKV), lambda i: (0, 0)),
                pl.BlockSpec((2, LANE), lambda i: (0, 0)),
                pl.BlockSpec((1, DN_DV), lambda i: (0, 0))]
    return _layer_state_call(_dn_dec_kernel, "dn_dec", l, prev, [p, cst_all, s_all, cw, gp, dng], in_specs,
                             512, [(3, DN_QKV), (DN_HEADS, DN_DK, DN_DV)], bt)


def _gla_dec_kernel(p_ref, s_ref, wf_ref, bf_ref, gng_ref, o_ref, sn_ref, o_scr):
    bt = p_ref.shape[0]
    pre = _dot(p_ref[:, 1536:GLA_SEG], wf_ref[...]) + bf_ref[...]
    alpha = jnp.exp((jnp.minimum(pre, 0.0) - jnp.log1p(jnp.exp(-jnp.abs(pre)))) * (1.0 / GLA_TAU))
    q_all = p_ref[:, 0:256] * (GLA_DK ** -0.5)
    k_all = p_ref[:, 256:512]
    for hh in range(GLA_HEADS):
        pair, off = (hh // 2) * LANE, (hh % 2) * GLA_DK
        v = p_ref[:, 512 + hh * GLA_DV:512 + (hh + 1) * GLA_DV]
        for bi in range(bt):
            a_col = _col_from_row(alpha[bi:bi + 1, pair:pair + LANE], GLA_DK, off)
            k_col = _col_from_row(k_all[bi:bi + 1, pair:pair + LANE], GLA_DK, off)
            q_col = _col_from_row(q_all[bi:bi + 1, pair:pair + LANE], GLA_DK, off)
            s_new = s_ref[bi, hh] * a_col + k_col * v[bi:bi + 1, :]
            sn_ref[bi, hh] = s_new
            o_scr[bi:bi + 1, hh * GLA_DV:(hh + 1) * GLA_DV] = jnp.sum(s_new * q_col, axis=0, keepdims=True)
    for hh in range(GLA_HEADS):
        lo = hh * GLA_DV
        z = p_ref[:, 1024 + lo:1024 + lo + GLA_DV]
        o_ref[:, lo:lo + GLA_DV] = _rms_rows(o_scr[:, lo:lo + GLA_DV]) * gng_ref[...] * _silu(z)


def _gla_dec(p, s_all, wf, bf, gng, l, prev, bt=8):
    in_specs = [pl.BlockSpec((bt, GLA_SEG), lambda i: (i, 0)),
                pl.BlockSpec((None, bt, GLA_HEADS, GLA_DK, GLA_DV), lambda i: (l, i, 0, 0, 0)),
                pl.BlockSpec((LANE, 256), lambda i: (0, 0)),
                pl.BlockSpec((1, 256), lambda i: (0, 0)),
                pl.BlockSpec((1, GLA_DV), lambda i: (0, 0))]
    return _layer_state_call(_gla_dec_kernel, "gla_dec", l, prev, [p, s_all, wf, bf, gng], in_specs,
                             512, [(GLA_HEADS, GLA_DK, GLA_DV)], bt)


def _rw_dec_kernel(p_ref, sh_ref, s_ref, mu_ref, rp_ref, w2_ref, a2_ref,
                   o_ref, shn_ref, sn_ref, o_scr):
    bt = p_ref.shape[0]
    seg_ones = _seg_ones()
    xs = p_ref[:, 0:RW_SHIFT]
    shn_ref[...] = xs
    xm = xs + (sh_ref[...] - xs) * mu_ref[...]
    r, k, v, kk, b, logw = _rw_features(xm, rp_ref, w2_ref, a2_ref, seg_ones)
    w = jnp.exp(logw)
    ones = jnp.ones((RW_N, RW_N), BF16)
    eye = _iota2((RW_N, RW_N), 0) == _iota2((RW_N, RW_N), 1)
    seqs = range(bt)

    def row_sums(a):
        hi = a.astype(BF16)
        lo = (a - hi.astype(F32)).astype(BF16)
        return (jnp.dot(hi, ones, preferred_element_type=F32) + jnp.dot(lo, ones, preferred_element_type=F32))

    for hh in range(RW_HEADS):
        sl = slice(hh * RW_N, (hh + 1) * RW_N)
        s_old = [s_ref[bi, hh] for bi in seqs]
        red = row_sums(jnp.concatenate([s_old[bi] * kk[bi:bi + 1, sl] for bi in seqs]
                                       + [jnp.where(eye, v[bi:bi + 1, sl], 0.0) for bi in seqs], axis=0))
        s_new = []
        for bi in seqs:
            row = slice(bi, bi + 1)
            skk = red[bi * RW_N:(bi + 1) * RW_N]
            v_col = red[(bt + bi) * RW_N:(bt + bi + 1) * RW_N]
            s_new.append(s_old[bi] * w[row, sl] - skk * b[row, sl] + v_col * k[row, sl])
            sn_ref[bi, hh] = s_new[bi]
        red = row_sums(jnp.concatenate([s_new[bi] * r[bi:bi + 1, sl] for bi in seqs], axis=0))
        for bi in seqs:
            o_scr[bi:bi + 1, sl] = jnp.sum(jnp.where(eye, red[bi * RW_N:(bi + 1) * RW_N], 0.0),
                                           axis=0, keepdims=True)
    rz = p_ref[:, RW_SHIFT:RW_SEG]
    o_ref[...] = _rw_finish(o_scr[...], r, k, v, rz, rp_ref, seg_ones)


def _rw_dec(p, sh_all, s_all, mu, rp, w2, a2, l, prev, bt=8):
    in_specs = [pl.BlockSpec((bt, RW_SEG), lambda i: (i, 0)),
                pl.BlockSpec((None, bt, RW_SHIFT), lambda i: (l, i, 0)),
                pl.BlockSpec((None, bt, RW_HEADS, RW_N, RW_N), lambda i: (l, i, 0, 0, 0)),
                pl.BlockSpec((1, RW_SHIFT), lambda i: (0, 0)),
                pl.BlockSpec((8, RW_C), lambda i: (0, 0)),
                pl.BlockSpec((LANE, RW_C), lambda i: (0, 0)),
                pl.BlockSpec((LANE, RW_C), lambda i: (0, 0))]
    return _layer_state_call(_rw_dec_kernel, "rw_dec", l, prev, [p, sh_all, s_all, mu, rp, w2, a2], in_specs,
                             RW_C, [(RW_SHIFT,), (RW_HEADS, RW_N, RW_N)], bt)


def _layer_params(l, norm_g, w_in, dn_conv_w, dn_a_log, dn_dt_bias, dn_norm_g, gla_wf, gla_bf,
                  gla_norm_g, rw_mu, rw_w0, rw_w2, rw_a0, rw_a2, rw_k_k, rw_k_a, rw_r_k,
                  rw_ln_w, rw_ln_b, w_out):
    w = w_in[l].astype(BF16)
    gp = jnp.zeros((2, LANE), F32)
    gp = gp.at[0, DN_HEADS:2 * DN_HEADS].set(dn_a_log[l]).at[1, DN_HEADS:2 * DN_HEADS].set(dn_dt_bias[l])
    zeros_r = jnp.zeros((1, RW_C), F32)
    rp = jnp.concatenate([rw_w0[l][None], rw_a0[l][None], rw_k_k[l][None], rw_k_a[l][None],
                          rw_r_k[l].reshape(1, RW_C), rw_ln_w[l][None], rw_ln_b[l][None], zeros_r], axis=0)
    return dict(
        ng=norm_g[l][None],
        w_dn=w[:, 0:DN_SEG],
        w_gla=w[:, GLA_WIN0:GLA_WIN0 + GLA_WIN],
        w_rw=jnp.pad(w[:, RW_WIN0:], ((0, 0), (0, RW_WIN - (D_PROJ - RW_WIN0)))),
        cw=dn_conv_w[l], gp=gp, dng=dn_norm_g[l][None],
        wf=jnp.pad(gla_wf[l], ((0, LANE - GLA_RANK), (0, 0))).astype(BF16),
        bf=gla_bf[l][None], gng=gla_norm_g[l][None],
        mu=rw_mu[l][None], rp=rp,
        w2=jnp.pad(rw_w2[l], ((0, LANE - 64), (0, 0))).astype(BF16),
        a2=jnp.pad(rw_a2[l], ((64, 0), (0, 0))).astype(BF16),
        w_out=w_out[l].astype(BF16))


def _gla_state_from_pairs(st):
    heads = []
    for hh in range(GLA_HEADS):
        p, hl = hh // 2, hh % 2
        blk = st[:, p, hl * GLA_DV:(hl + 1) * GLA_DV, hl * GLA_DK:(hl + 1) * GLA_DK]
        heads.append(jnp.swapaxes(blk, 1, 2))
    return jnp.stack(heads, axis=1)


def _rw_state_from_groups(s):
    heads = []
    for hh in range(RW_HEADS):
        g, hl = hh // RW_GROUP, hh % RW_GROUP
        heads.append(s[:, g, hl * RW_N:(hl + 1) * RW_N, hl * RW_N:(hl + 1) * RW_N])
    return jnp.stack(heads, axis=1)


def _prompt_tiles(t):
    tb = 512 if t % 512 == 0 else CHUNK
    tm = 1024 if t % 1024 == 0 else tb
    return tb, tm


def _prompt_trunk(x, mods, layers, final_g):
    b, t, _ = x.shape
    tb, tm = _prompt_tiles(t)
    states = []
    n_layers = len(layers)
    for l, lp in enumerate(layers):
        mod = mods[l]
        o_dn, conv_new, dn_new = _dn_prompt(x, mod, lp["ng"], lp["w_dn"], lp["cw"], lp["gp"], lp["dng"], tb)
        o_gla, gla_st = _gla_prompt(x, mod, lp["ng"], lp["w_gla"], lp["wf"], lp["bf"], lp["gng"], tb)
        o_rw, rs_new, rw_st = _rw_prompt(x, mod, lp["ng"], lp["w_rw"], lp["mu"], lp["rp"], lp["w2"], lp["a2"], tb)
        x = _outproj(o_dn, o_gla, o_rw, x, mod[:, 2:3, :], lp["w_out"], final_g, l == n_layers - 1, tm)
        states.append((conv_new, dn_new, _gla_state_from_pairs(gla_st), rs_new[:, 0, :],
                       _rw_state_from_groups(rw_st)))
    return x, states


def _sample_trunk(x, mods, layers, final_g, conv_s, dn_s, gla_s, rs_s, rw_s):
    n = x.shape[0]
    x = x.reshape(n, D_MODEL)
    st_dn = st_gla = st_rw = None
    n_layers = len(layers)
    for l, lp in enumerate(layers):
        shift, scale, gate = mods[l][:, 0, :], mods[l][:, 1, :], mods[l][:, 2, :]
        p_dn = _dec_proj(x, shift, scale, lp["ng"], lp["w_dn"], 0, DN_SEG)
        p_gla = _dec_proj(x, shift, scale, lp["ng"], lp["w_gla"], GLA_ROLL, GLA_SEG)
        p_rw = _dec_proj(x, shift, scale, lp["ng"], lp["w_rw"], RW_ROLL, RW_SEG)
        o_dn, st_dn = _dn_dec(p_dn, conv_s, dn_s, lp["cw"], lp["gp"], lp["dng"], l, st_dn)
        o_gla, st_gla = _gla_dec(p_gla, gla_s, lp["wf"], lp["bf"], lp["gng"], l, st_gla)
        o_rw, st_rw = _rw_dec(p_rw, rs_s, rw_s, lp["mu"], lp["rp"], lp["w2"], lp["a2"], l, st_rw)
        x = _outproj(o_dn[None], o_gla[None], o_rw[None], x[None], gate[None], lp["w_out"], final_g,
                     l == n_layers - 1, n)[0]
    return x.reshape(n, 1, D_MODEL), (st_dn[0], st_dn[1], st_gla[0], st_rw[0], st_rw[1])


def kernel(x_prompt, x_sample, c_prompt, c_sample, state_dn_conv, state_dn, state_gla, state_rwkv_shift, state_rwkv, norm_g, ada_w, ada_b, w_in, dn_conv_w, dn_a_log, dn_dt_bias, dn_norm_g, gla_wf, gla_bf, gla_norm_g, rw_mu, rw_w0, rw_w2, rw_a0, rw_a2, rw_k_k, rw_k_a, rw_r_k, rw_ln_w, rw_ln_b, w_out, final_norm_g):
    depth = w_in.shape[0]
    bp = x_prompt.shape[0]
    layers = [_layer_params(l, norm_g, w_in, dn_conv_w, dn_a_log, dn_dt_bias, dn_norm_g, gla_wf, gla_bf,
                            gla_norm_g, rw_mu, rw_w0, rw_w2, rw_a0, rw_a2, rw_k_k, rw_k_a, rw_r_k,
                            rw_ln_w, rw_ln_b, w_out) for l in range(depth)]
    final_g = final_norm_g[None]
    mod_all = _modulation(jnp.concatenate([c_prompt, c_sample], axis=0), ada_w, ada_b)
    mod_all = mod_all.reshape(depth, -1, 3, D_MODEL)
    mods_p = [mod_all[l, :bp] for l in range(depth)]
    mods_s = [mod_all[l, bp:] for l in range(depth)]

    y_p, st_p = _prompt_trunk(x_prompt, mods_p, layers, final_g)
    y_s, st_s = _sample_trunk(x_sample, mods_s, layers, final_g, state_dn_conv, state_dn, state_gla,
                              state_rwkv_shift, state_rwkv)
    stack = lambda sts, i: jnp.stack([s[i] for s in sts])
    return (y_p, y_s,
            stack(st_p, 0), stack(st_p, 1), stack(st_p, 2), stack(st_p, 3), stack(st_p, 4),
            *st_s)
```

```python
import functools

import jax
import jax.numpy as jnp
from jax import lax
from jax.experimental import pallas as pl
from jax.experimental.pallas import tpu as pltpu

F32 = jnp.float32
BF16 = jnp.bfloat16

D_MODEL = 1024
EPS = 1e-6
CHUNK = 64
SUB = 16
NEG = -1e30

DN_HEADS, DN_DK, DN_DV = 4, 128, 128
GLA_HEADS, GLA_DK, GLA_DV, GLA_RANK, GLA_TAU = 4, 64, 128, 16, 16.0
RW_HEADS, RW_N = 8, 64
RW_GROUP = 4
RW_LN_EPS = 64e-5
DN_QKV = 1536
RW_C = 512
RW_SHIFT = 1664
DN_SEG, GLA_SEG, RW_SEG = 2176, 1664, 2176
DN_PROJ, GLA_PROJ, D_PROJ = 2056, 1552, 5784
GLA_WIN0, RW_WIN0 = (DN_PROJ // 128) * 128, ((DN_PROJ + GLA_PROJ) // 128) * 128
GLA_ROLL, RW_ROLL = DN_PROJ - GLA_WIN0, DN_PROJ + GLA_PROJ - RW_WIN0
GLA_WIN, RW_WIN = GLA_SEG + 128, RW_SEG + 128
LANE = 128
VMEM_LIMIT = 56 * 1024 * 1024


def _dot(a, b):
    return jnp.dot(a.astype(BF16), b.astype(BF16), preferred_element_type=F32)


def _dot_nt(a, b):
    return lax.dot_general(a.astype(BF16), b.astype(BF16), (((1,), (1,)), ((), ())),
                           preferred_element_type=F32)


def _dot_tn(a, b):
    return lax.dot_general(a.astype(BF16), b.astype(BF16), (((0,), (0,)), ((), ())),
                           preferred_element_type=F32)


def _sigmoid(x):
    return jax.nn.sigmoid(x)


def _silu(x):
    return x * jax.nn.sigmoid(x)


def _softplus(x):
    return jnp.maximum(x, 0.0) + jnp.log1p(jnp.exp(-jnp.abs(x)))


def _iota2(shape, dim):
    return lax.broadcasted_iota(jnp.int32, shape, dim)


def _modulated_norm(x, ng, shift, scale):
    ms = jnp.mean(x * x, axis=-1, keepdims=True)
    return x * lax.rsqrt(ms + EPS) * ng * (1.0 + scale) + shift


def _l2norm_rows(x):
    return x * lax.rsqrt(jnp.sum(x * x, axis=-1, keepdims=True) + EPS)


def _rms_rows(x):
    return x * lax.rsqrt(jnp.mean(x * x, axis=-1, keepdims=True) + EPS)


def _seg_sum(x, seg_ones):
    rows, width = x.shape
    hi = x.astype(BF16)
    lo = (x - hi.astype(F32)).astype(BF16)
    out = []
    for j in range(width // seg_ones.shape[0]):
        sl = slice(j * seg_ones.shape[0], (j + 1) * seg_ones.shape[0])
        t = jnp.dot(jnp.concatenate([hi[:, sl], lo[:, sl]], axis=0), seg_ones, preferred_element_type=F32)
        out.append(t[0:rows] + t[rows:])
    return jnp.concatenate(out, axis=1)


def _split3(x):
    hi = x.astype(BF16)
    rest = x - hi.astype(F32)
    mid = rest.astype(BF16)
    return hi, mid, (rest - mid.astype(F32)).astype(BF16)


def _chunk_cumsum(x):
    r, c = _iota2((CHUNK, 3 * CHUNK), 0), _iota2((CHUNK, 3 * CHUNK), 1) & (CHUNK - 1)
    tri3 = (r >= c).astype(BF16)
    terms = _split3(x)
    out = []
    for c0 in range(0, x.shape[0], CHUNK):
        stacked = jnp.concatenate([t[c0:c0 + CHUNK] for t in terms], axis=0)
        out.append(jnp.dot(tri3, stacked, preferred_element_type=F32))
    return jnp.concatenate(out, axis=0)


def _bd_masks(n=CHUNK):
    r, c = _iota2((n, n), 0), _iota2((n, n), 1)
    same = (r >> 6) == (c >> 6)
    return dict(
        causal=same & (r >= c), strict=same & (r > c), eye=(r == c).astype(F32), same=same,
        bd16=(r >> 4) == (c >> 4),
        off32=((r >> 5) == (c >> 5)) & ((r >> 4) != (c >> 4)),
        off64=same & ((r >> 5) != (c >> 5)))


def _expand_rows(a, n_heads, shift):
    head = _iota2(a.shape, 1) >> shift
    zero = jnp.zeros_like(a)
    return jnp.concatenate([jnp.where(head == hh, a, zero) for hh in range(n_heads)], axis=0)


def _sbs_masks(n_heads):
    shape = (CHUNK, n_heads * CHUNK)
    r, c = _iota2(shape, 0), _iota2(shape, 1) & (CHUNK - 1)
    return dict(
        causal=r >= c, strict=r > c, eye=(r == c).astype(F32),
        bd16=(r >> 4) == (c >> 4),
        off32=((r >> 5) == (c >> 5)) & ((r >> 4) != (c >> 4)),
        off64=(r >> 5) != (c >> 5))


def _unit_lower_inverse(Ls, m, n_heads):
    bd = lambda a: _expand_rows(a.astype(BF16), n_heads, 6)
    n = range(len(Ls))
    Ld = [jnp.where(m["bd16"], L, 0.0) for L in Ls]
    X = [m["eye"] - a for a in Ld]
    P = [_dot(a, bd(a)) for a in Ld]
    for _ in range(2):
        t = [_dot(jnp.concatenate([P[i], X[i]], axis=0), bd(P[i])) for i in n]
        P = [a[0:CHUNK] for a in t]
        X = [X[i] + t[i][CHUNK:] for i in n]
    X = [X[i] + _dot(X[i], bd(P[i])) for i in n]
    for off in ("off32", "off64"):
        Y = [_dot(X[i], bd(jnp.where(m[off], Ls[i], 0.0))) for i in n]
        X = [X[i] - _dot(Y[i], bd(X[i])) for i in n]
    return X


def _shift_lanes_left(x, shift):
    return pltpu.roll(x, x.shape[1] - shift, 1) if shift else x


def _load_weight_window(w_ref, w_s, shift):
    width = w_s.shape[1]
    for r0 in range(0, w_ref.shape[0], 256):
        blk = _shift_lanes_left(w_ref[r0:r0 + 256, :].astype(F32), shift)
        w_s[r0:r0 + 256, :] = blk[:, 0:width].astype(BF16)


def _col_from_row(row, n, lane_off=0):
    w = row.shape[1]
    sel = (_iota2((n, w), 0) + lane_off) == _iota2((n, w), 1)
    return jnp.sum(jnp.where(sel, row, 0.0), axis=1, keepdims=True)


def _params(n_axes):
    return pltpu.CompilerParams(dimension_semantics=("arbitrary",) * n_axes,
                                vmem_limit_bytes=VMEM_LIMIT)


def _mod_kernel(c_ref, w_ref, b_ref, o_ref):
    o_ref[0] = _dot(_silu(c_ref[...]), w_ref[0]) + b_ref[0]


def _modulation(c_all, ada_w, ada_b):
    depth, d, n = ada_w.shape
    rows = c_all.shape[0]
    tn = 1024
    return pl.pallas_call(
        _mod_kernel,
        grid=(depth, n // tn),
        in_specs=[pl.BlockSpec((rows, d), lambda l, j: (0, 0)),
                  pl.BlockSpec((1, d, tn), lambda l, j: (l, 0, j)),
                  pl.BlockSpec((1, 1, tn), lambda l, j: (l, 0, j))],
        out_specs=pl.BlockSpec((1, rows, tn), lambda l, j: (l, 0, j)),
        out_shape=jax.ShapeDtypeStruct((depth, rows, n), F32),
        compiler_params=_params(2),
        name="adaln_mod",
    )(c_all, ada_w, ada_b.reshape(depth, 1, n))


def _dn_prompt_kernel(x_ref, mod_ref, ng_ref, w_ref, cw_ref, gp_ref, dng_ref,
                      o_ref, h_ref, conv_ref, s_ref,
                      pbuf, qs, ks, vs, gb, gcs, os_, U_s, W_s, Q_s, A_s, K_s, E_s, S):
    t = pl.program_id(1)
    tb = x_ref.shape[1]
    nb = DN_HEADS * CHUNK
    m = _sbs_masks(DN_HEADS)
    lane_c = _iota2((CHUNK, LANE), 1)
    sbs_head = _iota2((CHUNK, nb), 1) >> 6
    gate_lanes = ((_iota2((8, LANE), 1) >= DN_HEADS) & (_iota2((8, LANE), 1) < 2 * DN_HEADS)).astype(BF16)
    heads = range(DN_HEADS)

    def stack(f):
        return jnp.concatenate([f(hh) for hh in heads], axis=0)

    def stack_heads(a):
        return stack(lambda hh: a[:, hh * DN_DK:(hh + 1) * DN_DK])

    def per_head_lanes(col_of):
        return jnp.concatenate([jnp.broadcast_to(col_of(hh), (CHUNK, DN_DK)) for hh in heads], axis=1)

    @pl.when(t == 0)
    def _():
        pbuf[0:8, :] = jnp.zeros((8, DN_SEG), F32)
        S[...] = jnp.zeros_like(S)

    h_ref[0] = _modulated_norm(x_ref[0], ng_ref[...], mod_ref[0, 0:1, :], mod_ref[0, 1:2, :]).astype(BF16)
    pbuf[8:8 + tb, :] = _dot(h_ref[0], w_ref[...])

    conv = pbuf[5:5 + tb, 0:DN_QKV] * cw_ref[0:1, :]
    for j in range(1, 4):
        conv = conv + pbuf[5 + j:5 + j + tb, 0:DN_QKV] * cw_ref[j:j + 1, :]
    conv = _silu(conv)
    for hh in range(DN_HEADS):
        lo = hh * DN_DK
        qs[:, lo:lo + DN_DK] = _l2norm_rows(conv[:, lo:lo + DN_DK]) * (DN_DK ** -0.5)
        ks[:, lo:lo + DN_DK] = _l2norm_rows(conv[:, 512 + lo:512 + lo + DN_DK])
    vs[...] = conv[:, 1024:DN_QKV]
    gseg = pbuf[8:8 + tb, 2048:DN_SEG]
    lane = _iota2((tb, LANE), 1)
    gb[...] = jnp.where(lane < DN_HEADS, _sigmoid(gseg),
                        -jnp.exp(gp_ref[0:1, :]) * _softplus(gseg + gp_ref[1:2, :]))

    gcs[...] = _chunk_cumsum(gb[...])

    def chunk_matrices(c):
        rows = pl.ds(c * CHUNK, CHUNK)
        gbc = gb[rows, :]
        gc = gcs[rows, :]
        g_of = lambda hh: gc[:, DN_HEADS + hh:DN_HEADS + hh + 1]
        g_col = jnp.zeros((CHUNK, nb), F32)
        for hh in heads:
            g_col = jnp.where(sbs_head == hh, g_of(hh), g_col)
        g_sel = stack(lambda hh: jnp.where(lane_c == DN_HEADS + hh, gc, 0.0))
        g_row = lax.dot_general(jnp.concatenate([gate_lanes] * 3, axis=1),
                                jnp.concatenate(_split3(g_sel), axis=1),
                                (((1,), (1,)), ((), ())), preferred_element_type=F32)[0:1, :]
        decay = jnp.exp(jnp.where(m["causal"], g_col - g_row, NEG))
        k_all = ks[rows, :]
        q_all = qs[rows, :]
        beta_l = per_head_lanes(lambda hh: gbc[:, hh:hh + 1])
        g_l = per_head_lanes(g_of)
        eg_l = jnp.exp(g_l)
        kb_all = k_all * beta_l
        expand = lambda a: _expand_rows(a.astype(BF16), DN_HEADS, 7)
        kq = _dot_nt(jnp.concatenate([kb_all, q_all], axis=0), expand(k_all))
        L = jnp.where(m["strict"], kq[0:CHUNK] * decay, 0.0)
        A_s[c] = _expand_rows((kq[CHUNK:2 * CHUNK] * decay).astype(BF16), DN_HEADS, 6)
        Q_s[c] = stack_heads(q_all * eg_l).astype(BF16)
        K_s[c] = stack_heads(k_all * jnp.exp(g_l[CHUNK - 1:CHUNK, :] - g_l)).astype(BF16)
        E_s[c] = jnp.exp(gc[CHUNK - 8:CHUNK, :])
        return L, expand(vs[rows, :] * beta_l), expand(kb_all * eg_l)

    n_chunks = tb // CHUNK
    pre = [chunk_matrices(c) for c in range(n_chunks)]
    inv = _unit_lower_inverse([p[0] for p in pre], m, DN_HEADS)
    for c in range(n_chunks):
        X = inv[c].astype(BF16)
        U_s[c] = stack_heads(_dot(X, pre[c][1]))
        W_s[c] = stack_heads(_dot(X, pre[c][2])).astype(BF16)

    def recurrence(c, carry):
        rows = pl.ds(pl.multiple_of(c * CHUNK, CHUNK), CHUNK)
        w, qe, kh = W_s[c], Q_s[c], K_s[c]
        e_last = E_s[c][7:8, :]
        ws = []
        for hh in heads:
            sl = slice(hh * CHUNK, (hh + 1) * CHUNK)
            ws.append(jnp.dot(jnp.concatenate([w[sl], qe[sl]], axis=0), S[hh].astype(BF16),
                              preferred_element_type=F32))
        v_new = U_s[c] - jnp.concatenate([x[0:CHUNK] for x in ws], axis=0)
        vb = v_new.astype(BF16)
        o_st = jnp.concatenate([x[CHUNK:2 * CHUNK] for x in ws], axis=0) + jnp.dot(
            A_s[c], vb, preferred_element_type=F32)
        for hh in heads:
            sl = slice(hh * CHUNK, (hh + 1) * CHUNK)
            S[hh] = S[hh] * e_last[:, DN_HEADS + hh:DN_HEADS + hh + 1] + _dot_tn(kh[sl], vb[sl])
            os_[rows, hh * DN_DV:(hh + 1) * DN_DV] = o_st[sl]
        return carry

    lax.fori_loop(0, n_chunks, recurrence, 0, unroll=True)

    for hh in heads:
        lo = hh * DN_DV
        z = pbuf[8:8 + tb, DN_QKV + lo:DN_QKV + lo + DN_DV]
        o_ref[0, :, lo:lo + DN_DV] = (_rms_rows(os_[:, lo:lo + DN_DV]) * dng_ref[...] * _silu(z)).astype(o_ref.dtype)

    pbuf[0:8, :] = pbuf[tb:tb + 8, :]

    @pl.when(t == pl.num_programs(1) - 1)
    def _():
        conv_ref[0] = pbuf[tb + 5:tb + 8, 0:DN_QKV]
        s_ref[0] = S[...]


def _dn_prompt(x, mod, ng, w_seg, cw, gp, dng, tb):
    b, t, d = x.shape
    nch, nb = tb // CHUNK, DN_HEADS * CHUNK
    return pl.pallas_call(
        _dn_prompt_kernel,
        grid=(b, t // tb),
        in_specs=[pl.BlockSpec((1, tb, d), lambda i, j: (i, j, 0)),
                  pl.BlockSpec((1, 3, d), lambda i, j: (i, 0, 0)),
                  pl.BlockSpec((1, d), lambda i, j: (0, 0)),
                  pl.BlockSpec((d, DN_SEG), lambda i, j: (0, 0)),
                  pl.BlockSpec((4, DN_QKV), lambda i, j: (0, 0)),
                  pl.BlockSpec((2, LANE), lambda i, j: (0, 0)),
                  pl.BlockSpec((1, DN_DV), lambda i, j: (0, 0))],
        out_specs=[pl.BlockSpec((1, tb, 512), lambda i, j: (i, j, 0)),
                   pl.BlockSpec((1, tb, d), lambda i, j: (i, j, 0)),
                   pl.BlockSpec((1, 3, DN_QKV), lambda i, j: (i, 0, 0)),
                   pl.BlockSpec((1, DN_HEADS, DN_DK, DN_DV), lambda i, j: (i, 0, 0, 0))],
        out_shape=[jax.ShapeDtypeStruct((b, t, 512), BF16),
                   jax.ShapeDtypeStruct((b, t, d), BF16),
                   jax.ShapeDtypeStruct((b, 3, DN_QKV), F32),
                   jax.ShapeDtypeStruct((b, DN_HEADS, DN_DK, DN_DV), F32)],
        scratch_shapes=[pltpu.VMEM((tb + 8, DN_SEG), F32),
                        pltpu.VMEM((tb, 512), F32), pltpu.VMEM((tb, 512), F32),
                        pltpu.VMEM((tb, 512), F32), pltpu.VMEM((tb, LANE), F32),
                        pltpu.VMEM((tb, LANE), F32), pltpu.VMEM((tb, 512), F32),
                        pltpu.VMEM((nch, nb, DN_DV), F32), pltpu.VMEM((nch, nb, DN_DV), BF16),
                        pltpu.VMEM((nch, nb, DN_DK), BF16), pltpu.VMEM((nch, nb, nb), BF16),
                        pltpu.VMEM((nch, nb, DN_DK), BF16), pltpu.VMEM((nch, 8, LANE), F32),
                        pltpu.VMEM((DN_HEADS, DN_DK, DN_DV), F32)],
        compiler_params=_params(2),
        name="dn_prompt",
    )(x, mod, ng, w_seg, cw, gp, dng)


def _gla_prompt_kernel(h_ref, w_ref, wf_ref, bf_ref, gng_ref,
                       o_ref, s_ref,
                       w_s, pbuf, lfs, os_, QE_s, KH_s, EG_s, ST):
    t = pl.program_id(1)
    tb = h_ref.shape[1]
    m = _bd_masks()

    @pl.when((pl.program_id(0) == 0) & (t == 0))
    def _():
        _load_weight_window(w_ref, w_s, GLA_ROLL)

    n_sub = CHUNK // SUB
    lane = _iota2((CHUNK, LANE), 1)
    head1 = lane >= GLA_DK
    rowblk = _iota2((CHUNK, LANE), 0) >> 4
    rowloc = _iota2((CHUNK, LANE), 0) & (SUB - 1)
    er = _iota2((2 * SUB, SUB * LANE), 0)
    ec = _iota2((2 * SUB, SUB * LANE), 1)
    eblk = (((ec >> 7) == (er & (SUB - 1))) & (((ec & (LANE - 1)) >= GLA_DK) == (er >= SUB))).astype(BF16)
    st_r = _iota2((2 * GLA_DV, LANE), 0)
    st_c = _iota2((2 * GLA_DV, LANE), 1)
    st_mask = (st_r >= GLA_DV) == (st_c >= GLA_DK)

    @pl.when(t == 0)
    def _():
        ST[...] = jnp.zeros_like(ST)

    pbuf[...] = _dot(h_ref[0], w_s[...])
    pre = _dot(pbuf[:, 1536:GLA_SEG], wf_ref[...]) + bf_ref[...]
    lfs[...] = (jnp.minimum(pre, 0.0) - jnp.log1p(jnp.exp(-jnp.abs(pre)))) * (1.0 / GLA_TAU)

    lfs[...] = _chunk_cumsum(lfs[...])

    def intra_matrices(c, p):
        rows = pl.ds(c * CHUNK, CHUNK)
        lo = p * LANE
        G = lfs[rows, lo:lo + LANE]
        q = pbuf[rows, lo:lo + LANE] * (GLA_DK ** -0.5)
        k = pbuf[rows, 256 + lo:256 + lo + LANE]
        G3 = G.reshape(n_sub, SUB, LANE)
        q3 = q.reshape(n_sub, SUB, LANE)

        def pick(a3, i):
            return jnp.broadcast_to(a3[:, i:i + 1, :], (n_sub, SUB, LANE)).reshape(CHUNK, LANE)

        q_t = q * jnp.exp(G - pick(G3, 0))
        q_parts, k_parts = [], []
        for i in range(1, n_sub):
            q_parts.append(jnp.where(rowblk == i, q_t, 0.0))
            ref_i = G[i * SUB:i * SUB + 1, :]
            k_parts.append(k * jnp.exp(jnp.where(rowblk < i, ref_i - G, NEG)))
        q_big = jnp.concatenate(q_parts, axis=1)
        k_big = jnp.concatenate(k_parts, axis=1)
        head1_big = jnp.concatenate([head1] * (n_sub - 1), axis=1)
        q_both = jnp.concatenate([jnp.where(head1_big, 0.0, q_big),
                                  jnp.where(head1_big, q_big, 0.0)], axis=0)
        a_off = _dot_nt(q_both, k_big)

        slabs = []
        for i in range(SUB):
            dec = jnp.exp(jnp.where(rowloc <= i, pick(G3, i) - G, NEG))
            slabs.append((k * pick(q3, i) * dec).astype(BF16))
        a_diag = lax.dot_general(eblk, jnp.concatenate(slabs, axis=1), (((1,), (1,)), ((), ())),
                                 preferred_element_type=F32)

        g_end = G[CHUNK - 1:CHUNK, :]
        QE_s[c, p] = (q * jnp.exp(G)).astype(BF16)
        KH_s[c, p] = (k * jnp.exp(g_end - G)).astype(BF16)
        EG_s[c, p] = jnp.exp(G[CHUNK - 8:CHUNK, :])
        return a_off, a_diag

    def intra_output(c, p, a_off, a_diag):
        rows = pl.ds(c * CHUNK, CHUNK)
        for hl in range(2):
            hh = 2 * p + hl
            a_d = a_diag[hl * SUB:(hl + 1) * SUB, :]
            a_h = a_off[hl * CHUNK:(hl + 1) * CHUNK, :] + jnp.where(
                m["bd16"], jnp.concatenate([a_d] * n_sub, axis=0), 0.0)
            os_[rows, hh * GLA_DV:(hh + 1) * GLA_DV] = _dot(a_h, pbuf[rows, 512 + hh * GLA_DV:512 + (hh + 1) * GLA_DV])

    n_chunks = tb // CHUNK
    problems = [(c, p) for c in range(n_chunks) for p in range(GLA_HEADS // 2)]
    mats = [intra_matrices(c, p) for c, p in problems]
    for (c, p), ab in zip(problems, mats):
        intra_output(c, p, *ab)

    def recurrence(c, carry):
        rows = pl.ds(pl.multiple_of(c * CHUNK, CHUNK), CHUNK)
        pairs = range(GLA_HEADS // 2)
        st = [ST[p] for p in pairs]
        o_inter = [_dot_nt(QE_s[c, p], st[p]) for p in pairs]
        kv = [_dot_tn(pbuf[rows, 512 + 2 * p * LANE:512 + 2 * (p + 1) * LANE], KH_s[c, p]) for p in pairs]
        for p in pairs:
            ST[p] = st[p] * EG_s[c, p][7:8, :] + jnp.where(st_mask, kv[p], 0.0)
            sl = slice(2 * p * LANE, 2 * (p + 1) * LANE)
            os_[rows, sl] = os_[rows, sl] + o_inter[p]
        return carry

    lax.fori_loop(0, n_chunks, recurrence, 0, unroll=True)

    for hh in range(GLA_HEADS):
        sl = slice(hh * GLA_DV, (hh + 1) * GLA_DV)
        z = pbuf[:, 1024 + hh * GLA_DV:1024 + (hh + 1) * GLA_DV]
        o_ref[0, :, sl] = (_rms_rows(os_[:, sl]) * gng_ref[...] * _silu(z)).astype(o_ref.dtype)

    @pl.when(t == pl.num_programs(1) - 1)
    def _():
        s_ref[0] = ST[...]


def _gla_prompt(h, w_seg, wf, bf, gng, tb):
    b, t, d = h.shape
    return pl.pallas_call(
        _gla_prompt_kernel,
        grid=(b, t // tb),
        in_specs=[pl.BlockSpec((1, tb, d), lambda i, j: (i, j, 0)),
                  pl.BlockSpec((d, GLA_WIN), lambda i, j: (0, 0)),
                  pl.BlockSpec((LANE, 256), lambda i, j: (0, 0)),
                  pl.BlockSpec((1, 256), lambda i, j: (0, 0)),
                  pl.BlockSpec((1, GLA_DV), lambda i, j: (0, 0))],
        out_specs=[pl.BlockSpec((1, tb, 512), lambda i, j: (i, j, 0)),
                   pl.BlockSpec((1, 2, 2 * GLA_DV, LANE), lambda i, j: (i, 0, 0, 0))],
        out_shape=[jax.ShapeDtypeStruct((b, t, 512), BF16),
                   jax.ShapeDtypeStruct((b, 2, 2 * GLA_DV, LANE), F32)],
        scratch_shapes=[pltpu.VMEM((d, GLA_SEG), BF16),
                        pltpu.VMEM((tb, GLA_SEG), F32), pltpu.VMEM((tb, 256), F32),
                        pltpu.VMEM((tb, 512), F32),
                        pltpu.VMEM((tb // CHUNK, 2, CHUNK, LANE), BF16),
                        pltpu.VMEM((tb // CHUNK, 2, CHUNK, LANE), BF16),
                        pltpu.VMEM((tb // CHUNK, 2, 8, LANE), F32),
                        pltpu.VMEM((2, 2 * GLA_DV, LANE), F32)],
        compiler_params=_params(2),
        name="gla_prompt",
    )(h, w_seg, wf, bf, gng)


def _rw_features(xm, rp_ref, w2_ref, a2_ref, seg_ones):
    r = xm[:, 0:RW_C]
    k = xm[:, RW_C:2 * RW_C]
    v = xm[:, 2 * RW_C:3 * RW_C]
    lo = xm[:, 3 * RW_C:RW_SHIFT]
    w_log = -_softplus(-(rp_ref[0:1, :] + _dot(jnp.tanh(lo), w2_ref[...]))) - 0.5
    logw = -jnp.exp(w_log)
    a = _sigmoid(rp_ref[1:2, :] + _dot(lo, a2_ref[...]))
    kk = k * rp_ref[2:3, :]
    kk = kk * lax.rsqrt(_seg_sum(kk * kk, seg_ones) + EPS)
    k = k * (1.0 + (a - 1.0) * rp_ref[3:4, :])
    return r, k, v, kk, kk * a, logw


def _rw_finish(o, r, k, v, rz, rp_ref, seg_ones):
    mu = _seg_sum(o, seg_ones) * (1.0 / RW_N)
    dev = o - mu
    var = _seg_sum(dev * dev, seg_ones) * (1.0 / RW_N)
    on = dev * lax.rsqrt(var + RW_LN_EPS) * rp_ref[5:6, :] + rp_ref[6:7, :]
    bonus = _seg_sum(r * k * rp_ref[4:5, :], seg_ones) * v
    return (on + bonus) * _silu(rz)


def _seg_ones():
    n = RW_GROUP * RW_N
    return ((_iota2((n, n), 0) >> 6) == (_iota2((n, n), 1) >> 6)).astype(BF16)


def _rw_prompt_kernel(h_ref, w_ref, mu_ref, rp_ref, w2_ref, a2_ref,
                      o_ref, sh_ref, s_ref,
                      w_s, pbuf, rs, ks, vs, kks, bs, lws, gs, os_,
                      P1_s, KT_s, P2_s, RH_s, AB_s, KB_s, E_s, S):
    t = pl.program_id(1)
    tb = h_ref.shape[1]
    gw = RW_GROUP * RW_N

    @pl.when((pl.program_id(0) == 0) & (t == 0))
    def _():
        _load_weight_window(w_ref, w_s, RW_ROLL)

    m = _sbs_masks(RW_GROUP)
    same_head = (_iota2((gw, gw), 0) >> 6) == (_iota2((gw, gw), 1) >> 6)
    seg_ones = _seg_ones()
    mid = CHUNK // 2

    def expand(a):
        return _expand_rows(a.astype(BF16), RW_GROUP, 6)

    @pl.when(t == 0)
    def _():
        pbuf[0:8, :] = jnp.zeros((8, RW_SEG), F32)
        S[...] = jnp.zeros_like(S)

    pbuf[8:8 + tb, :] = _dot(h_ref[0], w_s[...])
    xs = pbuf[8:8 + tb, 0:RW_SHIFT]
    prev = pbuf[7:7 + tb, 0:RW_SHIFT]
    xm = xs + (prev - xs) * mu_ref[...]
    r, k, v, kk, b, logw = _rw_features(xm, rp_ref, w2_ref, a2_ref, seg_ones)
    rs[...] = r
    ks[...] = k
    vs[...] = v
    kks[...] = kk
    bs[...] = b
    lws[...] = logw
    gs[...] = _chunk_cumsum(logw)

    def chunk_matrices(c, g):
        rows = pl.ds(c * CHUNK, CHUNK)
        sl = slice(g * gw, (g + 1) * gw)
        G = gs[rows, sl]
        Gx = G - lws[rows, sl]
        ref = G[mid:mid + 1, :]
        r_, k_, v_, kk_, b_ = rs[rows, sl], ks[rows, sl], vs[rows, sl], kks[rows, sl], bs[rows, sl]
        e_out = jnp.exp(ref - G)
        lhs = jnp.concatenate([kk_ * jnp.exp(Gx - ref), r_ * jnp.exp(G - ref)], axis=0)
        rhs = jnp.concatenate([expand(b_ * e_out), expand(k_ * e_out)], axis=0)
        mm = _dot_nt(lhs, rhs)
        Lb = jnp.where(m["strict"], mm[0:CHUNK, 0:gw], 0.0)
        Mk = jnp.where(m["strict"], mm[0:CHUNK, gw:], 0.0)
        Arb = jnp.where(m["causal"], mm[CHUNK:, 0:gw], 0.0)
        Ark = jnp.where(m["causal"], mm[CHUNK:, gw:], 0.0)
        kv = _dot(jnp.concatenate([Mk, Ark], axis=0), expand(v_))
        P2_s[c, g] = kv[CHUNK:]
        RH_s[c, g] = (r_ * jnp.exp(G)).astype(BF16)
        AB_s[c, g] = Arb.astype(BF16)
        e_end = jnp.exp(G[CHUNK - 1:CHUNK, :] - G)
        KB_s[c, g] = jnp.concatenate([k_ * e_end, -(b_ * e_end)], axis=0).astype(BF16)
        E_s[c, g] = jnp.exp(G[CHUNK - 8:CHUNK, :])
        return Lb, jnp.concatenate([expand(kv[0:CHUNK]), expand(kk_ * jnp.exp(Gx))], axis=1)

    n_chunks = tb // CHUNK
    problems = [(c, g) for c in range(n_chunks) for g in range(RW_HEADS // RW_GROUP)]
    pre = [chunk_matrices(c, g) for c, g in problems]
    inv = _unit_lower_inverse([p[0] for p in pre], m, RW_GROUP)
    for (c, g), X, p in zip(problems, inv, pre):
        sol = _dot(X, p[1])
        P1_s[c, g] = sol[:, 0:gw]
        KT_s[c, g] = sol[:, gw:].astype(BF16)

    def recurrence(c, carry):
        rows = pl.ds(pl.multiple_of(c * CHUNK, CHUNK), CHUNK)
        groups = range(RW_HEADS // RW_GROUP)
        sls = [slice(g * gw, (g + 1) * gw) for g in groups]
        s_old = [S[g] for g in groups]
        khu = [_dot_nt(jnp.concatenate([KT_s[c, g], RH_s[c, g]], axis=0), s_old[g]) for g in groups]
        u_ = [P1_s[c, g] + khu[g][0:CHUNK] for g in groups]
        au = [_dot(AB_s[c, g], expand(u_[g])) for g in groups]
        upd = [_dot_tn(jnp.concatenate([vs[rows, sls[g]], u_[g]], axis=0), KB_s[c, g]) for g in groups]
        for g in groups:
            S[g] = s_old[g] * E_s[c, g][7:8, :] + jnp.where(same_head, upd[g], 0.0)
            os_[rows, sls[g]] = khu[g][CHUNK:2 * CHUNK] + P2_s[c, g] - au[g]
        return carry

    lax.fori_loop(0, n_chunks, recurrence, 0, unroll=True)

    rz = pbuf[8:8 + tb, RW_SHIFT:RW_SEG]
    o_ref[0] = _rw_finish(os_[...], rs[...], ks[...], vs[...], rz, rp_ref, seg_ones).astype(o_ref.dtype)
    pbuf[0:8, :] = pbuf[tb:tb + 8, :]

    @pl.when(t == pl.num_programs(1) - 1)
    def _():
        sh_ref[0] = pbuf[tb + 7:tb + 8, 0:RW_SHIFT]
        s_ref[0] = S[...]


def _rw_prompt(h, w_seg, mu, rp, w2, a2, tb):
    b, t, d = h.shape
    nch, ng_, gw = tb // CHUNK, RW_HEADS // RW_GROUP, RW_GROUP * RW_N
    return pl.pallas_call(
        _rw_prompt_kernel,
        grid=(b, t // tb),
        in_specs=[pl.BlockSpec((1, tb, d), lambda i, j: (i, j, 0)),
                  pl.BlockSpec((d, RW_WIN), lambda i, j: (0, 0)),
                  pl.BlockSpec((1, RW_SHIFT), lambda i, j: (0, 0)),
                  pl.BlockSpec((8, RW_C), lambda i, j: (0, 0)),
                  pl.BlockSpec((LANE, RW_C), lambda i, j: (0, 0)),
                  pl.BlockSpec((LANE, RW_C), lambda i, j: (0, 0))],
        out_specs=[pl.BlockSpec((1, tb, RW_C), lambda i, j: (i, j, 0)),
                   pl.BlockSpec((1, 1, RW_SHIFT), lambda i, j: (i, 0, 0)),
                   pl.BlockSpec((1, ng_, gw, gw), lambda i, j: (i, 0, 0, 0))],
        out_shape=[jax.ShapeDtypeStruct((b, t, RW_C), BF16),
                   jax.ShapeDtypeStruct((b, 1, RW_SHIFT), F32),
                   jax.ShapeDtypeStruct((b, ng_, gw, gw), F32)],
        scratch_shapes=[pltpu.VMEM((d, RW_SEG), BF16), pltpu.VMEM((tb + 8, RW_SEG), F32)]
        + [pltpu.VMEM((tb, RW_C), F32) for _ in range(8)]
        + [pltpu.VMEM((nch, ng_, CHUNK, gw), F32), pltpu.VMEM((nch, ng_, CHUNK, gw), BF16),
           pltpu.VMEM((nch, ng_, CHUNK, gw), F32), pltpu.VMEM((nch, ng_, CHUNK, gw), BF16),
           pltpu.VMEM((nch, ng_, CHUNK, gw), BF16), pltpu.VMEM((nch, ng_, 2 * CHUNK, gw), BF16),
           pltpu.VMEM((nch, ng_, 8, gw), F32),
           pltpu.VMEM((ng_, gw, gw), F32)],
        compiler_params=_params(2),
        name="rw_prompt",
    )(h, w_seg, mu, rp, w2, a2)


def _outproj_kernel(od_ref, og_ref, or_ref, x_ref, gate_ref, w_ref, fg_ref, y_ref, *, final):
    acc = (_dot(od_ref[0], w_ref[0:512, :]) + _dot(og_ref[0], w_ref[512:1024, :])
           + _dot(or_ref[0], w_ref[1024:1536, :]))
    xn = x_ref[0] + gate_ref[0] * acc
    if final:
        xn = _rms_rows(xn) * fg_ref[...]
    y_ref[0] = xn


def _outproj(o_dn, o_gla, o_rw, x, gate, w_out, fg, final, tm):
    b, t, d = x.shape
    tg = gate.shape[1]
    gspec = (pl.BlockSpec((1, 1, d), lambda i, j: (i, 0, 0)) if tg == 1
             else pl.BlockSpec((1, tm, d), lambda i, j: (i, j, 0)))
    ospec = pl.BlockSpec((1, tm, 512), lambda i, j: (i, j, 0))
    return pl.pallas_call(
        functools.partial(_outproj_kernel, final=final),
        grid=(b, t // tm),
        in_specs=[ospec, ospec, ospec,
                  pl.BlockSpec((1, tm, d), lambda i, j: (i, j, 0)),
                  gspec,
                  pl.BlockSpec((1536, d), lambda i, j: (0, 0)),
                  pl.BlockSpec((1, d), lambda i, j: (0, 0))],
        out_specs=pl.BlockSpec((1, tm, d), lambda i, j: (i, j, 0)),
        out_shape=jax.ShapeDtypeStruct((b, t, d), F32),
        compiler_params=_params(2),
        name="out_proj",
    )(o_dn, o_gla, o_rw, x, gate, w_out, fg)


def _dec_proj_kernel(x_ref, shift_ref, scale_ref, ng_ref, w_ref, o_ref, *, roll):
    h = _modulated_norm(x_ref[...], ng_ref[...], shift_ref[...], scale_ref[...])
    o_ref[...] = _shift_lanes_left(_dot(h, w_ref[...]), roll)[:, 0:o_ref.shape[1]]


def _dec_proj(x, shift, scale, ng, w_win, roll, seg):
    n, d = x.shape
    win = w_win.shape[1]
    full = lambda shape: pl.BlockSpec(shape, lambda i: (0,) * len(shape))
    return pl.pallas_call(
        functools.partial(_dec_proj_kernel, roll=roll),
        grid=(1,),
        in_specs=[full((n, d)), full((n, d)), full((n, d)), full((1, d)), full((d, win))],
        out_specs=full((n, seg)),
        out_shape=jax.ShapeDtypeStruct((n, seg), F32),
        compiler_params=_params(1),
        name="dec_proj",
    )(x, shift, scale, ng, w_win)


def _dn_dec_kernel(p_ref, cst_ref, s_ref, cw_ref, gp_ref, dng_ref,
                   o_ref, cnew_ref, sn_ref, o_scr):
    bt = p_ref.shape[0]
    qkv = p_ref[:, 0:DN_QKV]
    c0, c1, c2 = cst_ref[:, 0, :], cst_ref[:, 1, :], cst_ref[:, 2, :]
    conv = _silu(c0 * cw_ref[0:1, :] + c1 * cw_ref[1:2, :] + c2 * cw_ref[2:3, :] + qkv * cw_ref[3:4, :])
    cnew_ref[:, 0, :] = c1
    cnew_ref[:, 1, :] = c2
    cnew_ref[:, 2, :] = qkv
    gseg = p_ref[:, 2048:DN_SEG]
    beta_all = _sigmoid(gseg)
    a_all = jnp.exp(-jnp.exp(gp_ref[0:1, :]) * _softplus(gseg + gp_ref[1:2, :]))
    for hh in range(DN_HEADS):
        lo = hh * DN_DK
        q = _l2norm_rows(conv[:, lo:lo + DN_DK]) * (DN_DK ** -0.5)
        k = _l2norm_rows(conv[:, 512 + lo:512 + lo + DN_DK])
        v = conv[:, 1024 + lo:1024 + lo + DN_DV]
        for bi in range(bt):
            k_col = _col_from_row(k[bi:bi + 1, :], DN_DK)
            q_col = _col_from_row(q[bi:bi + 1, :], DN_DK)
            a = a_all[bi:bi + 1, DN_HEADS + hh:DN_HEADS + hh + 1]
            beta = beta_all[bi:bi + 1, hh:hh + 1]
            s_old = s_ref[bi, hh]
            sk = jnp.sum(s_old * k_col, axis=0, keepdims=True)
            v_new = beta * (v[bi:bi + 1, :] - a * sk)
            s_new = s_old * a + k_col * v_new
            sn_ref[bi, hh] = s_new
            o_scr[bi:bi + 1, lo:lo + DN_DV] = jnp.sum(s_new * q_col, axis=0, keepdims=True)
    for hh in range(DN_HEADS):
        lo = hh * DN_DV
        z = p_ref[:, DN_QKV + lo:DN_QKV + lo + DN_DV]
        o_ref[:, lo:lo + DN_DV] = _rms_rows(o_scr[:, lo:lo + DN_DV]) * dng_ref[...] * _silu(z)


def _layer_state_call(kernel_fn, name, l, prev, args, in_specs, o_width, tails, bt):
    n = args[0].shape[0]
    depth = next(a.shape[0] for a in args if a.ndim >= 3)
    first = prev is None
    n_in = len(args)

    def spec(t):
        z = (0,) * len(t)
        if first:
            return pl.BlockSpec((depth, bt) + t, lambda i: (0, i) + z)
        return pl.BlockSpec((None, bt) + t, lambda i: (l, i) + z)

    def body(*refs):
        ins, rest = refs[:n_in], list(refs[n_in if first else n_in + len(tails):])
        if first:
            for j in range(1, 1 + len(tails)):
                if depth > 1:
                    rest[j][1:] = jnp.zeros((depth - 1,) + rest[j].shape[1:], F32)
                rest[j] = rest[j].at[0]
        kernel_fn(*ins, *rest)

    aliases = {}
    if not first:
        aliases = {n_in + j: 1 + j for j in range(len(tails))}
        in_specs = list(in_specs) + [pl.BlockSpec(memory_space=pl.ANY)] * len(tails)
        args = list(args) + list(prev)
    out = pl.pallas_call(
        body,
        grid=(n // bt,),
        in_specs=in_specs,
        out_specs=[pl.BlockSpec((bt, o_width), lambda i: (i, 0))] + [spec(t) for t in tails],
        out_shape=[jax.ShapeDtypeStruct((n, o_width), F32)]
        + [jax.ShapeDtypeStruct((depth, n) + t, F32) for t in tails],
        scratch_shapes=[pltpu.VMEM((bt, o_width), F32)],
        input_output_aliases=aliases,
        compiler_params=_params(1),
        name=name,
    )(*args)
    return out[0], tuple(out[1:])


def _dn_dec(p, cst_all, s_all, cw, gp, dng, l, prev, bt=8):
    in_specs = [pl.BlockSpec((bt, DN_SEG), lambda i: (i, 0)),
                pl.BlockSpec((None, bt, 3, DN_QKV), lambda i: (l, i, 0, 0)),
                pl.BlockSpec((None, bt, DN_HEADS, DN_DK, DN_DV), lambda i: (l, i, 0, 0, 0)),
                pl.BlockSpec((4, DN_QKV), lambda i: (0, 0)),
                pl.BlockSpec((2, LANE), lambda i: (0, 0)),
                pl.BlockSpec((1, DN_DV), lambda i: (0, 0))]
    return _layer_state_call(_dn_dec_kernel, "dn_dec", l, prev, [p, cst_all, s_all, cw, gp, dng], in_specs,
                             512, [(3, DN_QKV), (DN_HEADS, DN_DK, DN_DV)], bt)


def _gla_dec_kernel(p_ref, s_ref, wf_ref, bf_ref, gng_ref, o_ref, sn_ref, o_scr):
    bt = p_ref.shape[0]
    pre = _dot(p_ref[:, 1536:GLA_SEG], wf_ref[...]) + bf_ref[...]
    alpha = jnp.exp((jnp.minimum(pre, 0.0) - jnp.log1p(jnp.exp(-jnp.abs(pre)))) * (1.0 / GLA_TAU))
    q_all = p_ref[:, 0:256] * (GLA_DK ** -0.5)
    k_all = p_ref[:, 256:512]
    for hh in range(GLA_HEADS):
        pair, off = (hh // 2) * LANE, (hh % 2) * GLA_DK
        v = p_ref[:, 512 + hh * GLA_DV:512 + (hh + 1) * GLA_DV]
        for bi in range(bt):
            a_col = _col_from_row(alpha[bi:bi + 1, pair:pair + LANE], GLA_DK, off)
            k_col = _col_from_row(k_all[bi:bi + 1, pair:pair + LANE], GLA_DK, off)
            q_col = _col_from_row(q_all[bi:bi + 1, pair:pair + LANE], GLA_DK, off)
            s_new = s_ref[bi, hh] * a_col + k_col * v[bi:bi + 1, :]
            sn_ref[bi, hh] = s_new
            o_scr[bi:bi + 1, hh * GLA_DV:(hh + 1) * GLA_DV] = jnp.sum(s_new * q_col, axis=0, keepdims=True)
    for hh in range(GLA_HEADS):
        lo = hh * GLA_DV
        z = p_ref[:, 1024 + lo:1024 + lo + GLA_DV]
        o_ref[:, lo:lo + GLA_DV] = _rms_rows(o_scr[:, lo:lo + GLA_DV]) * gng_ref[...] * _silu(z)


def _gla_dec(p, s_all, wf, bf, gng, l, prev, bt=8):
    in_specs = [pl.BlockSpec((bt, GLA_SEG), lambda i: (i, 0)),
                pl.BlockSpec((None, bt, GLA_HEADS, GLA_DK, GLA_DV), lambda i: (l, i, 0, 0, 0)),
                pl.BlockSpec((LANE, 256), lambda i: (0, 0)),
                pl.BlockSpec((1, 256), lambda i: (0, 0)),
                pl.BlockSpec((1, GLA_DV), lambda i: (0, 0))]
    return _layer_state_call(_gla_dec_kernel, "gla_dec", l, prev, [p, s_all, wf, bf, gng], in_specs,
                             512, [(GLA_HEADS, GLA_DK, GLA_DV)], bt)


def _rw_dec_kernel(p_ref, sh_ref, s_ref, mu_ref, rp_ref, w2_ref, a2_ref,
                   o_ref, shn_ref, sn_ref, o_scr):
    bt = p_ref.shape[0]
    seg_ones = _seg_ones()
    xs = p_ref[:, 0:RW_SHIFT]
    shn_ref[...] = xs
    xm = xs + (sh_ref[...] - xs) * mu_ref[...]
    r, k, v, kk, b, logw = _rw_features(xm, rp_ref, w2_ref, a2_ref, seg_ones)
    w = jnp.exp(logw)
    ones = jnp.ones((RW_N, RW_N), BF16)
    eye = _iota2((RW_N, RW_N), 0) == _iota2((RW_N, RW_N), 1)
    seqs = range(bt)

    def row_sums(a):
        hi = a.astype(BF16)
        lo = (a - hi.astype(F32)).astype(BF16)
        return (jnp.dot(hi, ones, preferred_element_type=F32) + jnp.dot(lo, ones, preferred_element_type=F32))

    for hh in range(RW_HEADS):
        sl = slice(hh * RW_N, (hh + 1) * RW_N)
        s_old = [s_ref[bi, hh] for bi in seqs]
        red = row_sums(jnp.concatenate([s_old[bi] * kk[bi:bi + 1, sl] for bi in seqs]
                                       + [jnp.where(eye, v[bi:bi + 1, sl], 0.0) for bi in seqs], axis=0))
        s_new = []
        for bi in seqs:
            row = slice(bi, bi + 1)
            skk = red[bi * RW_N:(bi + 1) * RW_N]
            v_col = red[(bt + bi) * RW_N:(bt + bi + 1) * RW_N]
            s_new.append(s_old[bi] * w[row, sl] - skk * b[row, sl] + v_col * k[row, sl])
            sn_ref[bi, hh] = s_new[bi]
        red = row_sums(jnp.concatenate([s_new[bi] * r[bi:bi + 1, sl] for bi in seqs], axis=0))
        for bi in seqs:
            o_scr[bi:bi + 1, sl] = jnp.sum(jnp.where(eye, red[bi * RW_N:(bi + 1) * RW_N], 0.0),
                                           axis=0, keepdims=True)
    rz = p_ref[:, RW_SHIFT:RW_SEG]
    o_ref[...] = _rw_finish(o_scr[...], r, k, v, rz, rp_ref, seg_ones)


def _rw_dec(p, sh_all, s_all, mu, rp, w2, a2, l, prev, bt=8):
    in_specs = [pl.BlockSpec((bt, RW_SEG), lambda i: (i, 0)),
                pl.BlockSpec((None, bt, RW_SHIFT), lambda i: (l, i, 0)),
                pl.BlockSpec((None, bt, RW_HEADS, RW_N, RW_N), lambda i: (l, i, 0, 0, 0)),
                pl.BlockSpec((1, RW_SHIFT), lambda i: (0, 0)),
                pl.BlockSpec((8, RW_C), lambda i: (0, 0)),
                pl.BlockSpec((LANE, RW_C), lambda i: (0, 0)),
                pl.BlockSpec((LANE, RW_C), lambda i: (0, 0))]
    return _layer_state_call(_rw_dec_kernel, "rw_dec", l, prev, [p, sh_all, s_all, mu, rp, w2, a2], in_specs,
                             RW_C, [(RW_SHIFT,), (RW_HEADS, RW_N, RW_N)], bt)


def _layer_params(l, norm_g, w_in, dn_conv_w, dn_a_log, dn_dt_bias, dn_norm_g, gla_wf, gla_bf,
                  gla_norm_g, rw_mu, rw_w0, rw_w2, rw_a0, rw_a2, rw_k_k, rw_k_a, rw_r_k,
                  rw_ln_w, rw_ln_b, w_out):
    w = w_in[l].astype(BF16)
    gp = jnp.zeros((2, LANE), F32)
    gp = gp.at[0, DN_HEADS:2 * DN_HEADS].set(dn_a_log[l]).at[1, DN_HEADS:2 * DN_HEADS].set(dn_dt_bias[l])
    zeros_r = jnp.zeros((1, RW_C), F32)
    rp = jnp.concatenate([rw_w0[l][None], rw_a0[l][None], rw_k_k[l][None], rw_k_a[l][None],
                          rw_r_k[l].reshape(1, RW_C), rw_ln_w[l][None], rw_ln_b[l][None], zeros_r], axis=0)
    return dict(
        ng=norm_g[l][None],
        w_dn=w[:, 0:DN_SEG],
        w_gla=w[:, GLA_WIN0:GLA_WIN0 + GLA_WIN],
        w_rw=jnp.pad(w[:, RW_WIN0:], ((0, 0), (0, RW_WIN - (D_PROJ - RW_WIN0)))),
        cw=dn_conv_w[l], gp=gp, dng=dn_norm_g[l][None],
        wf=jnp.pad(gla_wf[l], ((0, LANE - GLA_RANK), (0, 0))).astype(BF16),
        bf=gla_bf[l][None], gng=gla_norm_g[l][None],
        mu=rw_mu[l][None], rp=rp,
        w2=jnp.pad(rw_w2[l], ((0, LANE - 64), (0, 0))).astype(BF16),
        a2=jnp.pad(rw_a2[l], ((64, 0), (0, 0))).astype(BF16),
        w_out=w_out[l].astype(BF16))


def _gla_state_from_pairs(st):
    heads = []
    for hh in range(GLA_HEADS):
        p, hl = hh // 2, hh % 2
        blk = st[:, p, hl * GLA_DV:(hl + 1) * GLA_DV, hl * GLA_DK:(hl + 1) * GLA_DK]
        heads.append(jnp.swapaxes(blk, 1, 2))
    return jnp.stack(heads, axis=1)


def _rw_state_from_groups(s):
    heads = []
    for hh in range(RW_HEADS):
        g, hl = hh // RW_GROUP, hh % RW_GROUP
        heads.append(s[:, g, hl * RW_N:(hl + 1) * RW_N, hl * RW_N:(hl + 1) * RW_N])
    return jnp.stack(heads, axis=1)


def _prompt_tiles(t):
    tb = 512 if t % 512 == 0 else CHUNK
    tm = 1024 if t % 1024 == 0 else tb
    return tb, tm


def _prompt_trunk(x, mods, layers, final_g):
    b, t, _ = x.shape
    tb, tm = _prompt_tiles(t)
    states = []
    n_layers = len(layers)
    for l, lp in enumerate(layers):
        mod = mods[l]
        o_dn, h, conv_new, dn_new = _dn_prompt(x, mod, lp["ng"], lp["w_dn"], lp["cw"], lp["gp"], lp["dng"], tb)
        o_gla, gla_st = _gla_prompt(h, lp["w_gla"], lp["wf"], lp["bf"], lp["gng"], tb)
        o_rw, rs_new, rw_st = _rw_prompt(h, lp["w_rw"], lp["mu"], lp["rp"], lp["w2"], lp["a2"], tb)
        x = _outproj(o_dn, o_gla, o_rw, x, mod[:, 2:3, :], lp["w_out"], final_g, l == n_layers - 1, tm)
        states.append((conv_new, dn_new, _gla_state_from_pairs(gla_st), rs_new[:, 0, :],
                       _rw_state_from_groups(rw_st)))
    return x, states


def _sample_trunk(x, mods, layers, final_g, conv_s, dn_s, gla_s, rs_s, rw_s):
    n = x.shape[0]
    x = x.reshape(n, D_MODEL)
    st_dn = st_gla = st_rw = None
    n_layers = len(layers)
    for l, lp in enumerate(layers):
        shift, scale, gate = mods[l][:, 0, :], mods[l][:, 1, :], mods[l][:, 2, :]
        p_dn = _dec_proj(x, shift, scale, lp["ng"], lp["w_dn"], 0, DN_SEG)
        p_gla = _dec_proj(x, shift, scale, lp["ng"], lp["w_gla"], GLA_ROLL, GLA_SEG)
        p_rw = _dec_proj(x, shift, scale, lp["ng"], lp["w_rw"], RW_ROLL, RW_SEG)
        o_dn, st_dn = _dn_dec(p_dn, conv_s, dn_s, lp["cw"], lp["gp"], lp["dng"], l, st_dn)
        o_gla, st_gla = _gla_dec(p_gla, gla_s, lp["wf"], lp["bf"], lp["gng"], l, st_gla)
        o_rw, st_rw = _rw_dec(p_rw, rs_s, rw_s, lp["mu"], lp["rp"], lp["w2"], lp["a2"], l, st_rw)
        x = _outproj(o_dn[None], o_gla[None], o_rw[None], x[None], gate[None], lp["w_out"], final_g,
                     l == n_layers - 1, n)[0]
    return x.reshape(n, 1, D_MODEL), (st_dn[0], st_dn[1], st_gla[0], st_rw[0], st_rw[1])


def kernel(x_prompt, x_sample, c_prompt, c_sample, state_dn_conv, state_dn, state_gla, state_rwkv_shift, state_rwkv, norm_g, ada_w, ada_b, w_in, dn_conv_w, dn_a_log, dn_dt_bias, dn_norm_g, gla_wf, gla_bf, gla_norm_g, rw_mu, rw_w0, rw_w2, rw_a0, rw_a2, rw_k_k, rw_k_a, rw_r_k, rw_ln_w, rw_ln_b, w_out, final_norm_g):
    depth = w_in.shape[0]
    bp = x_prompt.shape[0]
    layers = [_layer_params(l, norm_g, w_in, dn_conv_w, dn_a_log, dn_dt_bias, dn_norm_g, gla_wf, gla_bf,
                            gla_norm_g, rw_mu, rw_w0, rw_w2, rw_a0, rw_a2, rw_k_k, rw_k_a, rw_r_k,
                            rw_ln_w, rw_ln_b, w_out) for l in range(depth)]
    final_g = final_norm_g[None]
    mod_all = _modulation(jnp.concatenate([c_prompt, c_sample], axis=0), ada_w, ada_b)
    mod_all = mod_all.reshape(depth, -1, 3, D_MODEL)
    mods_p = [mod_all[l, :bp] for l in range(depth)]
    mods_s = [mod_all[l, bp:] for l in range(depth)]

    y_p, st_p = _prompt_trunk(x_prompt, mods_p, layers, final_g)
    y_s, st_s = _sample_trunk(x_sample, mods_s, layers, final_g, state_dn_conv, state_dn, state_gla,
                              state_rwkv_shift, state_rwkv)
    stack = lambda sts, i: jnp.stack([s[i] for s in sts])
    return (y_p, y_s,
            stack(st_p, 0), stack(st_p, 1), stack(st_p, 2), stack(st_p, 3), stack(st_p, 4),
            *st_s)
```
